```python
import math
import jax
import jax.numpy as jnp
from jax import lax
import numpy as np

D_MODEL = 1024
BATCH = 16
SEQ = 2048
DEPTH = 4
DEC_BATCH = 128
DEC_SEQ = 8
PAST_LEN = 8192
PAGE_SIZE = 128

N_MIXERS = 3
S5_GROUP = 16
N_GROUPS = D_MODEL // S5_GROUP
S5_STATE = 64
S5_CHUNK = 128
N_HEADS = 8
QK_NOPE = 128
QK_ROPE = 64
V_HEAD = 128
Q_RANK = D_MODEL // 2
KV_RANK = D_MODEL // 4
ROPE_THETA = 10000.0
ATTN_SCALE = (QK_NOPE + QK_ROPE) ** -0.5
Q_BLOCK = 128
POOL_WINDOWS = (2, 4, 8, 16)
POOL_MAX = 16
POOL_GROUP = D_MODEL // 4
D_FF = ((8 * D_MODEL // 3 + 127) // 128) * 128
N_EXPERTS = 8
TOP_K = 2
D_FF_EXPERT = 7 * D_MODEL // 2
NORM_EPS = 1e-6
NEG_INF = -1e30

kernel_name = 'hybrid_s5_mla_pool_moe_step'


def _rmsnorm(x, g):
    xf = x.astype(jnp.float32)
    y = xf * lax.rsqrt(jnp.mean(xf * xf, axis=-1, keepdims=True) + NORM_EPS)
    return (y * g.astype(jnp.float32)).astype(x.dtype)


def _rope(x, pos):
    half = x.shape[-1] // 2
    inv_freq = ROPE_THETA ** (-jnp.arange(half, dtype=jnp.float32) / half)
    ang = pos.astype(jnp.float32)[:, None] * inv_freq[None, :]
    ang = ang.reshape((1, pos.shape[0]) + (1,) * (x.ndim - 3) + (half,))
    cos, sin = jnp.cos(ang), jnp.sin(ang)
    xf = x.astype(jnp.float32)
    x1, x2 = xf[..., :half], xf[..., half:]
    return jnp.concatenate([x1 * cos - x2 * sin, x1 * sin + x2 * cos], axis=-1).astype(x.dtype)


def _block_len(n, b):
    return n if n <= b else math.gcd(n, b)


def _s5_discretise(lam_re, lam_im, log_dt, b_re, b_im):
    f32 = jnp.float32
    lr, li = lam_re.astype(f32), lam_im.astype(f32)
    dt = jnp.exp(log_dt.astype(f32))[:, None]
    mag = jnp.exp(lr * dt)
    ab_re, ab_im = mag * jnp.cos(li * dt), mag * jnp.sin(li * dt)
    nr, ni = ab_re - 1.0, ab_im
    den = lr * lr + li * li
    f_re = ((nr * lr + ni * li) / den)[..., None]
    f_im = ((ni * lr - nr * li) / den)[..., None]
    br, bi = b_re.astype(f32), b_im.astype(f32)
    return ab_re, ab_im, f_re * br - f_im * bi, f_re * bi + f_im * br


def _complex_affine_combine(e1, e2):
    a1r, a1i, b1r, b1i = e1
    a2r, a2i, b2r, b2i = e2
    return (a1r * a2r - a1i * a2i, a1r * a2i + a1i * a2r,
            a2r * b1r - a2i * b1i + b2r, a2r * b1i + a2i * b1r + b2i)


def _s5_mixer(h, h0_re, h0_im, lam_re, lam_im, log_dt, b_re, b_im, c_re, c_im, d, w_glu):
    f32 = jnp.float32
    n_b, seq, _ = h.shape
    ab_re, ab_im, bb_re, bb_im = _s5_discretise(lam_re, lam_im, log_dt, b_re, b_im)
    cr, ci = c_re.astype(f32), c_im.astype(f32)
    u = h.astype(f32)
    lc = _block_len(seq, S5_CHUNK)
    nc = seq // lc
    u_chunks = u.reshape(n_b, nc, lc, N_GROUPS, S5_GROUP).swapaxes(0, 1)

    def chunk_step(carry, uc):
        hr, hi = carry
        bu_re = jnp.einsum('blgc,gpc->blgp', uc, bb_re)
        bu_im = jnp.einsum('blgc,gpc->blgp', uc, bb_im)
        bu_re = bu_re.at[:, 0].add(ab_re * hr - ab_im * hi)
        bu_im = bu_im.at[:, 0].add(ab_re * hi + ab_im * hr)
        a_re = jnp.broadcast_to(ab_re, bu_re.shape)
        a_im = jnp.broadcast_to(ab_im, bu_im.shape)
        _, _, s_re, s_im = lax.associative_scan(
            _complex_affine_combine, (a_re, a_im, bu_re, bu_im), axis=1)
        y = jnp.einsum('gcp,blgp->blgc', cr, s_re) - jnp.einsum('gcp,blgp->blgc', ci, s_im)
        return (s_re[:, -1], s_im[:, -1]), y.reshape(n_b, lc, D_MODEL)

    (hr, hi), ys = lax.scan(chunk_step, (h0_re.astype(f32), h0_im.astype(f32)), u_chunks)
    y = ys.swapaxes(0, 1).reshape(n_b, seq, D_MODEL) + d.astype(f32) * u
    g = jax.nn.gelu(y).astype(h.dtype)
    ga, gb = jnp.split(g @ w_glu, 2, axis=-1)
    return ga * jax.nn.sigmoid(gb), hr.astype(h0_re.dtype), hi.astype(h0_im.dtype)


def _mla_project(h, pos, w_in, norm_q, norm_kv, w_q, qn_nope, qn_rope, kn_rope):
    n_b, seq, _ = h.shape
    a = h @ w_in
    q_lat = _rmsnorm(a[..., :Q_RANK], norm_q)
    c = _rmsnorm(a[..., Q_RANK:Q_RANK + KV_RANK], norm_kv)
    k_r = a[..., Q_RANK + KV_RANK:]
    q = (q_lat @ w_q).reshape(n_b, seq, N_HEADS, QK_NOPE + QK_ROPE)
    q_nope = _rmsnorm(q[..., :QK_NOPE], qn_nope)
    q_rope = _rope(_rmsnorm(q[..., QK_NOPE:], qn_rope), pos)
    k_rope = _rope(_rmsnorm(k_r, kn_rope), pos)
    return q_nope, q_rope, c, k_rope


def _mla_key_nope(c, w_uk, kn_nope):
    return _rmsnorm(jnp.einsum('bkr,rhd->bkhd', c, w_uk), kn_nope)


def _mla_scores(q_nope, q_rope, k_nope, k_rope):
    s = (jnp.einsum('bqhd,bkhd->bhqk', q_nope, k_nope)
         + jnp.einsum('bqhr,bkr->bhqk', q_rope, k_rope))
    return s.astype(jnp.float32) * ATTN_SCALE


def _mla_output(o_lat, w_uv, w_o):
    n_b, seq = o_lat.shape[0], o_lat.shape[1]
    v = jnp.einsum('blhr,rhv->blhv', o_lat, w_uv)
    return v.reshape(n_b, seq, N_HEADS * V_HEAD) @ w_o


def _mla_prompt(h, pos, w_in, norm_q, norm_kv, w_q, w_uk, w_uv, qn_nope, qn_rope, kn_nope, kn_rope, w_o):
    n_b, seq, _ = h.shape
    q_nope, q_rope, c, k_rope = _mla_project(h, pos, w_in, norm_q, norm_kv, w_q, qn_nope, qn_rope, kn_rope)
    k_nope = _mla_key_nope(c, w_uk, kn_nope)
    qb = _block_len(seq, Q_BLOCK)
    nq = seq // qb
    qn_b = q_nope.reshape(n_b, nq, qb, N_HEADS, QK_NOPE).swapaxes(0, 1)
    qr_b = q_rope.reshape(n_b, nq, qb, N_HEADS, QK_ROPE).swapaxes(0, 1)
    key_pos = jnp.arange(seq)

    def attend(args):
        qn, qr, q0 = args
        s = _mla_scores(qn, qr, k_nope, k_rope)
        causal = key_pos[None, :] <= (q0 + jnp.arange(qb))[:, None]
        p = jax.nn.softmax(jnp.where(causal, s, NEG_INF), axis=-1)
        return jnp.einsum('bhqk,bkr->bqhr', p.astype(c.dtype), c)

    o_lat = lax.map(attend, (qn_b, qr_b, jnp.arange(nq) * qb))
    o_lat = o_lat.swapaxes(0, 1).reshape(n_b, seq, N_HEADS, KV_RANK)
    return _mla_output(o_lat, w_uv, w_o), c, k_rope


def _online_softmax_update(carry, s, c_blk):
    m, l, acc = carry
    m_new = jnp.maximum(m, jnp.max(s, axis=-1))
    corr = jnp.exp(m - m_new)
    p = jnp.exp(s - m_new[..., None])
    l = l * corr + jnp.sum(p, axis=-1)
    acc = acc * corr[..., None] + jnp.einsum('bhqk,bkr->bhqr', p, c_blk.astype(jnp.float32))
    return (m_new, l, acc)


def _mla_sample(h, pos, cache_latent, cache_krope, page_table,
                w_in, norm_q, norm_kv, w_q, w_uk, w_uv, qn_nope, qn_rope, kn_nope, kn_rope, w_o):
    n_b, dq, _ = h.shape
    q_nope, q_rope, c_new, kr_new = _mla_project(h, pos, w_in, norm_q, norm_kv, w_q, qn_nope, qn_rope, kn_rope)

    def page_step(carry, page_ids):
        c_blk = cache_latent[page_ids]
        kr_blk = cache_krope[page_ids]
        s = _mla_scores(q_nope, q_rope, _mla_key_nope(c_blk, w_uk, kn_nope), kr_blk)
        return _online_softmax_update(carry, s, c_blk), None

    init = (jnp.full((n_b, N_HEADS, dq), NEG_INF, jnp.float32),
            jnp.zeros((n_b, N_HEADS, dq), jnp.float32),
            jnp.zeros((n_b, N_HEADS, dq, KV_RANK), jnp.float32))
    carry, _ = lax.scan(page_step, init, page_table.T)
    s_new = _mla_scores(q_nope, q_rope, _mla_key_nope(c_new, w_uk, kn_nope), kr_new)
    causal = jnp.arange(dq)[None, :] <= jnp.arange(dq)[:, None]
    s_new = jnp.where(causal, s_new, NEG_INF)
    _, l, acc = _online_softmax_update(carry, s_new, c_new)
    o_lat = (acc / l[..., None]).transpose(0, 2, 1, 3).astype(h.dtype)
    return _mla_output(o_lat, w_uv, w_o), c_new, kr_new


def _pool_mixer(h, prev, start, pool_w, pool_scale):
    n_b, seq, _ = h.shape
    f32 = jnp.float32
    ext = jnp.concatenate([prev.astype(h.dtype), h], axis=1)
    cs = jnp.cumsum(ext.astype(f32), axis=1)
    cs = jnp.concatenate([jnp.zeros((n_b, 1, D_MODEL), f32), cs], axis=1)
    pos = start + jnp.arange(seq)
    hf = h.astype(f32)
    groups = []
    for g, w in enumerate(POOL_WINDOWS):
        sl = slice(g * POOL_GROUP, (g + 1) * POOL_GROUP)
        total = cs[:, POOL_MAX:POOL_MAX + seq, sl] - cs[:, POOL_MAX - w:POOL_MAX - w + seq, sl]
        count = jnp.minimum(w, pos + 1).astype(f32)[None, :, None]
        groups.append(total / count - hf[:, :, sl])
    p = jnp.stack(groups, axis=2).astype(h.dtype)
    z = jnp.einsum('blgc,gcd->blgd', p, pool_w).reshape(n_b, seq, D_MODEL)
    return z * pool_scale, ext[:, -(POOL_MAX - 1):]


def _swiglu(h, w_in, w_out):
    g, u = jnp.split(h @ w_in, 2, axis=-1)
    return (jax.nn.silu(g) * u) @ w_out


def _moe_ffn(h, w_router, w_in, w_out):
    n_b, seq, _ = h.shape
    t = h.reshape(n_b * seq, D_MODEL)
    logits = (t @ w_router).astype(jnp.float32)
    top_v, top_i = lax.top_k(logits, TOP_K)
    gates = jax.nn.softmax(top_v, axis=-1)
    comb = jnp.sum(jax.nn.one_hot(top_i, N_EXPERTS, dtype=jnp.float32) * gates[..., None], axis=1)
    out = jnp.zeros((n_b * seq, D_MODEL), jnp.float32)
    for e in range(N_EXPERTS):
        out = out + comb[:, e:e + 1] * _swiglu(t, w_in[e], w_out[e]).astype(jnp.float32)
    return out.astype(h.dtype).reshape(n_b, seq, D_MODEL)


def setup_inputs(seed: int = 0) -> dict:
    key = jax.random.key(seed)
    keys = iter(jax.random.split(key, 256))

    def normal(shape, scale):
        return scale * jax.random.normal(next(keys), shape, jnp.float32)

    def gain(n):
        return 1.0 + normal((n,), 0.02)

    n_pages = PAST_LEN // PAGE_SIZE
    n_pool = (DEC_BATCH * n_pages * 5) // 4
    inp = {}
    inp['x_prompt'] = normal((BATCH, SEQ, D_MODEL), 1.0)
    inp['x_sample'] = normal((DEC_BATCH, DEC_SEQ, D_MODEL), 1.0)
    inp['state_l0_s5_re'] = normal((DEC_BATCH, N_GROUPS, S5_STATE), 0.1)
    inp['state_l0_s5_im'] = normal((DEC_BATCH, N_GROUPS, S5_STATE), 0.1)
    inp['cache_l1_latent'] = normal((n_pool, PAGE_SIZE, KV_RANK), 1.0)
    inp['cache_l1_krope'] = normal((n_pool, PAGE_SIZE, QK_ROPE), 1.0)
    inp['state_l2_pool'] = normal((DEC_BATCH, POOL_MAX - 1, D_MODEL), 1.0)
    inp['state_l3_s5_re'] = normal((DEC_BATCH, N_GROUPS, S5_STATE), 0.1)
    inp['state_l3_s5_im'] = normal((DEC_BATCH, N_GROUPS, S5_STATE), 0.1)
    perm = jax.random.permutation(next(keys), n_pool)
    inp['page_table'] = perm[:DEC_BATCH * n_pages].reshape(DEC_BATCH, n_pages).astype(jnp.int32)

    for i in range(DEPTH):
        pre = 'l%d_' % i
        inp[pre + 'norm_mix'] = gain(D_MODEL)
        kind = i % N_MIXERS
        if kind == 0:
            inp[pre + 's5_lambda_re'] = -0.5 + normal((N_GROUPS, S5_STATE), 0.01)
            inp[pre + 's5_lambda_im'] = (math.pi * jnp.arange(S5_STATE, dtype=jnp.float32))[None, :] + normal((N_GROUPS, S5_STATE), 0.01)
            inp[pre + 's5_log_dt'] = jax.random.uniform(next(keys), (N_GROUPS,), jnp.float32, math.log(1e-3), math.log(1e-1))
            inp[pre + 's5_b_re'] = normal((N_GROUPS, S5_STATE, S5_GROUP), (2 * S5_GROUP) ** -0.5)
            inp[pre + 's5_b_im'] = normal((N_GROUPS, S5_STATE, S5_GROUP), (2 * S5_GROUP) ** -0.5)
            inp[pre + 's5_c_re'] = normal((N_GROUPS, S5_GROUP, S5_STATE), (2 * S5_STATE) ** -0.5)
            inp[pre + 's5_c_im'] = normal((N_GROUPS, S5_GROUP, S5_STATE), (2 * S5_STATE) ** -0.5)
            inp[pre + 's5_d'] = normal((D_MODEL,), 1.0)
            inp[pre + 's5_w_glu'] = normal((D_MODEL, 2 * D_MODEL), D_MODEL ** -0.5)
        elif kind == 1:
            inp[pre + 'mla_w_in'] = normal((D_MODEL, Q_RANK + KV_RANK + QK_ROPE), D_MODEL ** -0.5)
            inp[pre + 'mla_norm_q'] = gain(Q_RANK)
            inp[pre + 'mla_norm_kv'] = gain(KV_RANK)
            inp[pre + 'mla_w_q'] = normal((Q_RANK, N_HEADS * (QK_NOPE + QK_ROPE)), Q_RANK ** -0.5)
            inp[pre + 'mla_w_uk'] = normal((KV_RANK, N_HEADS, QK_NOPE), KV_RANK ** -0.5)
            inp[pre + 'mla_w_uv'] = normal((KV_RANK, N_HEADS, V_HEAD), KV_RANK ** -0.5)
            inp[pre + 'mla_qn_nope'] = gain(QK_NOPE)
            inp[pre + 'mla_qn_rope'] = gain(QK_ROPE)
            inp[pre + 'mla_kn_nope'] = gain(QK_NOPE)
            inp[pre + 'mla_kn_rope'] = gain(QK_ROPE)
            inp[pre + 'mla_w_o'] = normal((N_HEADS * V_HEAD, D_MODEL), (N_HEADS * V_HEAD) ** -0.5)
        else:
            inp[pre + 'pool_w'] = normal((len(POOL_WINDOWS), POOL_GROUP, POOL_GROUP), POOL_GROUP ** -0.5)
            inp[pre + 'pool_scale'] = gain(D_MODEL)
        inp[pre + 'norm_ffn'] = gain(D_MODEL)
        if i % 2 == 0:
            inp[pre + 'ffn_w_in'] = normal((D_MODEL, 2 * D_FF), D_MODEL ** -0.5)
            inp[pre + 'ffn_w_out'] = normal((D_FF, D_MODEL), D_FF ** -0.5)
        else:
            inp[pre + 'moe_w_router'] = normal((D_MODEL, N_EXPERTS), D_MODEL ** -0.5)
            inp[pre + 'moe_w_in'] = normal((N_EXPERTS, D_MODEL, 2 * D_FF_EXPERT), D_MODEL ** -0.5)
            inp[pre + 'moe_w_out'] = normal((N_EXPERTS, D_FF_EXPERT, D_MODEL), D_FF_EXPERT ** -0.5)
    return inp


def reference(x_prompt, x_sample,
              state_l0_s5_re, state_l0_s5_im, cache_l1_latent, cache_l1_krope, state_l2_pool,
              state_l3_s5_re, state_l3_s5_im, page_table,
              l0_norm_mix, l0_s5_lambda_re, l0_s5_lambda_im, l0_s5_log_dt, l0_s5_b_re, l0_s5_b_im,
              l0_s5_c_re, l0_s5_c_im, l0_s5_d, l0_s5_w_glu, l0_norm_ffn, l0_ffn_w_in, l0_ffn_w_out,
              l1_norm_mix, l1_mla_w_in, l1_mla_norm_q, l1_mla_norm_kv, l1_mla_w_q, l1_mla_w_uk, l1_mla_w_uv,
              l1_mla_qn_nope, l1_mla_qn_rope, l1_mla_kn_nope, l1_mla_kn_rope, l1_mla_w_o,
              l1_norm_ffn, l1_moe_w_router, l1_moe_w_in, l1_moe_w_out,
              l2_norm_mix, l2_pool_w, l2_pool_scale, l2_norm_ffn, l2_ffn_w_in, l2_ffn_w_out,
              l3_norm_mix, l3_s5_lambda_re, l3_s5_lambda_im, l3_s5_log_dt, l3_s5_b_re, l3_s5_b_im,
              l3_s5_c_re, l3_s5_c_im, l3_s5_d, l3_s5_w_glu, l3_norm_ffn, l3_moe_w_router, l3_moe_w_in, l3_moe_w_out):
    n_batch, seq = x_prompt.shape[0], x_prompt.shape[1]
    dec_seq = x_sample.shape[1]
    past_len = page_table.shape[1] * cache_l1_latent.shape[1]
    pos_p = jnp.arange(seq)
    pos_s = past_len + jnp.arange(dec_seq)

    norm_mix = (l0_norm_mix, l1_norm_mix, l2_norm_mix, l3_norm_mix)
    norm_ffn = (l0_norm_ffn, l1_norm_ffn, l2_norm_ffn, l3_norm_ffn)
    mixer_params = (
        (l0_s5_lambda_re, l0_s5_lambda_im, l0_s5_log_dt, l0_s5_b_re, l0_s5_b_im, l0_s5_c_re, l0_s5_c_im, l0_s5_d, l0_s5_w_glu),
        (l1_mla_w_in, l1_mla_norm_q, l1_mla_norm_kv, l1_mla_w_q, l1_mla_w_uk, l1_mla_w_uv,
         l1_mla_qn_nope, l1_mla_qn_rope, l1_mla_kn_nope, l1_mla_kn_rope, l1_mla_w_o),
        (l2_pool_w, l2_pool_scale),
        (l3_s5_lambda_re, l3_s5_lambda_im, l3_s5_log_dt, l3_s5_b_re, l3_s5_b_im, l3_s5_c_re, l3_s5_c_im, l3_s5_d, l3_s5_w_glu),
    )
    mixer_state = (
        (state_l0_s5_re, state_l0_s5_im),
        (cache_l1_latent, cache_l1_krope),
        (state_l2_pool,),
        (state_l3_s5_re, state_l3_s5_im),
    )
    ffn_params = (
        (l0_ffn_w_in, l0_ffn_w_out),
        (l1_moe_w_router, l1_moe_w_in, l1_moe_w_out),
        (l2_ffn_w_in, l2_ffn_w_out),
        (l3_moe_w_router, l3_moe_w_in, l3_moe_w_out),
    )

    xp, xs = x_prompt, x_sample
    new_state = []
    for i in range(DEPTH):
        hp = _rmsnorm(xp, norm_mix[i])
        hs = _rmsnorm(xs, norm_mix[i])
        kind = i % N_MIXERS
        if kind == 0:
            zero = jnp.zeros((n_batch, N_GROUPS, S5_STATE), mixer_state[i][0].dtype)
            mp, p_re, p_im = _s5_mixer(hp, zero, zero, *mixer_params[i])
            ms, s_re, s_im = _s5_mixer(hs, mixer_state[i][0], mixer_state[i][1], *mixer_params[i])
            new_state.append((p_re, s_re, p_im, s_im))
        elif kind == 1:
            mp, p_lat, p_kr = _mla_prompt(hp, pos_p, *mixer_params[i])
            ms, s_lat, s_kr = _mla_sample(hs, pos_s, mixer_state[i][0], mixer_state[i][1], page_table, *mixer_params[i])
            new_state.append((p_lat, s_lat, p_kr, s_kr))
        else:
            zero = jnp.zeros((n_batch, POOL_MAX - 1, D_MODEL), mixer_state[i][0].dtype)
            mp, p_pool = _pool_mixer(hp, zero, 0, *mixer_params[i])
            ms, s_pool = _pool_mixer(hs, mixer_state[i][0], past_len, *mixer_params[i])
            new_state.append((p_pool, s_pool))
        xp = xp + mp
        xs = xs + ms
        hp = _rmsnorm(xp, norm_ffn[i])
        hs = _rmsnorm(xs, norm_ffn[i])
        if i % 2 == 0:
            xp = xp + _swiglu(hp, *ffn_params[i])
            xs = xs + _swiglu(hs, *ffn_params[i])
        else:
            xp = xp + _moe_ffn(hp, *ffn_params[i])
            xs = xs + _moe_ffn(hs, *ffn_params[i])

    l0_re_p, l0_re_s, l0_im_p, l0_im_s = new_state[0]
    l1_lat_p, l1_lat_s, l1_kr_p, l1_kr_s = new_state[1]
    l2_pool_p, l2_pool_s = new_state[2]
    l3_re_p, l3_re_s, l3_im_p, l3_im_s = new_state[3]
    return (xp, xs,
            l0_re_p, l0_re_s, l0_im_p, l0_im_s,
            l1_lat_p, l1_lat_s, l1_kr_p, l1_kr_s,
            l2_pool_p, l2_pool_s,
            l3_re_p, l3_re_s, l3_im_p, l3_im_s)
```

```python
import functools
import math

import jax
import jax.numpy as jnp
from jax import lax
from jax.experimental import pallas as pl
from jax.experimental.pallas import tpu as pltpu

F32 = jnp.float32
BF16 = jnp.bfloat16
I32 = jnp.int32

D = 1024
NORM_EPS = 1e-6
NEG_INF = -1e30
S5_GROUP = 16
S5_STATE = 64
N_GROUPS = D // S5_GROUP
S5_T = 8
S5_BUNDLE = 8
N_BUNDLES = N_GROUPS // S5_BUNDLE
N_HEADS = 8
QK_NOPE = 128
QK_ROPE = 64
Q_RANK = 512
KV_RANK = 256
ROPE_THETA = 10000.0
ATTN_SCALE = (QK_NOPE + QK_ROPE) ** -0.5
PAGE = 128
PAGES_PER_STEP = 8
POOL_WINDOWS = (2, 4, 8, 16)
POOL_HALO = 16
POOL_GROUP = D // 4
N_EXPERTS = 8
MOE_TB = 1024
MOE_CH = 256
MOE_TM = 1024
MOE_ALIGN = 16
MOE_PIECE = 16


def _cparams(sem, vmem_mb=48):
    return pltpu.CompilerParams(dimension_semantics=sem, vmem_limit_bytes=vmem_mb * 1024 * 1024)


def _rms(x, g):
    r = lax.rsqrt(jnp.mean(x * x, axis=-1, keepdims=True) + NORM_EPS)
    return x * r * g


def _dot(a, b):
    return jnp.dot(a, b, preferred_element_type=F32)


def _dot_nt(a, b):
    return lax.dot_general(a, b, (((1,), (1,)), ((), ())), preferred_element_type=F32)


def _rmsnorm_kernel(x_ref, g_ref, o_ref):
    o_ref[...] = _rms(x_ref[...], g_ref[...]).astype(o_ref.dtype)


def _rmsnorm(x, g, tm=1024):
    t = x.shape[0]
    return pl.pallas_call(
        _rmsnorm_kernel, grid=(t // tm,),
        in_specs=[pl.BlockSpec((tm, D), lambda i: (i, 0)), pl.BlockSpec((1, D), lambda i: (0, 0))],
        out_specs=pl.BlockSpec((tm, D), lambda i: (i, 0)),
        out_shape=jax.ShapeDtypeStruct((t, D), F32),
        compiler_params=_cparams(("parallel",)), name="rmsnorm",
    )(x, g.reshape(1, D))


def _s5_weights(lam_re, lam_im, log_dt, b_re, b_im, c_re, c_im, d):
    lr, li = lam_re.astype(F32), lam_im.astype(F32)
    dt = jnp.exp(log_dt.astype(F32))[:, None]
    mag = jnp.exp(lr * dt)
    ar, ai = mag * jnp.cos(li * dt), mag * jnp.sin(li * dt)
    nr, ni = ar - 1.0, ai
    den = lr * lr + li * li
    f_re = ((nr * lr + ni * li) / den)[..., None]
    f_im = ((ni * lr - nr * li) / den)[..., None]
    br, bi = b_re.astype(F32), b_im.astype(F32)
    bb_re, bb_im = f_re * br - f_im * bi, f_re * bi + f_im * br
    cr, ci = c_re.astype(F32), c_im.astype(F32)

    pr, pi = [jnp.ones_like(ar)], [jnp.zeros_like(ar)]
    for _ in range(S5_T):
        pr.append(pr[-1] * ar - pi[-1] * ai)
        pi.append(pr[-2] * ai + pi[-1] * ar)
    pr, pi = jnp.stack(pr), jnp.stack(pi)

    ab_re = pr[:S5_T, :, :, None] * bb_re - pi[:S5_T, :, :, None] * bb_im
    ab_im = pr[:S5_T, :, :, None] * bb_im + pi[:S5_T, :, :, None] * bb_re
    hi = lax.Precision.HIGHEST
    kern = (jnp.einsum("gcp,jgpd->jgcd", cr, ab_re, precision=hi)
            - jnp.einsum("gcp,jgpd->jgcd", ci, ab_im, precision=hi))

    nb, bg, t = N_BUNDLES, S5_BUNDLE, S5_T
    eye = jnp.eye(bg, dtype=F32)
    tt = jnp.arange(t)
    lag = tt[None, :] - tt[:, None]
    ksel = jnp.where((lag >= 0)[:, :, None, None, None], kern[jnp.clip(lag, 0, t - 1)], 0.0)
    ksel = ksel.reshape(t, t, nb, bg, S5_GROUP, S5_GROUP)
    m = jnp.einsum("stbgcd,gh->bsgdthc", ksel, eye, precision=hi).reshape(nb, t * 128, t * 128)

    inj = jnp.stack([ab_re[::-1], ab_im[::-1]])
    inj = inj.reshape(2, t, nb, bg, S5_STATE, S5_GROUP)
    we = jnp.einsum("rsbgpd,gh->bsgdrhp", inj, eye, precision=hi).reshape(nb, t * 128, 2 * bg * S5_STATE)

    p1r, p1i = pr[1:, :, None, :], pi[1:, :, None, :]
    co_re = cr[None] * p1r - ci[None] * p1i
    co_im = -(cr[None] * p1i + ci[None] * p1r)
    co = jnp.stack([co_re, co_im]).reshape(2, t, nb, bg, S5_GROUP, S5_STATE)
    ws = jnp.einsum("rtbgcp,gh->brgpthc", co, eye, precision=hi).reshape(nb, 2 * bg * S5_STATE, t * 128)

    a_t = jnp.stack([pr[t], pi[t]]).reshape(2, nb, bg * S5_STATE)
    a_t = a_t.transpose(1, 0, 2).reshape(nb, 1, 2 * bg * S5_STATE)
    dv = jnp.broadcast_to(d.astype(F32).reshape(1, nb, 128), (t, nb, 128)).transpose(1, 0, 2).reshape(nb, 1, t * 128)
    return m.astype(BF16), we.astype(BF16), ws.astype(BF16), a_t, dv


def _s5_kernel(u_ref, s0_ref, m_ref, we_ref, ws_ref, a_ref, d_ref, g_ref, sfin_ref, state, sprev, ebuf,
               *, nb, cb, last):
    i = pl.program_id(1)
    half = S5_BUNDLE * S5_STATE

    @pl.when(i == 0)
    def _():
        state[...] = s0_ref[...]

    u = u_ref[...]
    ub = u.astype(BF16)
    ebuf[...] = _dot(ub, we_ref[...])
    ar, ai = a_ref[:, :half], a_ref[:, half:]
    s = state[...]
    for c in range(cb):
        rows = slice(c * nb, (c + 1) * nb)
        sprev[rows, :] = s
        sr, si = s[:, :half], s[:, half:]
        s = jnp.concatenate([ar * sr - ai * si + ebuf[rows, :half],
                             ar * si + ai * sr + ebuf[rows, half:]], axis=1)
    state[...] = s
    y = _dot(ub, m_ref[...]) + _dot(sprev[...].astype(BF16), ws_ref[...])
    y = y + d_ref[...] * u
    g_ref[...] = jax.nn.gelu(y).astype(BF16)

    @pl.when(i == last)
    def _():
        sfin_ref[...] = s


def _s5_scan(u, s0, tables, nb, cb):
    m, we, ws, a_t, dv = tables
    rows = u.shape[1]
    r = nb * cb
    ni = rows // r
    w = S5_T * 128
    kern = functools.partial(_s5_kernel, nb=nb, cb=cb, last=ni - 1)
    big = lambda: pl.BlockSpec((None, w, w), lambda g, i: (g, 0, 0))
    vec = lambda: pl.BlockSpec((None, 1, w), lambda g, i: (g, 0, 0))
    return pl.pallas_call(
        kern, grid=(N_BUNDLES, ni),
        in_specs=[pl.BlockSpec((None, r, w), lambda g, i: (g, i, 0)),
                  pl.BlockSpec((None, nb, w), lambda g, i: (g, 0, 0)),
                  big(), big(), big(), vec(), vec()],
        out_specs=[pl.BlockSpec((None, r, w), lambda g, i: (g, i, 0)),
                   pl.BlockSpec((None, nb, w), lambda g, i: (g, 0, 0))],
        out_shape=[jax.ShapeDtypeStruct((N_BUNDLES, rows, w), BF16),
                   jax.ShapeDtypeStruct((N_BUNDLES, nb, w), F32)],
        scratch_shapes=[pltpu.VMEM((nb, w), F32), pltpu.VMEM((r, w), F32), pltpu.VMEM((r, w), F32)],
        compiler_params=_cparams(("parallel", "arbitrary")), name="s5_scan",
    )(u, s0, m, we, ws, a_t, dv)


def _to_bundles(h, n_b, seq):
    nc = seq // S5_T
    h = h.reshape(n_b, nc, S5_T, N_BUNDLES, 128).transpose(3, 1, 0, 2, 4)
    return h.reshape(N_BUNDLES, nc * n_b, S5_T * 128)


def _from_bundles(g, n_b, seq):
    nc = seq // S5_T
    g = g.reshape(N_BUNDLES, nc, n_b, S5_T, 128).transpose(2, 1, 3, 0, 4)
    return g.reshape(n_b * seq, D)


def _state_to_bundles(s_re, s_im):
    n_b = s_re.shape[0]
    s = jnp.stack([s_re, s_im], axis=1).reshape(n_b, 2, N_BUNDLES, S5_BUNDLE * S5_STATE)
    return s.transpose(2, 0, 1, 3).reshape(N_BUNDLES, n_b, 2 * S5_BUNDLE * S5_STATE)


def _state_from_bundles(s):
    n_b = s.shape[1]
    s = s.reshape(N_BUNDLES, n_b, 2, S5_BUNDLE, S5_STATE).transpose(2, 1, 0, 3, 4)
    s = s.reshape(2, n_b, N_GROUPS, S5_STATE)
    return s[0], s[1]


def _glu_kernel(g_ref, x_ref, w_ref, o_ref):
    z = _dot(g_ref[...], w_ref[...])
    o_ref[...] = x_ref[...] + z[:, :D] * jax.nn.sigmoid(z[:, D:])


def _glu_residual(g, x, w_glu, tm=512):
    t = x.shape[0]
    return pl.pallas_call(
        _glu_kernel, grid=(t // tm,),
        in_specs=[pl.BlockSpec((tm, D), lambda i: (i, 0)), pl.BlockSpec((tm, D), lambda i: (i, 0)),
                  pl.BlockSpec((D, 2 * D), lambda i: (0, 0))],
        out_specs=pl.BlockSpec((tm, D), lambda i: (i, 0)),
        out_shape=jax.ShapeDtypeStruct((t, D), F32),
        compiler_params=_cparams(("parallel",)), name="glu_residual",
    )(g, x, w_glu)


def _s5_layer(x, norm_g, state_re, state_im, params, dims):
    n_bp, seq_p, n_bs, seq_s = dims
    tp = n_bp * seq_p
    lam_re, lam_im, log_dt, b_re, b_im, c_re, c_im, d, w_glu = params
    tables = _s5_weights(lam_re, lam_im, log_dt, b_re, b_im, c_re, c_im, d)
    h = _rmsnorm(x, norm_g)
    zeros = jnp.zeros((N_BUNDLES, n_bp, 2 * S5_BUNDLE * S5_STATE), F32)
    cb = math.gcd(seq_p // S5_T, max(1, 512 // n_bp))
    g_p, s_p = _s5_scan(_to_bundles(h[:tp], n_bp, seq_p), zeros, tables, n_bp, cb)
    g_s, s_s = _s5_scan(_to_bundles(h[tp:], n_bs, seq_s), _state_to_bundles(state_re, state_im), tables,
                        n_bs, seq_s // S5_T)
    g = jnp.concatenate([_from_bundles(g_p, n_bp, seq_p), _from_bundles(g_s, n_bs, seq_s)], axis=0)
    x = _glu_residual(g, x, w_glu.astype(BF16))
    p_re, p_im = _state_from_bundles(s_p)
    s_re, s_im = _state_from_bundles(s_s)
    return x, (p_re, s_re, p_im, s_im)


def _ffn_kernel(x_ref, gn_ref, wg_ref, wu_ref, wo_ref, o_ref, h_scr, acc, *, nf):
    f = pl.program_id(1)

    @pl.when(f == 0)
    def _():
        h_scr[...] = _rms(x_ref[...], gn_ref[...]).astype(BF16)
        acc[...] = jnp.zeros_like(acc)

    h = h_scr[...]
    t = (jax.nn.silu(_dot(h, wg_ref[...])) * _dot(h, wu_ref[...])).astype(BF16)
    acc[...] += _dot(t, wo_ref[...])

    @pl.when(f == nf - 1)
    def _():
        o_ref[...] = x_ref[...] + acc[...]


def _ffn_dense(x, norm_g, w_in, w_out, tm=512, nf=2):
    t = x.shape[0]
    d_ff = w_out.shape[0]
    tf = d_ff // nf
    w_in = w_in.astype(BF16)
    return pl.pallas_call(
        functools.partial(_ffn_kernel, nf=nf), grid=(t // tm, nf),
        in_specs=[pl.BlockSpec((tm, D), lambda i, f: (i, 0)), pl.BlockSpec((1, D), lambda i, f: (0, 0)),
                  pl.BlockSpec((D, tf), lambda i, f: (0, f)), pl.BlockSpec((D, tf), lambda i, f: (0, f + nf)),
                  pl.BlockSpec((tf, D), lambda i, f: (f, 0))],
        out_specs=pl.BlockSpec((tm, D), lambda i, f: (i, 0)),
        out_shape=jax.ShapeDtypeStruct((t, D), F32),
        scratch_shapes=[pltpu.VMEM((tm, D), BF16), pltpu.VMEM((tm, D), F32)],
        compiler_params=_cparams(("parallel", "arbitrary")), name="ffn_dense",
    )(x, norm_g.reshape(1, D), w_in, w_in, w_out.astype(BF16))


def _swap_halves(w):
    half = w.shape[-1] // 2
    return jnp.concatenate([w[..., half:], w[..., :half]], axis=-1)


def _pad_lanes(w, n=128):
    return jnp.concatenate([w, jnp.zeros(w.shape[:-1] + (n - w.shape[-1],), w.dtype)], axis=-1)


def _rope_tables(seq_p, seq_s, past_len, rows_s):
    half = QK_ROPE // 2
    inv_freq = ROPE_THETA ** (-jnp.arange(half, dtype=F32) / half)
    pos = jnp.concatenate([jnp.arange(seq_p), past_len + (jnp.arange(rows_s) % seq_s)])
    ang = pos.astype(F32)[:, None] * inv_freq[None, :]
    cos, sin = jnp.cos(ang), jnp.sin(ang)
    return _pad_lanes(jnp.concatenate([cos, cos], axis=1)), _pad_lanes(jnp.concatenate([-sin, sin], axis=1))


def _rope_norm(x, x_sw, g, g_sw, cos, sin):
    r = lax.rsqrt(jnp.sum(x * x, axis=-1, keepdims=True) * (1.0 / QK_ROPE) + NORM_EPS)
    return r * (x * g * cos + x_sw * g_sw * sin)


def _mla_proj_kernel(x_ref, gmix, win, nq, nkv, cos, sin, gkr, gkrs, wq, gqn, gqr, gqrs, wuk, gkn,
                     c_out, kr_out, cbf_out, q_out, k_out, hq, cn, krp):
    h = pl.program_id(1)

    @pl.when(h == 0)
    def _():
        hn = _rms(x_ref[...], gmix[...]).astype(BF16)
        a = _dot(hn, win[...])
        hq[...] = _rms(a[:, :Q_RANK], nq[...]).astype(BF16)
        c = _rms(a[:, Q_RANK:Q_RANK + KV_RANK], nkv[...])
        c_out[...] = c
        cb = c.astype(BF16)
        cn[...] = cb
        cbf_out[...] = cb
        kr = _rope_norm(a[:, 768:896], a[:, 896:1024], gkr[...], gkrs[...], cos[...], sin[...])
        kr_out[...] = kr
        krp[...] = kr.astype(BF16)

    q3 = _dot(hq[...], wq[...])
    qn = _rms(q3[:, :128], gqn[...])
    qr = _rope_norm(q3[:, 128:256], q3[:, 256:384], gqr[...], gqrs[...], cos[...], sin[...])
    q_out[...] = jnp.concatenate([qn, qr], axis=1).astype(BF16)
    kn = _rms(_dot(cn[...], wuk[...]), gkn[...])
    k_out[...] = jnp.concatenate([kn.astype(BF16), krp[...]], axis=1)


def _mla_proj(x, norm_g, params, dims, past_len, tm=512):
    n_bp, seq_p, n_bs, seq_s = dims
    t = x.shape[0]
    tp = n_bp * seq_p
    w_in, norm_q, norm_kv, w_q, w_uk, w_uv, qn_nope, qn_rope, kn_nope, kn_rope, w_o = params
    w_kr = w_in[:, Q_RANK + KV_RANK:]
    win = jnp.concatenate([w_in[:, :Q_RANK + KV_RANK], _pad_lanes(w_kr), _pad_lanes(_swap_halves(w_kr))],
                          axis=1).astype(BF16)
    wq3 = w_q.reshape(Q_RANK, N_HEADS, QK_NOPE + QK_ROPE)
    wq_r = wq3[:, :, QK_NOPE:]
    wq = jnp.concatenate([wq3[:, :, :QK_NOPE], _pad_lanes(wq_r), _pad_lanes(_swap_halves(wq_r))], axis=2)
    wq = wq.transpose(1, 0, 2).astype(BF16)
    wuk = w_uk.transpose(1, 0, 2).astype(BF16)
    rows_s = min(tm, t - tp)
    cos, sin = _rope_tables(seq_p, seq_s, past_len, rows_s)
    n_pt = seq_p // tm
    tab = lambda i, h: (jnp.where(i < tp // tm, i % n_pt, n_pt), 0)
    row = lambda v: v.reshape(1, -1).astype(F32)
    vec = lambda n: pl.BlockSpec((1, n), lambda i, h: (0, 0))
    tok = lambda n: pl.BlockSpec((tm, n), lambda i, h: (i, 0))
    return pl.pallas_call(
        _mla_proj_kernel, grid=(t // tm, N_HEADS),
        in_specs=[tok(D), vec(D), pl.BlockSpec((D, D), lambda i, h: (0, 0)), vec(Q_RANK), vec(KV_RANK),
                  pl.BlockSpec((tm, 128), tab), pl.BlockSpec((tm, 128), tab), vec(128), vec(128),
                  pl.BlockSpec((None, Q_RANK, 384), lambda i, h: (h, 0, 0)), vec(128), vec(128), vec(128),
                  pl.BlockSpec((None, KV_RANK, 128), lambda i, h: (h, 0, 0)), vec(128)],
        out_specs=[tok(KV_RANK), tok(128), tok(KV_RANK),
                   pl.BlockSpec((None, tm, 256), lambda i, h: (h, i, 0)),
                   pl.BlockSpec((None, tm, 256), lambda i, h: (h, i, 0))],
        out_shape=[jax.ShapeDtypeStruct((t, KV_RANK), F32), jax.ShapeDtypeStruct((t, 128), F32),
                   jax.ShapeDtypeStruct((t, KV_RANK), BF16),
                   jax.ShapeDtypeStruct((N_HEADS, t, 256), BF16), jax.ShapeDtypeStruct((N_HEADS, t, 256), BF16)],
        scratch_shapes=[pltpu.VMEM((tm, Q_RANK), BF16), pltpu.VMEM((tm, KV_RANK), BF16), pltpu.VMEM((tm, 128), BF16)],
        compiler_params=_cparams(("parallel", "arbitrary")), name="mla_proj",
    )(x, row(norm_g), win, row(norm_q), row(norm_kv), cos, sin,
      row(_pad_lanes(kn_rope)), row(_pad_lanes(_swap_halves(kn_rope))), wq, row(qn_nope),
      row(_pad_lanes(qn_rope)), row(_pad_lanes(_swap_halves(qn_rope))), wuk, row(kn_nope))


def _flash_kernel(qi_tab, ki_tab, q_ref, k_ref, v_ref, o_ref, m, l, acc, *, tq):
    step = pl.program_id(2)
    qi, ki = qi_tab[step], ki_tab[step]

    @pl.when(ki == 0)
    def _():
        m[...] = jnp.full_like(m, NEG_INF)
        l[...] = jnp.zeros_like(l)
        acc[...] = jnp.zeros_like(acc)

    def update(masked):
        s = _dot_nt(q_ref[...], k_ref[...]) * ATTN_SCALE
        if masked:
            r = lax.broadcasted_iota(I32, s.shape, 0)
            c = lax.broadcasted_iota(I32, s.shape, 1)
            s = jnp.where(c <= r, s, NEG_INF)
        m_new = jnp.maximum(m[...], jnp.max(s, axis=-1, keepdims=True))
        corr = jnp.exp(m[...] - m_new)
        p = jnp.exp(s - m_new)
        l[...] = l[...] * corr + jnp.sum(p, axis=-1, keepdims=True)
        acc[...] = acc[...] * corr + _dot(p.astype(BF16), v_ref[...])
        m[...] = m_new

    @pl.when(ki < qi)
    def _():
        update(False)

    @pl.when(ki == qi)
    def _():
        update(True)
        o_ref[...] = (acc[...] / l[...]).astype(o_ref.dtype)


def _mla_prompt_attn(qcat, kcat, cbf, n_b, seq, tq=512):
    nq = seq // tq
    pairs = [(qi, ki) for qi in range(nq) for ki in range(qi + 1)]
    qi_tab = jnp.array([p[0] for p in pairs], I32)
    ki_tab = jnp.array([p[1] for p in pairs], I32)
    grid_spec = pltpu.PrefetchScalarGridSpec(
        num_scalar_prefetch=2, grid=(n_b, N_HEADS, len(pairs)),
        in_specs=[pl.BlockSpec((None, tq, 256), lambda b, h, s, qt, kt: (h, b * nq + qt[s], 0)),
                  pl.BlockSpec((None, tq, 256), lambda b, h, s, qt, kt: (h, b * nq + kt[s], 0)),
                  pl.BlockSpec((tq, KV_RANK), lambda b, h, s, qt, kt: (b * nq + kt[s], 0))],
        out_specs=pl.BlockSpec((None, tq, KV_RANK), lambda b, h, s, qt, kt: (h, b * nq + qt[s], 0)),
        scratch_shapes=[pltpu.VMEM((tq, 1), F32), pltpu.VMEM((tq, 1), F32), pltpu.VMEM((tq, KV_RANK), F32)])
    return pl.pallas_call(
        functools.partial(_flash_kernel, tq=tq), grid_spec=grid_spec,
        out_shape=jax.ShapeDtypeStruct((N_HEADS, n_b * seq, KV_RANK), BF16),
        compiler_params=_cparams(("parallel", "parallel", "arbitrary")), name="mla_prompt_attn",
    )(qi_tab, ki_tab, qcat, kcat, cbf)


def _sample_attn_kernel(pt_ref, *refs, n_chunks, dq):
    npg = PAGES_PER_STEP
    lat = refs[:npg]
    krp = refs[npg:2 * npg]
    cnew, krnew, q_ref, wuk_ref, wukt_ref, gkn_ref, o_ref, qabs, cbuf, kbuf, m, l, acc = refs[2 * npg:]
    kc = pl.program_id(1)
    nhq = N_HEADS * dq

    @pl.when(kc == 0)
    def _():
        m[...] = jnp.full_like(m, NEG_INF)
        l[...] = jnp.zeros_like(l)
        acc[...] = jnp.zeros_like(acc)
        for h in range(N_HEADS):
            qn = (q_ref[h * dq:(h + 1) * dq, :QK_NOPE] * gkn_ref[...]).astype(BF16)
            qabs[h * dq:(h + 1) * dq, :] = _dot_nt(qn, wuk_ref[h]).astype(BF16)

    def attend(c_bf, kr_bf, mask):
        nk = c_bf.shape[0]
        rinv = []
        for h in range(N_HEADS):
            kn_t = _dot_nt(wukt_ref[h * QK_NOPE:(h + 1) * QK_NOPE, :], c_bf)
            ssq = jnp.sum(kn_t * kn_t, axis=0, keepdims=True)
            rinv.append(jnp.broadcast_to(lax.rsqrt(ssq * (1.0 / QK_NOPE) + NORM_EPS), (dq, nk)))
        rinv = jnp.concatenate(rinv, axis=0)
        qr = q_ref[:, QK_NOPE:QK_NOPE + QK_ROPE].astype(BF16)
        s = (_dot_nt(qabs[...], c_bf) * rinv + _dot_nt(qr, kr_bf)) * ATTN_SCALE
        if mask is not None:
            s = jnp.where(mask, s, NEG_INF)
        m_new = jnp.maximum(m[...], jnp.max(s, axis=-1, keepdims=True))
        corr = jnp.exp(m[...] - m_new)
        p = jnp.exp(s - m_new)
        l[...] = l[...] * corr + jnp.sum(p, axis=-1, keepdims=True)
        acc[...] = acc[...] * corr + _dot(p.astype(BF16), c_bf)
        m[...] = m_new

    @pl.when(kc < n_chunks)
    def _():
        for j in range(npg):
            cbuf[j * PAGE:(j + 1) * PAGE, :] = lat[j][...].astype(BF16)
            kbuf[j * PAGE:(j + 1) * PAGE, :] = krp[j][...].astype(BF16)
        attend(cbuf[...], kbuf[...], None)

    @pl.when(kc == n_chunks)
    def _():
        key = lax.broadcasted_iota(I32, (nhq, PAGE), 1)
        qpos = lax.broadcasted_iota(I32, (nhq, PAGE), 0) % dq
        attend(cnew[...].astype(BF16), krnew[...].astype(BF16), key <= qpos)
        o_ref[...] = (acc[...] / l[...]).astype(o_ref.dtype)


def _mla_sample_attn(q_s, c_new, kr_new, cache_latent, cache_krope, page_table, wuk, gkn, n_b, dq):
    npg = PAGES_PER_STEP
    n_pages = page_table.shape[1]
    n_chunks = n_pages // npg
    nhq = N_HEADS * dq
    wukt = wuk.transpose(0, 2, 1).reshape(N_HEADS * QK_NOPE, KV_RANK)
    pad = lambda a: jnp.concatenate([a, jnp.zeros((n_b, PAGE - dq, a.shape[-1]), a.dtype)], axis=1)
    page = lambda j, w: pl.BlockSpec(
        (None, PAGE, w), lambda b, kc, pt: (pt[b, jnp.minimum(kc, n_chunks - 1) * npg + j], 0, 0))
    per_b = lambda r, w: pl.BlockSpec((None, r, w), lambda b, kc, pt: (b, 0, 0))
    grid_spec = pltpu.PrefetchScalarGridSpec(
        num_scalar_prefetch=1, grid=(n_b, n_chunks + 1),
        in_specs=([page(j, KV_RANK) for j in range(npg)] + [page(j, QK_ROPE) for j in range(npg)]
                  + [per_b(PAGE, KV_RANK), per_b(PAGE, QK_ROPE), per_b(nhq, 256),
                     pl.BlockSpec((N_HEADS, KV_RANK, QK_NOPE), lambda b, kc, pt: (0, 0, 0)),
                     pl.BlockSpec((N_HEADS * QK_NOPE, KV_RANK), lambda b, kc, pt: (0, 0)),
                     pl.BlockSpec((1, QK_NOPE), lambda b, kc, pt: (0, 0))]),
        out_specs=per_b(nhq, KV_RANK),
        scratch_shapes=[pltpu.VMEM((nhq, KV_RANK), BF16), pltpu.VMEM((npg * PAGE, KV_RANK), BF16),
                        pltpu.VMEM((npg * PAGE, QK_ROPE), BF16),
                        pltpu.VMEM((nhq, 1), F32), pltpu.VMEM((nhq, 1), F32), pltpu.VMEM((nhq, KV_RANK), F32)])
    return pl.pallas_call(
        functools.partial(_sample_attn_kernel, n_chunks=n_chunks, dq=dq), grid_spec=grid_spec,
        out_shape=jax.ShapeDtypeStruct((n_b, nhq, KV_RANK), BF16),
        compiler_params=_cparams(("parallel", "arbitrary")), name="mla_sample_attn",
    )(page_table, *([cache_latent] * npg), *([cache_krope] * npg), pad(c_new), pad(kr_new), q_s,
      wuk, wukt, gkn.reshape(1, QK_NOPE).astype(F32))


def _mla_out_kernel(o_ref, x_ref, wuv_ref, wo_ref, out_ref):
    v = jnp.concatenate([_dot(o_ref[h], wuv_ref[h]) for h in range(N_HEADS)], axis=1)
    out_ref[...] = x_ref[...] + _dot(v.astype(BF16), wo_ref[...])


def _mla_out(o_lat, x, w_uv, w_o, tm=512):
    t = x.shape[0]
    return pl.pallas_call(
        _mla_out_kernel, grid=(t // tm,),
        in_specs=[pl.BlockSpec((N_HEADS, tm, KV_RANK), lambda i: (0, i, 0)), pl.BlockSpec((tm, D), lambda i: (i, 0)),
                  pl.BlockSpec((N_HEADS, KV_RANK, 128), lambda i: (0, 0, 0)), pl.BlockSpec((D, D), lambda i: (0, 0))],
        out_specs=pl.BlockSpec((tm, D), lambda i: (i, 0)),
        out_shape=jax.ShapeDtypeStruct((t, D), F32),
        compiler_params=_cparams(("parallel",)), name="mla_out",
    )(o_lat, x, w_uv.transpose(1, 0, 2).astype(BF16), w_o.astype(BF16))


def _mla_layer(x, norm_g, cache_latent, cache_krope, page_table, params, dims):
    n_bp, seq_p, n_bs, seq_s = dims
    tp = n_bp * seq_p
    past_len = page_table.shape[1] * cache_latent.shape[1]
    w_uk, w_uv, kn_nope, w_o = params[4], params[5], params[8], params[10]
    c, kr, cbf, qcat, kcat = _mla_proj(x, norm_g, params, dims, past_len)
    kr = kr[:, :QK_ROPE]
    o_p = _mla_prompt_attn(qcat, kcat, cbf, n_bp, seq_p)
    q_s = qcat[:, tp:].astype(F32).reshape(N_HEADS, n_bs, seq_s, 256).transpose(1, 0, 2, 3)
    q_s = q_s.reshape(n_bs, N_HEADS * seq_s, 256)
    c_s = c[tp:].reshape(n_bs, seq_s, KV_RANK)
    kr_s = kr[tp:].reshape(n_bs, seq_s, QK_ROPE)
    o_s = _mla_sample_attn(q_s, c_s, kr_s, cache_latent, cache_krope, page_table,
                           w_uk.transpose(1, 0, 2).astype(BF16), kn_nope, n_bs, seq_s)
    o_s = o_s.reshape(n_bs, N_HEADS, seq_s, KV_RANK).transpose(1, 0, 2, 3).reshape(N_HEADS, n_bs * seq_s, KV_RANK)
    x = _mla_out(jnp.concatenate([o_p, o_s], axis=1), x, w_uv, w_o)
    state = (c[:tp].reshape(n_bp, seq_p, KV_RANK), c_s, kr[:tp].reshape(n_bp, seq_p, QK_ROPE), kr_s)
    return x, state


def _pool_kernel(x_ref, prev_ref, gn_ref, w_ref, sc_ref, o_ref, st_ref, ext, *, tl, start, last):
    ti = pl.program_id(1)

    @pl.when(ti == 0)
    def _():
        ext[0:POOL_HALO, :] = prev_ref[...]

    @pl.when(ti > 0)
    def _():
        ext[0:POOL_HALO, :] = ext[tl:tl + POOL_HALO, :]

    x = x_ref[...]
    h = _rms(x, gn_ref[...])
    ext[POOL_HALO:POOL_HALO + tl, :] = h
    pos = start + ti * tl + lax.broadcasted_iota(I32, (tl, 1), 0)
    run = ext[...]
    z = []
    for g, w in enumerate(POOL_WINDOWS):
        run = run[:, (POOL_GROUP if g else 0):]
        run = run + pltpu.roll(run, w // 2, 0)
        total = run[POOL_HALO:POOL_HALO + tl, :POOL_GROUP]
        count = jnp.minimum(w, pos + 1).astype(F32)
        p = total / count - h[:, g * POOL_GROUP:(g + 1) * POOL_GROUP]
        z.append(_dot(p.astype(BF16), w_ref[g]))
    o_ref[...] = x + jnp.concatenate(z, axis=1) * sc_ref[...]

    @pl.when(ti == last)
    def _():
        st_ref[...] = ext[tl:tl + POOL_HALO, :]


def _pool_stream(x, prev, norm_g, pool_w, pool_scale, n_b, seq, row0, start, tl):
    nt = seq // tl
    blk0 = row0 // tl
    prev = jnp.concatenate([jnp.zeros((n_b, 1, D), F32), prev.astype(F32)], axis=1)
    out, st = pl.pallas_call(
        functools.partial(_pool_kernel, tl=tl, start=start, last=nt - 1), grid=(n_b, nt),
        in_specs=[pl.BlockSpec((tl, D), lambda b, t: (blk0 + b * nt + t, 0)),
                  pl.BlockSpec((None, POOL_HALO, D), lambda b, t: (b, 0, 0)),
                  pl.BlockSpec((1, D), lambda b, t: (0, 0)),
                  pl.BlockSpec((4, POOL_GROUP, POOL_GROUP), lambda b, t: (0, 0, 0)),
                  pl.BlockSpec((1, D), lambda b, t: (0, 0))],
        out_specs=[pl.BlockSpec((tl, D), lambda b, t: (b * nt + t, 0)),
                   pl.BlockSpec((None, POOL_HALO, D), lambda b, t: (b, 0, 0))],
        out_shape=[jax.ShapeDtypeStruct((n_b * seq, D), F32), jax.ShapeDtypeStruct((n_b, POOL_HALO, D), F32)],
        scratch_shapes=[pltpu.VMEM((POOL_HALO + tl, D), F32)],
        compiler_params=_cparams(("parallel", "arbitrary")), name="pool_mixer",
    )(x, prev, norm_g.reshape(1, D), pool_w.astype(BF16), pool_scale.reshape(1, D))
    return out, st[:, 1:]


def _pool_layer(x, norm_g, state_pool, params, dims, past_len):
    n_bp, seq_p, n_bs, seq_s = dims
    tp = n_bp * seq_p
    pool_w, pool_scale = params
    zero = jnp.zeros((n_bp, POOL_HALO - 1, D), F32)
    x_p, st_p = _pool_stream(x, zero, norm_g, pool_w, pool_scale, n_bp, seq_p, 0, 0, min(512, seq_p))
    x_s, st_s = _pool_stream(x, state_pool, norm_g, pool_w, pool_scale, n_bs, seq_s, tp, past_len, seq_s)
    return jnp.concatenate([x_p, x_s], axis=0), (st_p, st_s)


def _router_kernel(x_ref, gn_ref, wr_ref, tri_ref, hb_ref, sel_ref, gate_ref, cnt_ref):
    h = _rms(x_ref[...], gn_ref[...])
    hb_ref[...] = h.astype(BF16)
    logits = lax.dot_general(wr_ref[...], h, (((1,), (1,)), ((), ())), precision=lax.Precision.HIGHEST,
                             preferred_element_type=F32)
    idx = lax.broadcasted_iota(I32, logits.shape, 0)
    m1 = jnp.max(logits, axis=0, keepdims=True)
    i1 = jnp.min(jnp.where(logits == m1, idx, N_EXPERTS), axis=0, keepdims=True)
    first = idx == i1
    rest = jnp.where(first, -jnp.inf, logits)
    m2 = jnp.max(rest, axis=0, keepdims=True)
    i2 = jnp.min(jnp.where(rest == m2, idx, N_EXPERTS), axis=0, keepdims=True)
    second = idx == i2
    e2 = jnp.exp(m2 - m1)
    g1 = 1.0 / (1.0 + e2)
    g2 = e2 / (1.0 + e2)
    chosen = jnp.where(first | second, 1.0, 0.0)
    rank = _dot(chosen.astype(BF16), tri_ref[...])
    rank1 = jnp.sum(jnp.where(first, rank, 0.0), axis=0, keepdims=True)
    rank2 = jnp.sum(jnp.where(second, rank, 0.0), axis=0, keepdims=True)
    zero = jnp.zeros((N_EXPERTS - 4,) + i1.shape[1:], I32)
    sel_ref[...] = jnp.concatenate([i1, i2, rank1.astype(I32), rank2.astype(I32), zero], axis=0)
    gate_ref[...] = jnp.concatenate([g1, g2, jnp.zeros((N_EXPERTS - 2,) + g1.shape[1:], F32)], axis=0)
    cnt_ref[...] = jnp.broadcast_to(jnp.sum(chosen, axis=1, keepdims=True), cnt_ref.shape).astype(I32)


def _moe_route(x, norm_g, w_router):
    t = x.shape[0]
    tb = MOE_TB
    nblk = t // tb
    tri = (jnp.arange(tb)[:, None] < jnp.arange(tb)[None, :]).astype(BF16)
    blk = lambda dt: (pl.BlockSpec((None, N_EXPERTS, tb), lambda i: (i, 0, 0)),
                      jax.ShapeDtypeStruct((nblk, N_EXPERTS, tb), dt))
    (sel_spec, sel_shape), (gate_spec, gate_shape) = blk(I32), blk(F32)
    return pl.pallas_call(
        _router_kernel, grid=(nblk,),
        in_specs=[pl.BlockSpec((tb, D), lambda i: (i, 0)), pl.BlockSpec((1, D), lambda i: (0, 0)),
                  pl.BlockSpec((N_EXPERTS, D), lambda i: (0, 0)), pl.BlockSpec((tb, tb), lambda i: (0, 0))],
        out_specs=[pl.BlockSpec((tb, D), lambda i: (i, 0)), sel_spec, gate_spec,
                   pl.BlockSpec((None, N_EXPERTS, 128), lambda i: (i, 0, 0))],
        out_shape=[jax.ShapeDtypeStruct((t, D), BF16), sel_shape, gate_shape,
                   jax.ShapeDtypeStruct((nblk, N_EXPERTS, 128), I32)],
        compiler_params=_cparams(("parallel",)), name="moe_router",
    )(x, norm_g.reshape(1, D), w_router.T.astype(F32), tri)


def _moe_plan(sel, counts, t):
    nblk = sel.shape[0]
    cnt = counts[:, :, 0]
    cpad = (cnt + MOE_ALIGN - 1) // MOE_ALIGN * MOE_ALIGN
    per_e = jnp.sum(cpad, axis=0)
    per_e_pad = (per_e + MOE_TM - 1) // MOE_TM * MOE_TM
    e_start = jnp.cumsum(per_e_pad) - per_e_pad
    off = e_start[None, :] + jnp.cumsum(cpad, axis=0) - cpad
    e1, e2, r1, r2 = sel[:, 0], sel[:, 1], sel[:, 2], sel[:, 3]
    dst1 = jnp.take_along_axis(off, e1, axis=1) + r1
    dst2 = jnp.take_along_axis(off, e2, axis=1) + r2
    nch = (cpad + MOE_CH - 1) // MOE_CH
    cum = jnp.cumsum(nch, axis=1)
    max_chunks = 2 * MOE_TB // MOE_CH + N_EXPERTS
    j = jnp.arange(max_chunks)[None, :]
    ce = jnp.sum(j[:, :, None] >= cum[:, None, :], axis=2)
    ce = jnp.minimum(ce, N_EXPERTS - 1)
    ck = j - (jnp.take_along_axis(cum, ce, axis=1) - jnp.take_along_axis(nch, ce, axis=1))
    valid = j < cum[:, -1:]
    c_row0 = jnp.where(valid, jnp.take_along_axis(off, ce, axis=1) + ck * MOE_CH, 0)
    c_rows = jnp.where(valid, jnp.minimum(jnp.take_along_axis(cpad, ce, axis=1) - ck * MOE_CH, MOE_CH), 0)
    n_chunks = cum[:, -1]
    n_tiles = _moe_rows(t) // MOE_TM
    tile_row = jnp.arange(n_tiles) * MOE_TM
    e_end = e_start + per_e_pad
    tile_e = jnp.sum(tile_row[:, None] >= e_end[None, :], axis=1)
    active = (tile_e < N_EXPERTS).astype(I32)
    tile_e = jnp.where(active == 1, tile_e, N_EXPERTS - 1)
    return dict(dst=jnp.stack([dst1, dst2], axis=1).astype(I32),
                n_chunks=n_chunks.astype(I32), c_row0=c_row0.astype(I32).reshape(-1),
                c_rows=c_rows.astype(I32).reshape(-1), tile_e=tile_e.astype(I32), active=active,
                max_chunks=max_chunks)


def _moe_rows(t):
    nblk = t // MOE_TB
    worst = 2 * t + nblk * N_EXPERTS * (MOE_ALIGN - 1) + N_EXPERTS * (MOE_TM - 1) + MOE_CH
    return (worst + MOE_TM - 1) // MOE_TM * MOE_TM


def _dispatch_kernel(nch_ref, row0_ref, rows_ref, hb_ref, dst_ref, xs_in, xs_ref, buf, sem, *, max_chunks):
    del xs_in
    b = pl.program_id(0)
    n = nch_ref[b]
    base = b * max_chunks

    def pieces(j):
        return rows_ref[base + j] // MOE_PIECE

    def copy(j, i, slot):
        r = pl.multiple_of(row0_ref[base + j] + i * MOE_PIECE, MOE_PIECE)
        src = buf.at[slot, pl.ds(pl.multiple_of(i * MOE_PIECE, MOE_PIECE), MOE_PIECE), :]
        return pltpu.make_async_copy(src, xs_ref.at[pl.ds(r, MOE_PIECE), :], sem.at[slot])

    def drain(j, slot):
        lax.fori_loop(0, pieces(j), lambda i, c: (copy(j, i, slot).wait(), c)[1], 0)

    def chunk(j, carry):
        slot = j % 2

        @pl.when(j >= 2)
        def _():
            drain(j - 2, slot)

        r0 = row0_ref[base + j]
        want = r0 + lax.broadcasted_iota(I32, (MOE_CH, 1), 0)
        hit = (dst_ref[0:1, :] == want) | (dst_ref[1:2, :] == want)
        onehot = jnp.where(hit, 1.0, 0.0).astype(BF16)
        buf[slot] = _dot(onehot, hb_ref[...]).astype(BF16)
        lax.fori_loop(0, pieces(j), lambda i, c: (copy(j, i, slot).start(), c)[1], 0)
        return carry

    lax.fori_loop(0, n, chunk, 0)

    @pl.when(n >= 2)
    def _():
        drain(n - 2, n % 2)

    @pl.when(n >= 1)
    def _():
        drain(n - 1, (n - 1) % 2)


def _moe_dispatch(hb, plan):
    t = hb.shape[0]
    nblk = t // MOE_TB
    rows = _moe_rows(t)
    grid_spec = pltpu.PrefetchScalarGridSpec(
        num_scalar_prefetch=3, grid=(nblk,),
        in_specs=[pl.BlockSpec((MOE_TB, D), lambda b, *_: (b, 0)),
                  pl.BlockSpec((None, 2, MOE_TB), lambda b, *_: (b, 0, 0)),
                  pl.BlockSpec(memory_space=pl.ANY)],
        out_specs=pl.BlockSpec(memory_space=pl.ANY),
        scratch_shapes=[pltpu.VMEM((2, MOE_CH, D), BF16), pltpu.SemaphoreType.DMA((2,))])
    return pl.pallas_call(
        functools.partial(_dispatch_kernel, max_chunks=plan["max_chunks"]), grid_spec=grid_spec,
        out_shape=jax.ShapeDtypeStruct((rows, D), BF16),
        input_output_aliases={5: 0},
        compiler_params=_cparams(("arbitrary",)), name="moe_dispatch",
    )(plan["n_chunks"], plan["c_row0"], plan["c_rows"], hb, plan["dst"], jnp.zeros((rows, D), BF16))


def _expert_kernel(te_ref, act_ref, x_ref, wg_ref, wu_ref, wo_ref, o_ref, acc, *, nf):
    i, f = pl.program_id(0), pl.program_id(1)

    @pl.when(f == 0)
    def _():
        acc[...] = jnp.zeros_like(acc)

    @pl.when(act_ref[i] == 1)
    def _():
        x = x_ref[...]
        t = (jax.nn.silu(_dot(x, wg_ref[...])) * _dot(x, wu_ref[...])).astype(BF16)
        acc[...] += _dot(t, wo_ref[...])

    @pl.when(f == nf - 1)
    def _():
        o_ref[...] = acc[...].astype(o_ref.dtype)


def _moe_experts(xs, w_in, w_out, plan, tf=512):
    rows = xs.shape[0]
    d_ff = w_out.shape[1]
    nf = d_ff // tf
    fe = lambda i, f, act: jnp.where(act[i] == 1, f, nf - 1)
    grid_spec = pltpu.PrefetchScalarGridSpec(
        num_scalar_prefetch=2, grid=(rows // MOE_TM, nf),
        in_specs=[pl.BlockSpec((MOE_TM, D), lambda i, f, te, act: (i, 0)),
                  pl.BlockSpec((None, D, tf), lambda i, f, te, act: (te[i], 0, fe(i, f, act))),
                  pl.BlockSpec((None, D, tf), lambda i, f, te, act: (te[i], 0, fe(i, f, act) + nf)),
                  pl.BlockSpec((None, tf, D), lambda i, f, te, act: (te[i], fe(i, f, act), 0))],
        out_specs=pl.BlockSpec((MOE_TM, D), lambda i, f, te, act: (i, 0)),
        scratch_shapes=[pltpu.VMEM((MOE_TM, D), F32)])
    return pl.pallas_call(
        functools.partial(_expert_kernel, nf=nf), grid_spec=grid_spec,
        out_shape=jax.ShapeDtypeStruct((rows, D), BF16),
        compiler_params=_cparams(("parallel", "arbitrary")), name="moe_experts",
    )(plan["tile_e"], plan["active"], xs, w_in, w_in, w_out)


def _combine_kernel(nch_ref, row0_ref, rows_ref, x_ref, dst_ref, gate_ref, ys_ref, o_ref, buf, sem, *, max_chunks):
    b = pl.program_id(0)
    n = nch_ref[b]
    base = b * max_chunks

    def copy(j, slot):
        r = pl.multiple_of(row0_ref[base + j], MOE_ALIGN)
        return pltpu.make_async_copy(ys_ref.at[pl.ds(r, MOE_CH), :], buf.at[slot], sem.at[slot])

    @pl.when(n >= 1)
    def _():
        copy(0, 0).start()

    o_ref[...] = x_ref[...]
    d1, d2 = dst_ref[:, 0:1], dst_ref[:, 1:2]
    g1, g2 = gate_ref[:, 0:1], gate_ref[:, 1:2]

    def chunk(j, carry):
        slot = j % 2

        @pl.when(j + 1 < n)
        def _():
            copy(j + 1, 1 - slot).start()

        r0 = row0_ref[base + j]
        end = r0 + rows_ref[base + j]
        lane = lax.broadcasted_iota(I32, (1, MOE_CH), 1)
        want = jnp.where(r0 + lane < end, r0 + lane, -1)
        onehot = jnp.where((d1 == want) | (d2 == want), 1.0, 0.0).astype(BF16)
        in1 = (d1 >= r0) & (d1 < end)
        in2 = (d2 >= r0) & (d2 < end)
        scale = jnp.where(in1, g1, 0.0) + jnp.where(in2, g2, 0.0)
        copy(j, slot).wait()
        o_ref[...] += scale * _dot(onehot, buf[slot])
        return carry

    lax.fori_loop(0, n, chunk, 0)


def _moe_combine(x, ys, dst_tok, gate_tok, plan):
    t = x.shape[0]
    nblk = t // MOE_TB
    grid_spec = pltpu.PrefetchScalarGridSpec(
        num_scalar_prefetch=3, grid=(nblk,),
        in_specs=[pl.BlockSpec((MOE_TB, D), lambda b, *_: (b, 0)),
                  pl.BlockSpec((MOE_TB, 2), lambda b, *_: (b, 0)),
                  pl.BlockSpec((MOE_TB, 2), lambda b, *_: (b, 0)),
                  pl.BlockSpec(memory_space=pl.ANY)],
        out_specs=pl.BlockSpec((MOE_TB, D), lambda b, *_: (b, 0)),
        scratch_shapes=[pltpu.VMEM((2, MOE_CH, D), BF16), pltpu.SemaphoreType.DMA((2,))])
    return pl.pallas_call(
        functools.partial(_combine_kernel, max_chunks=plan["max_chunks"]), grid_spec=grid_spec,
        out_shape=jax.ShapeDtypeStruct((t, D), F32),
        compiler_params=_cparams(("arbitrary",)), name="moe_combine",
    )(plan["n_chunks"], plan["c_row0"], plan["c_rows"], x, dst_tok, gate_tok, ys)


def _moe_layer(x, norm_g, w_router, w_in, w_out):
    t = x.shape[0]
    hb, sel, gate, counts = _moe_route(x, norm_g, w_router)
    plan = _moe_plan(sel, counts, t)
    xs = _moe_dispatch(hb, plan)
    ys = _moe_experts(xs, w_in.astype(BF16), w_out.astype(BF16), plan)
    dst_tok = plan["dst"].transpose(0, 2, 1).reshape(t, 2)
    gate_tok = gate[:, :2].transpose(0, 2, 1).reshape(t, 2)
    return _moe_combine(x, ys, dst_tok, gate_tok, plan)


def kernel(x_prompt, x_sample, state_l0_s5_re, state_l0_s5_im, cache_l1_latent, cache_l1_krope, state_l2_pool, state_l3_s5_re, state_l3_s5_im, page_table, l0_norm_mix, l0_s5_lambda_re, l0_s5_lambda_im, l0_s5_log_dt, l0_s5_b_re, l0_s5_b_im, l0_s5_c_re, l0_s5_c_im, l0_s5_d, l0_s5_w_glu, l0_norm_ffn, l0_ffn_w_in, l0_ffn_w_out, l1_norm_mix, l1_mla_w_in, l1_mla_norm_q, l1_mla_norm_kv, l1_mla_w_q, l1_mla_w_uk, l1_mla_w_uv, l1_mla_qn_nope, l1_mla_qn_rope, l1_mla_kn_nope, l1_mla_kn_rope, l1_mla_w_o, l1_norm_ffn, l1_moe_w_router, l1_moe_w_in, l1_moe_w_out, l2_norm_mix, l2_pool_w, l2_pool_scale, l2_norm_ffn, l2_ffn_w_in, l2_ffn_w_out, l3_norm_mix, l3_s5_lambda_re, l3_s5_lambda_im, l3_s5_log_dt, l3_s5_b_re, l3_s5_b_im, l3_s5_c_re, l3_s5_c_im, l3_s5_d, l3_s5_w_glu, l3_norm_ffn, l3_moe_w_router, l3_moe_w_in, l3_moe_w_out):
    n_bp, seq_p, _ = x_prompt.shape
    n_bs, seq_s, _ = x_sample.shape
    dims = (n_bp, seq_p, n_bs, seq_s)
    tp = n_bp * seq_p
    past_len = page_table.shape[1] * cache_l1_latent.shape[1]
    x = jnp.concatenate([x_prompt.reshape(tp, D), x_sample.reshape(n_bs * seq_s, D)], axis=0)

    x, st0 = _s5_layer(x, l0_norm_mix, state_l0_s5_re, state_l0_s5_im,
                       (l0_s5_lambda_re, l0_s5_lambda_im, l0_s5_log_dt, l0_s5_b_re, l0_s5_b_im,
                        l0_s5_c_re, l0_s5_c_im, l0_s5_d, l0_s5_w_glu), dims)
    x = _ffn_dense(x, l0_norm_ffn, l0_ffn_w_in, l0_ffn_w_out)

    x, st1 = _mla_layer(x, l1_norm_mix, cache_l1_latent, cache_l1_krope, page_table,
                        (l1_mla_w_in, l1_mla_norm_q, l1_mla_norm_kv, l1_mla_w_q, l1_mla_w_uk, l1_mla_w_uv,
                         l1_mla_qn_nope, l1_mla_qn_rope, l1_mla_kn_nope, l1_mla_kn_rope, l1_mla_w_o), dims)
    x = _moe_layer(x, l1_norm_ffn, l1_moe_w_router, l1_moe_w_in, l1_moe_w_out)

    x, st2 = _pool_layer(x, l2_norm_mix, state_l2_pool, (l2_pool_w, l2_pool_scale), dims, past_len)
    x = _ffn_dense(x, l2_norm_ffn, l2_ffn_w_in, l2_ffn_w_out)

    x, st3 = _s5_layer(x, l3_norm_mix, state_l3_s5_re, state_l3_s5_im,
                       (l3_s5_lambda_re, l3_s5_lambda_im, l3_s5_log_dt, l3_s5_b_re, l3_s5_b_im,
                        l3_s5_c_re, l3_s5_c_im, l3_s5_d, l3_s5_w_glu), dims)
    x = _moe_layer(x, l3_norm_ffn, l3_moe_w_router, l3_moe_w_in, l3_moe_w_out)

    return (x[:tp].reshape(n_bp, seq_p, D), x[tp:].reshape(n_bs, seq_s, D)) + st0 + st1 + st2 + st3
```

```python
import functools
import math

import jax
import jax.numpy as jnp
from jax import lax
from jax.experimental import pallas as pl
from jax.experimental.pallas import tpu as pltpu

F32 = jnp.float32
BF16 = jnp.bfloat16
I32 = jnp.int32

D = 1024
NORM_EPS = 1e-6
NEG_INF = -1e30
S5_GROUP = 16
S5_STATE = 64
N_GROUPS = D // S5_GROUP
S5_T = 8
S5_BUNDLE = 8
N_BUNDLES = N_GROUPS // S5_BUNDLE
S5_W = S5_T * 128
N_HEADS = 8
QK_NOPE = 128
QK_ROPE = 64
Q_RANK = 512
KV_RANK = 256
ROPE_THETA = 10000.0
ATTN_SCALE = (QK_NOPE + QK_ROPE) ** -0.5
PAGE = 128
MAX_PAGES_PER_STEP = 16
KEY_BLOCK = 512
POOL_WINDOWS = (2, 4, 8, 16)
POOL_HALO = 16
POOL_GROUP = D // 4
N_EXPERTS = 8
MOE_TB = 1024
MOE_TM = 1024
MOE_ALIGN = 16
MOE_KL = (2 * MOE_TB + N_EXPERTS * (MOE_ALIGN - 1) + 127) // 128 * 128
MOE_PIECES = MOE_KL // MOE_ALIGN
MOE_XW = D + 128


def _cparams(sem, vmem_mb=48):
    return pltpu.CompilerParams(dimension_semantics=sem, vmem_limit_bytes=vmem_mb * 1024 * 1024)


def _rms(x, g):
    r = lax.rsqrt(jnp.mean(x * x, axis=-1, keepdims=True) + NORM_EPS)
    return x * r * g


def _dot(a, b):
    return jnp.dot(a, b, preferred_element_type=F32)


def _dot_nt(a, b):
    return lax.dot_general(a, b, (((1,), (1,)), ((), ())), preferred_element_type=F32)


def _lo_map(n):
    return lambda i: (jnp.minimum(i, n - 1), 0)


def _hi_map(n):
    return lambda i: (jnp.maximum(i - n, 0), 0)


def _rmsnorm_kernel(x_ref, g_ref, o_ref):
    o_ref[...] = _rms(x_ref[...], g_ref[...]).astype(o_ref.dtype)


def _rmsnorm(x, g, row0, rows, tm=1024):
    tm = min(tm, rows)
    blk0 = row0 // tm
    return pl.pallas_call(
        _rmsnorm_kernel, grid=(rows // tm,),
        in_specs=[pl.BlockSpec((tm, D), lambda i: (blk0 + i, 0)), pl.BlockSpec((1, D), lambda i: (0, 0))],
        out_specs=pl.BlockSpec((tm, D), lambda i: (i, 0)),
        out_shape=jax.ShapeDtypeStruct((rows, D), F32),
        compiler_params=_cparams(("parallel",)), name="rmsnorm",
    )(x, g.reshape(1, D))


def _s5_weights(lam_re, lam_im, log_dt, b_re, b_im, c_re, c_im, d):
    lr, li = lam_re.astype(F32), lam_im.astype(F32)
    dt = jnp.exp(log_dt.astype(F32))[:, None]
    mag = jnp.exp(lr * dt)
    ar, ai = mag * jnp.cos(li * dt), mag * jnp.sin(li * dt)
    nr, ni = ar - 1.0, ai
    den = lr * lr + li * li
    f_re = ((nr * lr + ni * li) / den)[..., None]
    f_im = ((ni * lr - nr * li) / den)[..., None]
    br, bi = b_re.astype(F32), b_im.astype(F32)
    bb_re, bb_im = f_re * br - f_im * bi, f_re * bi + f_im * br
    cr, ci = c_re.astype(F32), c_im.astype(F32)

    pr, pi = [jnp.ones_like(ar)], [jnp.zeros_like(ar)]
    for _ in range(S5_T):
        pr.append(pr[-1] * ar - pi[-1] * ai)
        pi.append(pr[-2] * ai + pi[-1] * ar)
    pr, pi = jnp.stack(pr), jnp.stack(pi)

    ab_re = pr[:S5_T, :, :, None] * bb_re - pi[:S5_T, :, :, None] * bb_im
    ab_im = pr[:S5_T, :, :, None] * bb_im + pi[:S5_T, :, :, None] * bb_re
    hi = lax.Precision.HIGHEST
    kern = (jnp.einsum("gcp,jgpd->jgcd", cr, ab_re, precision=hi)
            - jnp.einsum("gcp,jgpd->jgcd", ci, ab_im, precision=hi))

    nb, bg, t = N_BUNDLES, S5_BUNDLE, S5_T
    same = jnp.eye(bg, dtype=jnp.bool_)
    tt = jnp.arange(t)
    lag = tt[None, :] - tt[:, None]
    ksel = jnp.where((lag >= 0)[:, :, None, None, None], kern[jnp.clip(lag, 0, t - 1)], 0.0)
    ksel = ksel.reshape(t, t, nb, bg, S5_GROUP, S5_GROUP).transpose(2, 0, 3, 5, 1, 4)
    m = jnp.where(same[None, None, :, None, None, :, None], ksel[:, :, :, :, :, None, :], 0.0)
    m = m.astype(BF16).reshape(nb, t * 128, t * 128)

    inj = jnp.stack([ab_re[::-1], ab_im[::-1]])
    inj = inj.reshape(2, t, nb, bg, S5_STATE, S5_GROUP).transpose(2, 1, 3, 5, 0, 4)
    we = jnp.where(same[None, None, :, None, None, :, None], inj[:, :, :, :, :, None, :], 0.0)
    we = we.astype(BF16).reshape(nb, t * 128, 2 * bg * S5_STATE)

    p1r, p1i = pr[1:, :, None, :], pi[1:, :, None, :]
    co_re = cr[None] * p1r - ci[None] * p1i
    co_im = -(cr[None] * p1i + ci[None] * p1r)
    co = jnp.stack([co_re, co_im]).reshape(2, t, nb, bg, S5_GROUP, S5_STATE).transpose(2, 0, 3, 5, 1, 4)
    ws = jnp.where(same[None, None, :, None, None, :, None], co[:, :, :, :, :, None, :], 0.0)
    ws = ws.astype(BF16).reshape(nb, 2 * bg * S5_STATE, t * 128)

    a_t = jnp.stack([pr[t], pi[t]]).reshape(2, nb, bg * S5_STATE)
    a_t = a_t.transpose(1, 0, 2).reshape(nb, 1, 2 * bg * S5_STATE)
    dv = jnp.broadcast_to(d.astype(F32).reshape(1, nb, 128), (t, nb, 128)).transpose(1, 0, 2).reshape(nb, 1, t * 128)
    return m, we, ws, a_t, dv


def _s5_kernel(u_ref, s0_ref, m_ref, we_ref, ws_ref, a_ref, d_ref, g_ref, sfin_ref, state, ubuf, sprev, ebuf,
               *, nb, cb, last):
    i = pl.program_id(1)
    half = S5_BUNDLE * S5_STATE

    @pl.when(i == 0)
    def _():
        state[...] = s0_ref[...]

    for c in range(cb):
        for t in range(S5_T):
            ubuf[c * nb:(c + 1) * nb, t * 128:(t + 1) * 128] = u_ref[:, c * S5_T + t, :]
    u = ubuf[...]
    ub = u.astype(BF16)
    ebuf[...] = _dot(ub, we_ref[...])
    ar, ai = a_ref[:, :half], a_ref[:, half:]
    s = state[...]
    for c in range(cb):
        rows = slice(c * nb, (c + 1) * nb)
        sprev[rows, :] = s
        sr, si = s[:, :half], s[:, half:]
        s = jnp.concatenate([ar * sr - ai * si + ebuf[rows, :half],
                             ar * si + ai * sr + ebuf[rows, half:]], axis=1)
    state[...] = s
    y = _dot(ub, m_ref[...]) + _dot(sprev[...].astype(BF16), ws_ref[...])
    ebuf[...] = jax.nn.gelu(y + d_ref[...] * u)
    for c in range(cb):
        for t in range(S5_T):
            g_ref[:, c * S5_T + t, :] = ebuf[c * nb:(c + 1) * nb, t * 128:(t + 1) * 128]

    @pl.when(i == last)
    def _():
        sfin_ref[...] = s


def _s5_scan(h, s0, tables, cb):
    m, we, ws, a_t, dv = tables
    nb, seq, _ = h.shape
    tok = S5_T * cb
    ni = seq // tok
    r = nb * cb
    w = S5_W
    kern = functools.partial(_s5_kernel, nb=nb, cb=cb, last=ni - 1)
    big = lambda: pl.BlockSpec((None, w, w), lambda g, i: (g, 0, 0))
    vec = lambda: pl.BlockSpec((None, 1, w), lambda g, i: (g, 0, 0))
    return pl.pallas_call(
        kern, grid=(N_BUNDLES, ni),
        in_specs=[pl.BlockSpec((nb, tok, 128), lambda g, i: (0, i, g)),
                  pl.BlockSpec((None, nb, w), lambda g, i: (g, 0, 0)),
                  big(), big(), big(), vec(), vec()],
        out_specs=[pl.BlockSpec((nb, tok, 128), lambda g, i: (0, i, g)),
                   pl.BlockSpec((None, nb, w), lambda g, i: (g, 0, 0))],
        out_shape=[jax.ShapeDtypeStruct((nb, seq, D), F32),
                   jax.ShapeDtypeStruct((N_BUNDLES, nb, w), F32)],
        scratch_shapes=[pltpu.VMEM((nb, w), F32), pltpu.VMEM((r, w), F32), pltpu.VMEM((r, w), F32),
                        pltpu.VMEM((r, w), F32)],
        compiler_params=_cparams(("parallel", "arbitrary")), name="s5_scan",
    )(h, s0, m, we, ws, a_t, dv)


def _state_to_bundles(s_re, s_im):
    n_b = s_re.shape[0]
    s = jnp.stack([s_re, s_im], axis=1).reshape(n_b, 2, N_BUNDLES, S5_BUNDLE * S5_STATE)
    return s.transpose(2, 0, 1, 3).reshape(N_BUNDLES, n_b, 2 * S5_BUNDLE * S5_STATE)


def _state_from_bundles(s):
    n_b = s.shape[1]
    s = s.reshape(N_BUNDLES, n_b, 2, S5_BUNDLE, S5_STATE).transpose(2, 1, 0, 3, 4)
    s = s.reshape(2, n_b, N_GROUPS, S5_STATE)
    return s[0], s[1]


def _glu_kernel(*refs, n0, two_x):
    if two_x:
        gp_ref, gs_ref, xp_ref, xs_ref, w_ref, o_ref = refs
    else:
        gp_ref, gs_ref, xp_ref, w_ref, o_ref = refs
    first = pl.program_id(0) < n0
    g = jnp.where(first, gp_ref[...], gs_ref[...])
    x = jnp.where(first, xp_ref[...], xs_ref[...]) if two_x else xp_ref[...]
    z = _dot(g.astype(BF16), w_ref[...])
    o_ref[...] = x + z[:, :D] * jax.nn.sigmoid(z[:, D:])


def _glu_residual(g_p, g_s, x_parts, w_glu, tm=512):
    t = g_p.shape[0] + g_s.shape[0]
    n0 = g_p.shape[0] // tm
    two_x = len(x_parts) == 2
    tile = lambda m: pl.BlockSpec((tm, D), m)
    x_specs = [tile(_lo_map(n0)), tile(_hi_map(n0))] if two_x else [tile(lambda i: (i, 0))]
    return pl.pallas_call(
        functools.partial(_glu_kernel, n0=n0, two_x=two_x), grid=(t // tm,),
        in_specs=[tile(_lo_map(n0)), tile(_hi_map(n0))] + x_specs + [pl.BlockSpec((D, 2 * D), lambda i: (0, 0))],
        out_specs=tile(lambda i: (i, 0)),
        out_shape=jax.ShapeDtypeStruct((t, D), F32),
        compiler_params=_cparams(("parallel",)), name="glu_residual",
    )(g_p, g_s, *x_parts, w_glu)


def _s5_layer(x_parts, norm_g, state_re, state_im, params, dims):
    n_bp, seq_p, n_bs, seq_s = dims
    tp, ts = n_bp * seq_p, n_bs * seq_s
    lam_re, lam_im, log_dt, b_re, b_im, c_re, c_im, d, w_glu = params
    tables = _s5_weights(lam_re, lam_im, log_dt, b_re, b_im, c_re, c_im, d)
    if len(x_parts) == 2:
        h_p = _rmsnorm(x_parts[0], norm_g, 0, tp)
        h_s = _rmsnorm(x_parts[1], norm_g, 0, ts)
    else:
        h_p = _rmsnorm(x_parts[0], norm_g, 0, tp)
        h_s = _rmsnorm(x_parts[0], norm_g, tp, ts)
    zeros = jnp.zeros((N_BUNDLES, n_bp, 2 * S5_BUNDLE * S5_STATE), F32)
    cb = math.gcd(seq_p // S5_T, max(1, 512 // n_bp))
    g_p, s_p = _s5_scan(h_p.reshape(n_bp, seq_p, D), zeros, tables, cb)
    g_s, s_s = _s5_scan(h_s.reshape(n_bs, seq_s, D), _state_to_bundles(state_re, state_im), tables, seq_s // S5_T)
    x = _glu_residual(g_p.reshape(tp, D), g_s.reshape(ts, D), x_parts, w_glu.astype(BF16))
    p_re, p_im = _state_from_bundles(s_p)
    s_re, s_im = _state_from_bundles(s_s)
    return x, (p_re, s_re, p_im, s_im)


def _ffn_kernel(x_ref, gn_ref, wg_ref, wu_ref, wo_ref, o_ref, h_scr, acc, *, nf):
    f = pl.program_id(1)

    @pl.when(f == 0)
    def _():
        h_scr[...] = _rms(x_ref[...], gn_ref[...]).astype(BF16)
        acc[...] = jnp.zeros_like(acc)

    h = h_scr[...]
    t = (jax.nn.silu(_dot(h, wg_ref[...])) * _dot(h, wu_ref[...])).astype(BF16)
    acc[...] += _dot(t, wo_ref[...])

    @pl.when(f == nf - 1)
    def _():
        o_ref[...] = x_ref[...] + acc[...]


def _ffn_dense(x, norm_g, w_in, w_out, tm=512, nf=2):
    t = x.shape[0]
    d_ff = w_out.shape[0]
    tf = d_ff // nf
    w_in = w_in.astype(BF16)
    return pl.pallas_call(
        functools.partial(_ffn_kernel, nf=nf), grid=(t // tm, nf),
        in_specs=[pl.BlockSpec((tm, D), lambda i, f: (i, 0)), pl.BlockSpec((1, D), lambda i, f: (0, 0)),
                  pl.BlockSpec((D, tf), lambda i, f: (0, f)), pl.BlockSpec((D, tf), lambda i, f: (0, f + nf)),
                  pl.BlockSpec((tf, D), lambda i, f: (f, 0))],
        out_specs=pl.BlockSpec((tm, D), lambda i, f: (i, 0)),
        out_shape=jax.ShapeDtypeStruct((t, D), F32),
        scratch_shapes=[pltpu.VMEM((tm, D), BF16), pltpu.VMEM((tm, D), F32)],
        compiler_params=_cparams(("parallel", "arbitrary")), name="ffn_dense",
    )(x, norm_g.reshape(1, D), w_in, w_in, w_out.astype(BF16))


def _swap_halves(w):
    half = w.shape[-1] // 2
    return jnp.concatenate([w[..., half:], w[..., :half]], axis=-1)


def _pad_lanes(w, n=128):
    return jnp.concatenate([w, jnp.zeros(w.shape[:-1] + (n - w.shape[-1],), w.dtype)], axis=-1)


def _rope_tables(seq_p, seq_s, past_len, rows_s):
    half = QK_ROPE // 2
    inv_freq = ROPE_THETA ** (-jnp.arange(half, dtype=F32) / half)
    pos = jnp.concatenate([jnp.arange(seq_p), past_len + (jnp.arange(rows_s) % seq_s)])
    ang = pos.astype(F32)[:, None] * inv_freq[None, :]
    cos, sin = jnp.cos(ang), jnp.sin(ang)
    return _pad_lanes(jnp.concatenate([cos, cos], axis=1)), _pad_lanes(jnp.concatenate([-sin, sin], axis=1))


def _rope_norm(x, x_sw, g, g_sw, cos, sin):
    r = lax.rsqrt(jnp.sum(x * x, axis=-1, keepdims=True) * (1.0 / QK_ROPE) + NORM_EPS)
    return r * (x * g * cos + x_sw * g_sw * sin)


def _mla_proj_kernel(x_ref, gmix, win, nq, nkv, cos, sin, gkr, gkrs, wq, gqn, gqr, gqrs, wuk, gkn,
                     cp_out, cs_out, krp_out, krs_out, cbf_out, q_out, k_out, *, n0):
    is_prompt = pl.program_id(0) < n0
    hn = _rms(x_ref[...], gmix[...]).astype(BF16)
    a = _dot(hn, win[...])
    hq = _rms(a[:, :Q_RANK], nq[...]).astype(BF16)
    c = _rms(a[:, Q_RANK:Q_RANK + KV_RANK], nkv[...])
    cb = c.astype(BF16)
    kr = _rope_norm(a[:, 768:896], a[:, 896:1024], gkr[...], gkrs[...], cos[...], sin[...])
    krb = kr.astype(BF16)

    @pl.when(is_prompt)
    def _():
        cp_out[...] = c
        krp_out[...] = kr
        cbf_out[...] = cb

    @pl.when(jnp.logical_not(is_prompt))
    def _():
        cs_out[...] = c
        krs_out[...] = kr

    for h in range(N_HEADS):
        q3 = _dot(hq, wq[:, h * 384:(h + 1) * 384])
        qn = _rms(q3[:, :128], gqn[...])
        qr = _rope_norm(q3[:, 128:256], q3[:, 256:384], gqr[...], gqrs[...], cos[...], sin[...])
        q_out[h] = jnp.concatenate([qn, qr], axis=1).astype(BF16)

    @pl.when(is_prompt)
    def _():
        for h in range(N_HEADS):
            kn = _rms(_dot(cb, wuk[:, h * QK_NOPE:(h + 1) * QK_NOPE]), gkn[...])
            k_out[h] = jnp.concatenate([kn.astype(BF16), krb], axis=1)


def _mla_proj(x, norm_g, params, dims, past_len, tm=512):
    n_bp, seq_p, n_bs, seq_s = dims
    t = x.shape[0]
    tp = n_bp * seq_p
    ts = t - tp
    w_in, norm_q, norm_kv, w_q, w_uk, w_uv, qn_nope, qn_rope, kn_nope, kn_rope, w_o = params
    w_kr = w_in[:, Q_RANK + KV_RANK:]
    win = jnp.concatenate([w_in[:, :Q_RANK + KV_RANK], _pad_lanes(w_kr), _pad_lanes(_swap_halves(w_kr))],
                          axis=1).astype(BF16)
    wq3 = w_q.reshape(Q_RANK, N_HEADS, QK_NOPE + QK_ROPE)
    wq_r = wq3[:, :, QK_NOPE:]
    wq = jnp.concatenate([wq3[:, :, :QK_NOPE], _pad_lanes(wq_r), _pad_lanes(_swap_halves(wq_r))], axis=2)
    wq = wq.reshape(Q_RANK, N_HEADS * 384).astype(BF16)
    wuk = w_uk.reshape(KV_RANK, N_HEADS * QK_NOPE).astype(BF16)
    cos, sin = _rope_tables(seq_p, seq_s, past_len, min(tm, ts))
    n_pt = seq_p // tm
    n0 = tp // tm
    tab = lambda i: (jnp.where(i < n0, i % n_pt, n_pt), 0)
    row = lambda v: v.reshape(1, -1).astype(F32)
    vec = lambda n: pl.BlockSpec((1, n), lambda i: (0, 0))
    full = lambda a: pl.BlockSpec(a.shape, lambda i: (0, 0))
    lo = lambda n: pl.BlockSpec((tm, n), _lo_map(n0))
    hi = lambda n: pl.BlockSpec((tm, n), _hi_map(n0))
    sds = jax.ShapeDtypeStruct
    return pl.pallas_call(
        functools.partial(_mla_proj_kernel, n0=n0), grid=(t // tm,),
        in_specs=[pl.BlockSpec((tm, D), lambda i: (i, 0)), vec(D), full(win), vec(Q_RANK), vec(KV_RANK),
                  pl.BlockSpec((tm, 128), tab), pl.BlockSpec((tm, 128), tab), vec(128), vec(128),
                  full(wq), vec(128), vec(128), vec(128), full(wuk), vec(128)],
        out_specs=[lo(KV_RANK), hi(KV_RANK), lo(128), hi(128), lo(KV_RANK),
                   pl.BlockSpec((N_HEADS, tm, 256), lambda i: (0, i, 0)),
                   pl.BlockSpec((N_HEADS, tm, 256), lambda i: (0, jnp.minimum(i, n0 - 1), 0))],
        out_shape=[sds((tp, KV_RANK), F32), sds((ts, KV_RANK), F32), sds((tp, 128), F32), sds((ts, 128), F32),
                   sds((tp, KV_RANK), BF16), sds((N_HEADS, t, 256), BF16), sds((N_HEADS, tp, 256), BF16)],
        compiler_params=_cparams(("arbitrary",)), name="mla_proj",
    )(x, row(norm_g), win, row(norm_q), row(norm_kv), cos, sin,
      row(_pad_lanes(kn_rope)), row(_pad_lanes(_swap_halves(kn_rope))), wq, row(qn_nope),
      row(_pad_lanes(qn_rope)), row(_pad_lanes(_swap_halves(qn_rope))), wuk, row(kn_nope))


def _flash_kernel(qi_tab, ki_tab, q_ref, k_ref, v_ref, o_ref, m, l, acc):
    step = pl.program_id(1)
    qi, ki = qi_tab[step], ki_tab[step]

    @pl.when(ki == 0)
    def _():
        m[...] = jnp.full_like(m, NEG_INF)
        l[...] = jnp.zeros_like(l)
        acc[...] = jnp.zeros_like(acc)

    def update(h, masked):
        s = _dot_nt(q_ref[h], k_ref[h]) * ATTN_SCALE
        if masked:
            r = lax.broadcasted_iota(I32, s.shape, 0)
            c = lax.broadcasted_iota(I32, s.shape, 1)
            s = jnp.where(c <= r, s, NEG_INF)
        m_new = jnp.maximum(m[h], jnp.max(s, axis=-1, keepdims=True))
        corr = jnp.exp(m[h] - m_new)
        p = jnp.exp(s - m_new)
        l[h] = l[h] * corr + jnp.sum(p, axis=-1, keepdims=True)
        acc[h] = acc[h] * corr + _dot(p.astype(BF16), v_ref[...])
        m[h] = m_new

    def heads(masked):
        def pair(hp, carry):
            update(2 * hp, masked)
            update(2 * hp + 1, masked)
            return carry
        lax.fori_loop(0, N_HEADS // 2, pair, 0)

    @pl.when(ki < qi)
    def _():
        heads(False)

    @pl.when(ki == qi)
    def _():
        heads(True)
        for h in range(N_HEADS):
            o_ref[h] = (acc[h] / l[h]).astype(o_ref.dtype)


def _mla_prompt_attn(qcat, kcat, cbf, n_b, seq, tq=512):
    nq = seq // tq
    pairs = [(qi, ki) for qi in range(nq) for ki in range(qi + 1)]
    qi_tab = jnp.array([p[0] for p in pairs], I32)
    ki_tab = jnp.array([p[1] for p in pairs], I32)
    grid_spec = pltpu.PrefetchScalarGridSpec(
        num_scalar_prefetch=2, grid=(n_b, len(pairs)),
        in_specs=[pl.BlockSpec((N_HEADS, tq, 256), lambda b, s, qt, kt: (0, b * nq + qt[s], 0)),
                  pl.BlockSpec((N_HEADS, tq, 256), lambda b, s, qt, kt: (0, b * nq + kt[s], 0)),
                  pl.BlockSpec((tq, KV_RANK), lambda b, s, qt, kt: (b * nq + kt[s], 0))],
        out_specs=pl.BlockSpec((N_HEADS, tq, KV_RANK), lambda b, s, qt, kt: (0, b * nq + qt[s], 0)),
        scratch_shapes=[pltpu.VMEM((N_HEADS, tq, 1), F32), pltpu.VMEM((N_HEADS, tq, 1), F32),
                        pltpu.VMEM((N_HEADS, tq, KV_RANK), F32)])
    return pl.pallas_call(
        _flash_kernel, grid_spec=grid_spec,
        out_shape=jax.ShapeDtypeStruct((N_HEADS, n_b * seq, KV_RANK), BF16),
        compiler_params=_cparams(("parallel", "arbitrary")), name="mla_prompt_attn",
    )(qi_tab, ki_tab, qcat, kcat, cbf)


def _sample_attn_kernel(pt_ref, *refs, npg, n_chunks, dq):
    lat = refs[:npg]
    krp = refs[npg:2 * npg]
    cnew, krnew, q_ref, wuk_ref, wukt_ref, gkn_ref, o_ref, lhs, ct, cbuf, kbuf, m, l, acc = refs[2 * npg:]
    kc = pl.program_id(1)
    nhq = N_HEADS * dq
    nkd = N_HEADS * QK_NOPE

    @pl.when(kc == 0)
    def _():
        m[...] = jnp.full_like(m, NEG_INF)
        l[...] = jnp.zeros_like(l)
        acc[...] = jnp.zeros_like(acc)
        lhs[0:nkd, :] = wukt_ref[...]
        qn = (q_ref[:, :QK_NOPE] * gkn_ref[...]).astype(BF16)
        qabs = [_dot_nt(qn[h * dq:(h + 1) * dq], wuk_ref[h]) for h in range(N_HEADS)]
        lhs[nkd:nkd + nhq, :] = jnp.concatenate(qabs, axis=0).astype(BF16)

    def scores(lo, nk):
        r = _dot(lhs[...], ct[:, lo:lo + nk])
        rinv = []
        for h in range(N_HEADS):
            kn = r[h * QK_NOPE:(h + 1) * QK_NOPE]
            ssq = jnp.sum(kn * kn, axis=0, keepdims=True)
            rinv.append(jnp.broadcast_to(lax.rsqrt(ssq * (1.0 / QK_NOPE) + NORM_EPS), (dq, nk)))
        qr = q_ref[:, QK_NOPE:QK_NOPE + QK_ROPE].astype(BF16)
        return (r[nkd:] * jnp.concatenate(rinv, axis=0) + _dot_nt(qr, kbuf[lo:lo + nk, :])) * ATTN_SCALE

    def update(s, nk):
        m_new = jnp.maximum(m[...], jnp.max(s, axis=-1, keepdims=True))
        corr = jnp.exp(m[...] - m_new)
        p = jnp.exp(s - m_new)
        l[...] = l[...] * corr + jnp.sum(p, axis=-1, keepdims=True)
        acc[...] = acc[...] * corr + _dot(p.astype(BF16), cbuf[0:nk, :])
        m[...] = m_new

    def stage(j, c_page, kr_page):
        rows = slice(j * PAGE, (j + 1) * PAGE)
        ct[:, rows] = c_page.T.astype(BF16)
        cbuf[rows, :] = c_page.astype(BF16)
        kbuf[rows, :] = kr_page.astype(BF16)

    @pl.when(kc < n_chunks)
    def _():
        for j in range(npg):
            stage(j, lat[j][...], krp[j][...])
        nk = npg * PAGE
        kb = min(KEY_BLOCK, nk)
        update(jnp.concatenate([scores(lo, kb) for lo in range(0, nk, kb)], axis=1), nk)

    @pl.when(kc == n_chunks)
    def _():
        stage(0, cnew[...], krnew[...])
        key = lax.broadcasted_iota(I32, (nhq, PAGE), 1)
        qpos = lax.broadcasted_iota(I32, (nhq, PAGE), 0) % dq
        update(jnp.where(key <= qpos, scores(0, PAGE), NEG_INF), PAGE)
        o_ref[...] = (acc[...] / l[...]).astype(o_ref.dtype)


def _mla_sample_attn(q_s, c_new, kr_new, cache_latent, cache_krope, page_table, wuk, gkn, n_b, dq):
    n_pages = page_table.shape[1]
    npg = math.gcd(n_pages, MAX_PAGES_PER_STEP)
    n_chunks = n_pages // npg
    nhq = N_HEADS * dq
    nkd = N_HEADS * QK_NOPE
    wukt = wuk.transpose(0, 2, 1).reshape(nkd, KV_RANK)
    pad = lambda a: jnp.concatenate([a, jnp.zeros((n_b, PAGE - dq, a.shape[-1]), a.dtype)], axis=1)
    page = lambda j, w: pl.BlockSpec(
        (None, PAGE, w), lambda b, kc, pt: (pt[b, jnp.minimum(kc, n_chunks - 1) * npg + j], 0, 0))
    per_b = lambda r, w: pl.BlockSpec((None, r, w), lambda b, kc, pt: (b, 0, 0))
    grid_spec = pltpu.PrefetchScalarGridSpec(
        num_scalar_prefetch=1, grid=(n_b, n_chunks + 1),
        in_specs=([page(j, KV_RANK) for j in range(npg)] + [page(j, QK_ROPE) for j in range(npg)]
                  + [per_b(PAGE, KV_RANK), per_b(PAGE, QK_ROPE), per_b(nhq, 256),
                     pl.BlockSpec((N_HEADS, KV_RANK, QK_NOPE), lambda b, kc, pt: (0, 0, 0)),
                     pl.BlockSpec((nkd, KV_RANK), lambda b, kc, pt: (0, 0)),
                     pl.BlockSpec((1, QK_NOPE), lambda b, kc, pt: (0, 0))]),
        out_specs=per_b(nhq, KV_RANK),
        scratch_shapes=[pltpu.VMEM((nkd + nhq, KV_RANK), BF16), pltpu.VMEM((KV_RANK, npg * PAGE), BF16),
                        pltpu.VMEM((npg * PAGE, KV_RANK), BF16), pltpu.VMEM((npg * PAGE, QK_ROPE), BF16),
                        pltpu.VMEM((nhq, 1), F32), pltpu.VMEM((nhq, 1), F32), pltpu.VMEM((nhq, KV_RANK), F32)])
    return pl.pallas_call(
        functools.partial(_sample_attn_kernel, npg=npg, n_chunks=n_chunks, dq=dq), grid_spec=grid_spec,
        out_shape=jax.ShapeDtypeStruct((n_b, nhq, KV_RANK), BF16),
        compiler_params=_cparams(("parallel", "arbitrary")), name="mla_sample_attn",
    )(page_table, *([cache_latent] * npg), *([cache_krope] * npg), pad(c_new), pad(kr_new), q_s,
      wuk, wukt, gkn.reshape(1, QK_NOPE).astype(F32))


def _mla_out_kernel(op_ref, os_ref, x_ref, wuv_ref, wo_ref, out_ref, *, n0):
    first = pl.program_id(0) < n0
    v = jnp.concatenate([_dot(jnp.where(first, op_ref[h], os_ref[h]), wuv_ref[h]) for h in range(N_HEADS)], axis=1)
    out_ref[...] = x_ref[...] + _dot(v.astype(BF16), wo_ref[...])


def _mla_out(o_p, o_s, x, w_uv, w_o, tm=512):
    t = x.shape[0]
    n0 = o_p.shape[1] // tm
    return pl.pallas_call(
        functools.partial(_mla_out_kernel, n0=n0), grid=(t // tm,),
        in_specs=[pl.BlockSpec((N_HEADS, tm, KV_RANK), lambda i: (0, jnp.minimum(i, n0 - 1), 0)),
                  pl.BlockSpec((N_HEADS, tm, KV_RANK), lambda i: (0, jnp.maximum(i - n0, 0), 0)),
                  pl.BlockSpec((tm, D), lambda i: (i, 0)),
                  pl.BlockSpec((N_HEADS, KV_RANK, 128), lambda i: (0, 0, 0)), pl.BlockSpec((D, D), lambda i: (0, 0))],
        out_specs=pl.BlockSpec((tm, D), lambda i: (i, 0)),
        out_shape=jax.ShapeDtypeStruct((t, D), F32),
        compiler_params=_cparams(("parallel",)), name="mla_out",
    )(o_p, o_s, x, w_uv.transpose(1, 0, 2).astype(BF16), w_o.astype(BF16))


def _mla_layer(x, norm_g, cache_latent, cache_krope, page_table, params, dims):
    n_bp, seq_p, n_bs, seq_s = dims
    tp = n_bp * seq_p
    past_len = page_table.shape[1] * cache_latent.shape[1]
    w_uk, w_uv, kn_nope, w_o = params[4], params[5], params[8], params[10]
    c_p, c_s, kr_p, kr_s, cbf, qcat, kcat = _mla_proj(x, norm_g, params, dims, past_len)
    c_s = c_s.reshape(n_bs, seq_s, KV_RANK)
    kr_s = kr_s[:, :QK_ROPE].reshape(n_bs, seq_s, QK_ROPE)
    o_p = _mla_prompt_attn(qcat, kcat, cbf, n_bp, seq_p)
    q_s = qcat[:, tp:].astype(F32).reshape(N_HEADS, n_bs, seq_s, 256).transpose(1, 0, 2, 3)
    q_s = q_s.reshape(n_bs, N_HEADS * seq_s, 256)
    o_s = _mla_sample_attn(q_s, c_s, kr_s, cache_latent, cache_krope, page_table,
                           w_uk.transpose(1, 0, 2).astype(BF16), kn_nope, n_bs, seq_s)
    o_s = o_s.reshape(n_bs, N_HEADS, seq_s, KV_RANK).transpose(1, 0, 2, 3).reshape(N_HEADS, n_bs * seq_s, KV_RANK)
    x = _mla_out(o_p, o_s, x, w_uv, w_o)
    state = (c_p.reshape(n_bp, seq_p, KV_RANK), c_s, kr_p[:, :QK_ROPE].reshape(n_bp, seq_p, QK_ROPE), kr_s)
    return x, state


def _pool_kernel(x_ref, prev_ref, gn_ref, w_ref, sc_ref, o_ref, st_ref, ext, *, tl, start, last):
    ti = pl.program_id(1)

    @pl.when(ti == 0)
    def _():
        ext[0:POOL_HALO, :] = prev_ref[...]

    @pl.when(ti > 0)
    def _():
        ext[0:POOL_HALO, :] = ext[tl:tl + POOL_HALO, :]

    x = x_ref[...]
    h = _rms(x, gn_ref[...])
    ext[POOL_HALO:POOL_HALO + tl, :] = h
    pos = start + ti * tl + lax.broadcasted_iota(I32, (tl, 1), 0)
    run = ext[...]
    z = []
    for g, w in enumerate(POOL_WINDOWS):
        run = run[:, (POOL_GROUP if g else 0):]
        run = run + pltpu.roll(run, w // 2, 0)
        total = run[POOL_HALO:POOL_HALO + tl, :POOL_GROUP]
        count = jnp.minimum(w, pos + 1).astype(F32)
        p = total / count - h[:, g * POOL_GROUP:(g + 1) * POOL_GROUP]
        z.append(_dot(p.astype(BF16), w_ref[g]))
    o_ref[...] = x + jnp.concatenate(z, axis=1) * sc_ref[...]

    @pl.when(ti == last)
    def _():
        st_ref[...] = ext[tl:tl + POOL_HALO, :]


def _pool_stream(x, prev, norm_g, pool_w, pool_scale, n_b, seq, row0, start, tl):
    nt = seq // tl
    blk0 = row0 // tl
    prev = jnp.concatenate([jnp.zeros((n_b, 1, D), F32), prev.astype(F32)], axis=1)
    rows = lambda b, t: (blk0 + b * nt + t, 0)
    out, st = pl.pallas_call(
        functools.partial(_pool_kernel, tl=tl, start=start, last=nt - 1), grid=(n_b, nt),
        in_specs=[pl.BlockSpec((tl, D), rows),
                  pl.BlockSpec((None, POOL_HALO, D), lambda b, t: (b, 0, 0)),
                  pl.BlockSpec((1, D), lambda b, t: (0, 0)),
                  pl.BlockSpec((4, POOL_GROUP, POOL_GROUP), lambda b, t: (0, 0, 0)),
                  pl.BlockSpec((1, D), lambda b, t: (0, 0))],
        out_specs=[pl.BlockSpec((tl, D), rows),
                   pl.BlockSpec((None, POOL_HALO, D), lambda b, t: (b, 0, 0))],
        out_shape=[jax.ShapeDtypeStruct(x.shape, F32), jax.ShapeDtypeStruct((n_b, POOL_HALO, D), F32)],
        scratch_shapes=[pltpu.VMEM((POOL_HALO + tl, D), F32)],
        input_output_aliases={0: 0},
        compiler_params=_cparams(("parallel", "arbitrary")), name="pool_mixer",
    )(x, prev, norm_g.reshape(1, D), pool_w.astype(BF16), pool_scale.reshape(1, D))
    return out, st[:, 1:]


def _pool_layer(x, norm_g, state_pool, params, dims, past_len):
    n_bp, seq_p, n_bs, seq_s = dims
    tp = n_bp * seq_p
    pool_w, pool_scale = params
    zero = jnp.zeros((n_bp, POOL_HALO - 1, D), F32)
    x, st_p = _pool_stream(x, zero, norm_g, pool_w, pool_scale, n_bp, seq_p, 0, 0, min(512, seq_p))
    x, st_s = _pool_stream(x, state_pool, norm_g, pool_w, pool_scale, n_bs, seq_s, tp, past_len, seq_s)
    return x, (st_p, st_s)


def _router_kernel(x_ref, gn_ref, wr_ref, tri_ref, hb_ref, sel_ref, gate_ref, cnt_ref):
    h = _rms(x_ref[...], gn_ref[...])
    hb_ref[...] = h.astype(BF16)
    logits = lax.dot_general(wr_ref[...], h, (((1,), (1,)), ((), ())), precision=lax.Precision.HIGHEST,
                             preferred_element_type=F32)
    idx = lax.broadcasted_iota(I32, logits.shape, 0)
    m1 = jnp.max(logits, axis=0, keepdims=True)
    i1 = jnp.min(jnp.where(logits == m1, idx, N_EXPERTS), axis=0, keepdims=True)
    first = idx == i1
    rest = jnp.where(first, -jnp.inf, logits)
    m2 = jnp.max(rest, axis=0, keepdims=True)
    i2 = jnp.min(jnp.where(rest == m2, idx, N_EXPERTS), axis=0, keepdims=True)
    second = idx == i2
    e2 = jnp.exp(m2 - m1)
    g1 = 1.0 / (1.0 + e2)
    g2 = e2 / (1.0 + e2)
    chosen = jnp.where(first | second, 1.0, 0.0)
    rank = _dot(chosen.astype(BF16), tri_ref[...])
    rank1 = jnp.sum(jnp.where(first, rank, 0.0), axis=0, keepdims=True)
    rank2 = jnp.sum(jnp.where(second, rank, 0.0), axis=0, keepdims=True)
    zero = jnp.zeros((N_EXPERTS - 4,) + i1.shape[1:], I32)
    sel_ref[...] = jnp.concatenate([i1, i2, rank1.astype(I32), rank2.astype(I32), zero], axis=0)
    gate_ref[...] = jnp.concatenate([g1, g2, jnp.zeros((N_EXPERTS - 2,) + g1.shape[1:], F32)], axis=0)
    cnt_ref[...] = jnp.broadcast_to(jnp.sum(chosen, axis=1, keepdims=True), cnt_ref.shape).astype(I32)


def _moe_route(x, norm_g, w_router):
    t = x.shape[0]
    tb = MOE_TB
    nblk = t // tb
    tri = (jnp.arange(tb)[:, None] < jnp.arange(tb)[None, :]).astype(BF16)
    blk = lambda dt: (pl.BlockSpec((None, N_EXPERTS, tb), lambda i: (i, 0, 0)),
                      jax.ShapeDtypeStruct((nblk, N_EXPERTS, tb), dt))
    (sel_spec, sel_shape), (gate_spec, gate_shape) = blk(I32), blk(F32)
    return pl.pallas_call(
        _router_kernel, grid=(nblk,),
        in_specs=[pl.BlockSpec((tb, D), lambda i: (i, 0)), pl.BlockSpec((1, D), lambda i: (0, 0)),
                  pl.BlockSpec((N_EXPERTS, D), lambda i: (0, 0)), pl.BlockSpec((tb, tb), lambda i: (0, 0))],
        out_specs=[pl.BlockSpec((tb, D), lambda i: (i, 0)), sel_spec, gate_spec,
                   pl.BlockSpec((None, N_EXPERTS, 128), lambda i: (i, 0, 0))],
        out_shape=[jax.ShapeDtypeStruct((t, D), BF16), sel_shape, gate_shape,
                   jax.ShapeDtypeStruct((nblk, N_EXPERTS, 128), I32)],
        compiler_params=_cparams(("parallel",)), name="moe_router",
    )(x, norm_g.reshape(1, D), w_router.T.astype(F32), tri)


def _moe_rows(t):
    nblk = t // MOE_TB
    worst = 2 * t + nblk * N_EXPERTS * (MOE_ALIGN - 1) + N_EXPERTS * (MOE_TM - 1)
    return (worst + MOE_TM - 1) // MOE_TM * MOE_TM


def _moe_plan(sel, counts, t):
    cnt = counts[:, :, 0]
    cpad = (cnt + MOE_ALIGN - 1) // MOE_ALIGN * MOE_ALIGN
    per_e = jnp.sum(cpad, axis=0)
    per_e_pad = (per_e + MOE_TM - 1) // MOE_TM * MOE_TM
    e_start = jnp.cumsum(per_e_pad) - per_e_pad
    off = e_start[None, :] + jnp.cumsum(cpad, axis=0) - cpad
    lend = jnp.cumsum(cpad, axis=1)
    lstart = lend - cpad
    e1, e2, r1, r2 = sel[:, 0], sel[:, 1], sel[:, 2], sel[:, 3]
    experts = jnp.arange(N_EXPERTS)
    pick = lambda tab, e: jnp.sum(jnp.where(e[:, :, None] == experts, tab[:, None, :], 0), axis=2)
    dstl = jnp.stack([pick(lstart, e1) + r1, pick(lstart, e2) + r2], axis=1).astype(I32)
    prow = jnp.arange(MOE_PIECES)[None, :] * MOE_ALIGN
    pe = jnp.minimum(jnp.sum(prow[:, :, None] >= lend[:, None, :], axis=2), N_EXPERTS - 1)
    grow = pick(off, pe) + prow - pick(lstart, pe)
    n_pieces = lend[:, -1] // MOE_ALIGN
    grow = jnp.where(jnp.arange(MOE_PIECES)[None, :] < n_pieces[:, None], grow, 0)
    n_tiles = _moe_rows(t) // MOE_TM
    e_end = e_start + per_e_pad
    tile_e = jnp.sum((jnp.arange(n_tiles) * MOE_TM)[:, None] >= e_end[None, :], axis=1)
    active = (tile_e < N_EXPERTS).astype(I32)
    tile_e = jnp.where(active == 1, tile_e, N_EXPERTS - 1)
    return dict(dstl=dstl, grow=grow.astype(I32).reshape(-1), n_pieces=n_pieces.astype(I32),
                tile_e=tile_e.astype(I32), active=active)


def _dispatch_kernel(np_ref, grow_ref, hb_ref, dst_ref, gate_ref, xs_in, xs_ref, buf, sem):
    del xs_in
    b = pl.program_id(0)
    nblk = pl.num_programs(0)
    slot = b % 2

    def copy(blk, i, s):
        r = pl.multiple_of(grow_ref[blk * MOE_PIECES + i], MOE_ALIGN)
        src = buf.at[s, pl.ds(pl.multiple_of(i * MOE_ALIGN, MOE_ALIGN), MOE_ALIGN), :]
        return pltpu.make_async_copy(src, xs_ref.at[pl.ds(r, MOE_ALIGN), :], sem.at[s])

    def drain(blk, s):
        lax.fori_loop(0, np_ref[blk], lambda i, c: (copy(blk, i, s).wait(), c)[1], 0)

    @pl.when(b >= 2)
    def _():
        drain(b - 2, slot)

    want = lax.broadcasted_iota(I32, (MOE_KL, 1), 0)
    hit1 = dst_ref[0:1, :] == want
    hit2 = dst_ref[1:2, :] == want
    onehot = jnp.where(hit1 | hit2, 1.0, 0.0).astype(BF16)
    buf[slot, :, 0:D] = _dot(onehot, hb_ref[...]).astype(BF16)
    gate = jnp.sum(jnp.where(hit1, gate_ref[0:1, :], 0.0) + jnp.where(hit2, gate_ref[1:2, :], 0.0),
                   axis=1, keepdims=True)
    g_hi = gate.astype(BF16).astype(F32)
    g_mid = (gate - g_hi).astype(BF16).astype(F32)
    g_lo = gate - g_hi - g_mid
    lane = lax.broadcasted_iota(I32, (MOE_KL, 128), 1)
    slab = jnp.where(lane == 0, g_hi, jnp.where(lane == 1, g_mid, jnp.where(lane == 2, g_lo, 0.0)))
    buf[slot, :, D:MOE_XW] = slab.astype(BF16)
    lax.fori_loop(0, np_ref[b], lambda i, c: (copy(b, i, slot).start(), c)[1], 0)

    @pl.when(b == nblk - 1)
    def _():
        @pl.when(b >= 1)
        def _():
            drain(b - 1, 1 - slot)
        drain(b, slot)


def _moe_dispatch(hb, gate, plan):
    t = hb.shape[0]
    nblk = t // MOE_TB
    rows = _moe_rows(t)
    grid_spec = pltpu.PrefetchScalarGridSpec(
        num_scalar_prefetch=2, grid=(nblk,),
        in_specs=[pl.BlockSpec((MOE_TB, D), lambda b, *_: (b, 0)),
                  pl.BlockSpec((None, 2, MOE_TB), lambda b, *_: (b, 0, 0)),
                  pl.BlockSpec((None, N_EXPERTS, MOE_TB), lambda b, *_: (b, 0, 0)),
                  pl.BlockSpec(memory_space=pl.ANY)],
        out_specs=pl.BlockSpec(memory_space=pl.ANY),
        scratch_shapes=[pltpu.VMEM((2, MOE_KL, MOE_XW), BF16), pltpu.SemaphoreType.DMA((2,))])
    return pl.pallas_call(
        _dispatch_kernel, grid_spec=grid_spec,
        out_shape=jax.ShapeDtypeStruct((rows, MOE_XW), BF16),
        input_output_aliases={5: 0},
        compiler_params=_cparams(("arbitrary",)), name="moe_dispatch",
    )(plan["n_pieces"], plan["grow"], hb, plan["dstl"], gate, jnp.zeros((rows, MOE_XW), BF16))


def _expert_kernel(te_ref, act_ref, x_ref, wg_ref, wu_ref, wo_ref, o_ref, acc, *, nf):
    i, f = pl.program_id(0), pl.program_id(1)

    @pl.when(f == 0)
    def _():
        acc[...] = jnp.zeros_like(acc)

    @pl.when(act_ref[i] == 1)
    def _():
        x = x_ref[:, 0:D]
        t = (jax.nn.silu(_dot(x, wg_ref[...])) * _dot(x, wu_ref[...])).astype(BF16)
        acc[...] += _dot(t, wo_ref[...])

    @pl.when(f == nf - 1)
    def _():
        gate = jnp.sum(x_ref[:, D:MOE_XW].astype(F32), axis=1, keepdims=True)
        o_ref[...] = (acc[...] * gate).astype(o_ref.dtype)


def _moe_experts(xs, w_in, w_out, plan, tf=512):
    rows = xs.shape[0]
    d_ff = w_out.shape[1]
    nf = d_ff // tf
    fe = lambda i, f, act: jnp.where(act[i] == 1, f, nf - 1)
    grid_spec = pltpu.PrefetchScalarGridSpec(
        num_scalar_prefetch=2, grid=(rows // MOE_TM, nf),
        in_specs=[pl.BlockSpec((MOE_TM, MOE_XW), lambda i, f, te, act: (i, 0)),
                  pl.BlockSpec((None, D, tf), lambda i, f, te, act: (te[i], 0, fe(i, f, act))),
                  pl.BlockSpec((None, D, tf), lambda i, f, te, act: (te[i], 0, fe(i, f, act) + nf)),
                  pl.BlockSpec((None, tf, D), lambda i, f, te, act: (te[i], fe(i, f, act), 0))],
        out_specs=pl.BlockSpec((MOE_TM, D), lambda i, f, te, act: (i, 0)),
        scratch_shapes=[pltpu.VMEM((MOE_TM, D), F32)])
    return pl.pallas_call(
        functools.partial(_expert_kernel, nf=nf), grid_spec=grid_spec,
        out_shape=jax.ShapeDtypeStruct((rows, D), BF16),
        compiler_params=_cparams(("parallel", "arbitrary")), name="moe_experts",
    )(plan["tile_e"], plan["active"], xs, w_in, w_in, w_out)


def _combine_kernel(np_ref, grow_ref, x_ref, dst_ref, ys_ref, *rest, n0):
    outs, (buf, sem) = rest[:-2], rest[-2:]
    b = pl.program_id(0)
    nblk = pl.num_programs(0)
    slot = b % 2

    def copy(blk, i, s):
        r = pl.multiple_of(grow_ref[blk * MOE_PIECES + i], MOE_ALIGN)
        dst = buf.at[s, pl.ds(pl.multiple_of(i * MOE_ALIGN, MOE_ALIGN), MOE_ALIGN), :]
        return pltpu.make_async_copy(ys_ref.at[pl.ds(r, MOE_ALIGN), :], dst, sem.at[s])

    def fetch(blk, s):
        lax.fori_loop(0, np_ref[blk], lambda i, c: (copy(blk, i, s).start(), c)[1], 0)

    @pl.when(b == 0)
    def _():
        buf[...] = jnp.zeros_like(buf)
        fetch(0, 0)

    @pl.when(b + 1 < nblk)
    def _():
        fetch(b + 1, 1 - slot)

    want = lax.broadcasted_iota(I32, (1, MOE_KL), 1)
    onehot = jnp.where((dst_ref[:, 0:1] == want) | (dst_ref[:, 1:2] == want), 1.0, 0.0).astype(BF16)
    lax.fori_loop(0, np_ref[b], lambda i, c: (copy(b, i, slot).wait(), c)[1], 0)
    y = x_ref[...] + _dot(onehot, buf[slot])
    if len(outs) == 1:
        outs[0][...] = y
    else:
        @pl.when(b < n0)
        def _():
            outs[0][...] = y

        @pl.when(b >= n0)
        def _():
            outs[1][...] = y


def _moe_combine(x, ys, dst_tok, plan, split_rows):
    t = x.shape[0]
    nblk = t // MOE_TB
    tile = lambda m: pl.BlockSpec((MOE_TB, D), m)
    if split_rows is None:
        n0 = nblk
        out_specs = tile(lambda b, *_: (b, 0))
        out_shape = jax.ShapeDtypeStruct((t, D), F32)
    else:
        n0 = split_rows // MOE_TB
        out_specs = [tile(lambda b, *_: (jnp.minimum(b, n0 - 1), 0)), tile(lambda b, *_: (jnp.maximum(b - n0, 0), 0))]
        out_shape = [jax.ShapeDtypeStruct((split_rows, D), F32), jax.ShapeDtypeStruct((t - split_rows, D), F32)]
    grid_spec = pltpu.PrefetchScalarGridSpec(
        num_scalar_prefetch=2, grid=(nblk,),
        in_specs=[tile(lambda b, *_: (b, 0)),
                  pl.BlockSpec((MOE_TB, 2), lambda b, *_: (b, 0)),
                  pl.BlockSpec(memory_space=pl.ANY)],
        out_specs=out_specs,
        scratch_shapes=[pltpu.VMEM((2, MOE_KL, D), BF16), pltpu.SemaphoreType.DMA((2,))])
    return pl.pallas_call(
        functools.partial(_combine_kernel, n0=n0), grid_spec=grid_spec, out_shape=out_shape,
        compiler_params=_cparams(("arbitrary",)), name="moe_combine",
    )(plan["n_pieces"], plan["grow"], x, dst_tok, ys)


def _moe_layer(x, norm_g, w_router, w_in, w_out, split_rows=None):
    t = x.shape[0]
    hb, sel, gate, counts = _moe_route(x, norm_g, w_router)
    plan = _moe_plan(sel, counts, t)
    xs = _moe_dispatch(hb, gate, plan)
    ys = _moe_experts(xs, w_in.astype(BF16), w_out.astype(BF16), plan)
    dst_tok = plan["dstl"].transpose(0, 2, 1).reshape(t, 2)
    return _moe_combine(x, ys, dst_tok, plan, split_rows)


def kernel(x_prompt, x_sample, state_l0_s5_re, state_l0_s5_im, cache_l1_latent, cache_l1_krope, state_l2_pool, state_l3_s5_re, state_l3_s5_im, page_table, l0_norm_mix, l0_s5_lambda_re, l0_s5_lambda_im, l0_s5_log_dt, l0_s5_b_re, l0_s5_b_im, l0_s5_c_re, l0_s5_c_im, l0_s5_d, l0_s5_w_glu, l0_norm_ffn, l0_ffn_w_in, l0_ffn_w_out, l1_norm_mix, l1_mla_w_in, l1_mla_norm_q, l1_mla_norm_kv, l1_mla_w_q, l1_mla_w_uk, l1_mla_w_uv, l1_mla_qn_nope, l1_mla_qn_rope, l1_mla_kn_nope, l1_mla_kn_rope, l1_mla_w_o, l1_norm_ffn, l1_moe_w_router, l1_moe_w_in, l1_moe_w_out, l2_norm_mix, l2_pool_w, l2_pool_scale, l2_norm_ffn, l2_ffn_w_in, l2_ffn_w_out, l3_norm_mix, l3_s5_lambda_re, l3_s5_lambda_im, l3_s5_log_dt, l3_s5_b_re, l3_s5_b_im, l3_s5_c_re, l3_s5_c_im, l3_s5_d, l3_s5_w_glu, l3_norm_ffn, l3_moe_w_router, l3_moe_w_in, l3_moe_w_out):
    n_bp, seq_p, _ = x_prompt.shape
    n_bs, seq_s, _ = x_sample.shape
    dims = (n_bp, seq_p, n_bs, seq_s)
    tp, ts = n_bp * seq_p, n_bs * seq_s
    past_len = page_table.shape[1] * cache_l1_latent.shape[1]

    x, st0 = _s5_layer((x_prompt.reshape(tp, D), x_sample.reshape(ts, D)), l0_norm_mix, state_l0_s5_re, state_l0_s5_im,
                       (l0_s5_lambda_re, l0_s5_lambda_im, l0_s5_log_dt, l0_s5_b_re, l0_s5_b_im,
                        l0_s5_c_re, l0_s5_c_im, l0_s5_d, l0_s5_w_glu), dims)
    x = _ffn_dense(x, l0_norm_ffn, l0_ffn_w_in, l0_ffn_w_out)

    x, st1 = _mla_layer(x, l1_norm_mix, cache_l1_latent, cache_l1_krope, page_table,
                        (l1_mla_w_in, l1_mla_norm_q, l1_mla_norm_kv, l1_mla_w_q, l1_mla_w_uk, l1_mla_w_uv,
                         l1_mla_qn_nope, l1_mla_qn_rope, l1_mla_kn_nope, l1_mla_kn_rope, l1_mla_w_o), dims)
    x = _moe_layer(x, l1_norm_ffn, l1_moe_w_router, l1_moe_w_in, l1_moe_w_out)

    x, st2 = _pool_layer(x, l2_norm_mix, state_l2_pool, (l2_pool_w, l2_pool_scale), dims, past_len)
    x = _ffn_dense(x, l2_norm_ffn, l2_ffn_w_in, l2_ffn_w_out)

    x, st3 = _s5_layer((x,), l3_norm_mix, state_l3_s5_re, state_l3_s5_im,
                       (l3_s5_lambda_re, l3_s5_lambda_im, l3_s5_log_dt, l3_s5_b_re, l3_s5_b_im,
                        l3_s5_c_re, l3_s5_c_im, l3_s5_d, l3_s5_w_glu), dims)
    y_p, y_s = _moe_layer(x, l3_norm_ffn, l3_moe_w_router, l3_moe_w_in, l3_moe_w_out, split_rows=tp)

    return (y_p.reshape(n_bp, seq_p, D), y_s.reshape(n_bs, seq_s, D)) + st0 + st1 + st2 + st3
```

```python
import functools
import math

import jax
import jax.numpy as jnp
from jax import lax
from jax.experimental import pallas as pl
from jax.experimental.pallas import tpu as pltpu

F32 = jnp.float32
BF16 = jnp.bfloat16
I32 = jnp.int32

D = 1024
NORM_EPS = 1e-6
NEG_INF = -1e30
S5_GROUP = 16
S5_STATE = 64
N_GROUPS = D // S5_GROUP
S5_T = 8
S5_BUNDLE = 8
N_BUNDLES = N_GROUPS // S5_BUNDLE
S5_W = S5_T * 128
N_HEADS = 8
QK_NOPE = 128
QK_ROPE = 64
Q_RANK = 512
KV_RANK = 256
ROPE_THETA = 10000.0
ATTN_SCALE = (QK_NOPE + QK_ROPE) ** -0.5
LOG2E = math.log2(math.e)
PAGE = 128
MAX_PAGES_PER_STEP = 16
KEY_BLOCK = 2048
POOL_WINDOWS = (2, 4, 8, 16)
POOL_HALO = 16
POOL_GROUP = D // 4
N_EXPERTS = 8
MOE_TB = 1024
MOE_TM = 1024
MOE_ALIGN = 16
MOE_KL = (2 * MOE_TB + N_EXPERTS * (MOE_ALIGN - 1) + 127) // 128 * 128
MOE_PIECES = MOE_KL // MOE_ALIGN
MOE_XW = D + 128


def _cparams(sem, vmem_mb=48):
    return pltpu.CompilerParams(dimension_semantics=sem, vmem_limit_bytes=vmem_mb * 1024 * 1024)


def _rms(x, g):
    r = lax.rsqrt(jnp.mean(x * x, axis=-1, keepdims=True) + NORM_EPS)
    return x * r * g


def _dot(a, b):
    return jnp.dot(a, b, preferred_element_type=F32)


def _dot_nt(a, b):
    return lax.dot_general(a, b, (((1,), (1,)), ((), ())), preferred_element_type=F32)


def _lo_map(n):
    return lambda i: (jnp.minimum(i, n - 1), 0)


def _hi_map(n):
    return lambda i: (jnp.maximum(i - n, 0), 0)


def _rmsnorm_kernel(x_ref, g_ref, o_ref):
    o_ref[...] = _rms(x_ref[...], g_ref[...]).astype(o_ref.dtype)


def _rmsnorm(x, g, row0, rows, tm=1024):
    tm = min(tm, rows)
    blk0 = row0 // tm
    return pl.pallas_call(
        _rmsnorm_kernel, grid=(rows // tm,),
        in_specs=[pl.BlockSpec((tm, D), lambda i: (blk0 + i, 0)), pl.BlockSpec((1, D), lambda i: (0, 0))],
        out_specs=pl.BlockSpec((tm, D), lambda i: (i, 0)),
        out_shape=jax.ShapeDtypeStruct((rows, D), F32),
        compiler_params=_cparams(("parallel",)), name="rmsnorm",
    )(x, g.reshape(1, D))


def _s5_weights(lam_re, lam_im, log_dt, b_re, b_im, c_re, c_im, d):
    lr, li = lam_re.astype(F32), lam_im.astype(F32)
    dt = jnp.exp(log_dt.astype(F32))[:, None]
    mag = jnp.exp(lr * dt)
    ar, ai = mag * jnp.cos(li * dt), mag * jnp.sin(li * dt)
    nr, ni = ar - 1.0, ai
    den = lr * lr + li * li
    f_re = ((nr * lr + ni * li) / den)[..., None]
    f_im = ((ni * lr - nr * li) / den)[..., None]
    br, bi = b_re.astype(F32), b_im.astype(F32)
    bb_re, bb_im = f_re * br - f_im * bi, f_re * bi + f_im * br
    cr, ci = c_re.astype(F32), c_im.astype(F32)

    pr, pi = [jnp.ones_like(ar)], [jnp.zeros_like(ar)]
    for _ in range(S5_T):
        pr.append(pr[-1] * ar - pi[-1] * ai)
        pi.append(pr[-2] * ai + pi[-1] * ar)
    pr, pi = jnp.stack(pr), jnp.stack(pi)

    ab_re = pr[:S5_T, :, :, None] * bb_re - pi[:S5_T, :, :, None] * bb_im
    ab_im = pr[:S5_T, :, :, None] * bb_im + pi[:S5_T, :, :, None] * bb_re
    hi = lax.Precision.HIGHEST
    kern = (jnp.einsum("gcp,jgpd->jgcd", cr, ab_re, precision=hi)
            - jnp.einsum("gcp,jgpd->jgcd", ci, ab_im, precision=hi))

    nb, bg, t = N_BUNDLES, S5_BUNDLE, S5_T
    tt = jnp.arange(t)
    lag = tt[None, :] - tt[:, None]
    ksel = jnp.where((lag >= 0)[:, :, None, None, None], kern[jnp.clip(lag, 0, t - 1)], 0.0)
    ksel = ksel.reshape(t, t, nb, bg, S5_GROUP, S5_GROUP).transpose(2, 0, 3, 5, 1, 4)
    m = ksel.astype(BF16).reshape(nb, t * 128, 128)

    inj = jnp.stack([ab_re[::-1], ab_im[::-1]])
    inj = inj.reshape(2, t, nb, bg, S5_STATE, S5_GROUP).transpose(2, 1, 3, 5, 0, 4)
    we = inj.astype(BF16).reshape(nb, t * 128, 128)

    p1r, p1i = pr[1:, :, None, :], pi[1:, :, None, :]
    co_re = cr[None] * p1r - ci[None] * p1i
    co_im = -(cr[None] * p1i + ci[None] * p1r)
    co = jnp.stack([co_re, co_im]).reshape(2, t, nb, bg, S5_GROUP, S5_STATE).transpose(2, 0, 3, 5, 1, 4)
    ws = co.astype(BF16).reshape(nb, 2 * bg * S5_STATE, 128)

    a_t = jnp.stack([pr[t], pi[t]]).reshape(2, nb, bg * S5_STATE)
    a_t = a_t.transpose(1, 0, 2).reshape(nb, 1, 2 * bg * S5_STATE)
    dv = jnp.broadcast_to(d.astype(F32).reshape(1, nb, 128), (t, nb, 128)).transpose(1, 0, 2).reshape(nb, 1, t * 128)
    return m, we, ws, a_t, dv


def _group_spread(n_outer, inner):
    r = jnp.arange(n_outer * inner)[:, None]
    c = jnp.arange(n_outer * S5_BUNDLE * inner)[None, :]
    return ((r // inner == c // (S5_BUNDLE * inner)) & (r % inner == c % inner)).astype(BF16)


def _widen(compact, spread, row_shift, col_shift):
    wide = _dot(compact, spread)
    row_g = (lax.broadcasted_iota(I32, wide.shape, 0) >> row_shift) & (S5_BUNDLE - 1)
    col_h = (lax.broadcasted_iota(I32, wide.shape, 1) >> col_shift) & (S5_BUNDLE - 1)
    return jnp.where(row_g == col_h, wide, 0.0).astype(BF16)


def _s5_kernel(u_ref, s0_ref, mc_ref, wec_ref, wsc_ref, sp_tc, sp_rp, a_ref, d_ref, g_ref, sfin_ref,
               m_ref, we_ref, ws_ref, state, ubuf, sprev, ebuf, *, nb, cb, last):
    i = pl.program_id(1)
    half = S5_BUNDLE * S5_STATE

    @pl.when(i == 0)
    def _():
        state[...] = s0_ref[...]
        m_ref[...] = _widen(mc_ref[...], sp_tc[...], 4, 4)
        we_ref[...] = _widen(wec_ref[...], sp_rp[...], 4, 6)
        ws_ref[...] = _widen(wsc_ref[...], sp_tc[...], 6, 4)

    for c in range(cb):
        for t in range(S5_T):
            ubuf[c * nb:(c + 1) * nb, t * 128:(t + 1) * 128] = u_ref[:, c * S5_T + t, :]
    u = ubuf[...]
    ub = u.astype(BF16)
    ebuf[...] = _dot(ub, we_ref[...])
    ar, ai = a_ref[:, :half], a_ref[:, half:]
    s = state[...]
    for c in range(cb):
        rows = slice(c * nb, (c + 1) * nb)
        sprev[rows, :] = s
        sr, si = s[:, :half], s[:, half:]
        s = jnp.concatenate([ar * sr - ai * si + ebuf[rows, :half],
                             ar * si + ai * sr + ebuf[rows, half:]], axis=1)
    state[...] = s
    y = _dot(ub, m_ref[...]) + _dot(sprev[...].astype(BF16), ws_ref[...])
    ebuf[...] = jax.nn.gelu(y + d_ref[...] * u)
    for c in range(cb):
        for t in range(S5_T):
            g_ref[:, c * S5_T + t, :] = ebuf[c * nb:(c + 1) * nb, t * 128:(t + 1) * 128]

    @pl.when(i == last)
    def _():
        sfin_ref[...] = s


def _s5_scan(h, s0, tables, cb):
    m, we, ws, a_t, dv = tables
    nb, seq, _ = h.shape
    tok = S5_T * cb
    ni = seq // tok
    r = nb * cb
    w = S5_W
    kern = functools.partial(_s5_kernel, nb=nb, cb=cb, last=ni - 1)
    compact = lambda: pl.BlockSpec((None, w, 128), lambda g, i: (g, 0, 0))
    spread = lambda: pl.BlockSpec((128, w), lambda g, i: (0, 0))
    vec = lambda: pl.BlockSpec((None, 1, w), lambda g, i: (g, 0, 0))
    return pl.pallas_call(
        kern, grid=(N_BUNDLES, ni),
        in_specs=[pl.BlockSpec((nb, tok, 128), lambda g, i: (0, i, g)),
                  pl.BlockSpec((None, nb, w), lambda g, i: (g, 0, 0)),
                  compact(), compact(), compact(), spread(), spread(), vec(), vec()],
        out_specs=[pl.BlockSpec((nb, tok, 128), lambda g, i: (0, i, g)),
                   pl.BlockSpec((None, nb, w), lambda g, i: (g, 0, 0))],
        out_shape=[jax.ShapeDtypeStruct((nb, seq, D), F32),
                   jax.ShapeDtypeStruct((N_BUNDLES, nb, w), F32)],
        scratch_shapes=[pltpu.VMEM((w, w), BF16), pltpu.VMEM((w, w), BF16), pltpu.VMEM((w, w), BF16),
                        pltpu.VMEM((nb, w), F32), pltpu.VMEM((r, w), F32), pltpu.VMEM((r, w), F32),
                        pltpu.VMEM((r, w), F32)],
        compiler_params=_cparams(("parallel", "arbitrary")), name="s5_scan",
    )(h, s0, m, we, ws, _group_spread(S5_T, S5_GROUP), _group_spread(2, S5_STATE), a_t, dv)


def _state_to_bundles(s_re, s_im):
    n_b = s_re.shape[0]
    s = jnp.stack([s_re, s_im], axis=1).reshape(n_b, 2, N_BUNDLES, S5_BUNDLE * S5_STATE)
    return s.transpose(2, 0, 1, 3).reshape(N_BUNDLES, n_b, 2 * S5_BUNDLE * S5_STATE)


def _state_from_bundles(s):
    n_b = s.shape[1]
    s = s.reshape(N_BUNDLES, n_b, 2, S5_BUNDLE, S5_STATE).transpose(2, 1, 0, 3, 4)
    s = s.reshape(2, n_b, N_GROUPS, S5_STATE)
    return s[0], s[1]


def _glu_kernel(*refs, n0, two_x):
    if two_x:
        gp_ref, gs_ref, xp_ref, xs_ref, w_ref, o_ref = refs
    else:
        gp_ref, gs_ref, xp_ref, w_ref, o_ref = refs
    first = pl.program_id(0) < n0
    g = jnp.where(first, gp_ref[...], gs_ref[...])
    x = jnp.where(first, xp_ref[...], xs_ref[...]) if two_x else xp_ref[...]
    z = _dot(g.astype(BF16), w_ref[...])
    o_ref[...] = x + z[:, :D] * jax.nn.sigmoid(z[:, D:])


def _glu_residual(g_p, g_s, x_parts, w_glu, tm=512):
    t = g_p.shape[0] + g_s.shape[0]
    n0 = g_p.shape[0] // tm
    two_x = len(x_parts) == 2
    tile = lambda m: pl.BlockSpec((tm, D), m)
    x_specs = [tile(_lo_map(n0)), tile(_hi_map(n0))] if two_x else [tile(lambda i: (i, 0))]
    return pl.pallas_call(
        functools.partial(_glu_kernel, n0=n0, two_x=two_x), grid=(t // tm,),
        in_specs=[tile(_lo_map(n0)), tile(_hi_map(n0))] + x_specs + [pl.BlockSpec((D, 2 * D), lambda i: (0, 0))],
        out_specs=tile(lambda i: (i, 0)),
        out_shape=jax.ShapeDtypeStruct((t, D), F32),
        compiler_params=_cparams(("parallel",)), name="glu_residual",
    )(g_p, g_s, *x_parts, w_glu)


def _s5_layer(x_parts, norm_g, state_re, state_im, params, dims):
    n_bp, seq_p, n_bs, seq_s = dims
    tp, ts = n_bp * seq_p, n_bs * seq_s
    lam_re, lam_im, log_dt, b_re, b_im, c_re, c_im, d, w_glu = params
    tables = _s5_weights(lam_re, lam_im, log_dt, b_re, b_im, c_re, c_im, d)
    if len(x_parts) == 2:
        h_p = _rmsnorm(x_parts[0], norm_g, 0, tp)
        h_s = _rmsnorm(x_parts[1], norm_g, 0, ts)
    else:
        h_p = _rmsnorm(x_parts[0], norm_g, 0, tp)
        h_s = _rmsnorm(x_parts[0], norm_g, tp, ts)
    zeros = jnp.zeros((N_BUNDLES, n_bp, 2 * S5_BUNDLE * S5_STATE), F32)
    cb = math.gcd(seq_p // S5_T, max(1, 512 // n_bp))
    g_p, s_p = _s5_scan(h_p.reshape(n_bp, seq_p, D), zeros, tables, cb)
    g_s, s_s = _s5_scan(h_s.reshape(n_bs, seq_s, D), _state_to_bundles(state_re, state_im), tables, seq_s // S5_T)
    x = _glu_residual(g_p.reshape(tp, D), g_s.reshape(ts, D), x_parts, w_glu.astype(BF16))
    p_re, p_im = _state_from_bundles(s_p)
    s_re, s_im = _state_from_bundles(s_s)
    return x, (p_re, s_re, p_im, s_im)


def _ffn_kernel(x_ref, gn_ref, wg_ref, wu_ref, wo_ref, o_ref, h_scr, acc, *, nf):
    f = pl.program_id(1)

    @pl.when(f == 0)
    def _():
        h_scr[...] = _rms(x_ref[...], gn_ref[...]).astype(BF16)
        acc[...] = jnp.zeros_like(acc)

    h = h_scr[...]
    t = (jax.nn.silu(_dot(h, wg_ref[...])) * _dot(h, wu_ref[...])).astype(BF16)
    acc[...] += _dot(t, wo_ref[...])

    @pl.when(f == nf - 1)
    def _():
        o_ref[...] = x_ref[...] + acc[...]


def _ffn_dense(x, norm_g, w_in, w_out, tm=512, nf=2):
    t = x.shape[0]
    d_ff = w_out.shape[0]
    tf = d_ff // nf
    w_in = w_in.astype(BF16)
    return pl.pallas_call(
        functools.partial(_ffn_kernel, nf=nf), grid=(t // tm, nf),
        in_specs=[pl.BlockSpec((tm, D), lambda i, f: (i, 0)), pl.BlockSpec((1, D), lambda i, f: (0, 0)),
                  pl.BlockSpec((D, tf), lambda i, f: (0, f)), pl.BlockSpec((D, tf), lambda i, f: (0, f + nf)),
                  pl.BlockSpec((tf, D), lambda i, f: (f, 0))],
        out_specs=pl.BlockSpec((tm, D), lambda i, f: (i, 0)),
        out_shape=jax.ShapeDtypeStruct((t, D), F32),
        scratch_shapes=[pltpu.VMEM((tm, D), BF16), pltpu.VMEM((tm, D), F32)],
        compiler_params=_cparams(("parallel", "arbitrary")), name="ffn_dense",
    )(x, norm_g.reshape(1, D), w_in, w_in, w_out.astype(BF16))


def _swap_halves(w):
    half = w.shape[-1] // 2
    return jnp.concatenate([w[..., half:], w[..., :half]], axis=-1)


def _pad_lanes(w, n=128):
    return jnp.concatenate([w, jnp.zeros(w.shape[:-1] + (n - w.shape[-1],), w.dtype)], axis=-1)


def _rope_tables(seq_p, seq_s, past_len, rows_s):
    half = QK_ROPE // 2
    inv_freq = ROPE_THETA ** (-jnp.arange(half, dtype=F32) / half)
    pos = jnp.concatenate([jnp.arange(seq_p), past_len + (jnp.arange(rows_s) % seq_s)])
    ang = pos.astype(F32)[:, None] * inv_freq[None, :]
    cos, sin = jnp.cos(ang), jnp.sin(ang)
    return _pad_lanes(jnp.concatenate([cos, cos], axis=1)), _pad_lanes(jnp.concatenate([-sin, sin], axis=1))


def _rope_norm(x, x_sw, g, g_sw, cos, sin):
    r = lax.rsqrt(jnp.sum(x * x, axis=-1, keepdims=True) * (1.0 / QK_ROPE) + NORM_EPS)
    return r * (x * g * cos + x_sw * g_sw * sin)


def _mla_proj_kernel(x_ref, gmix, win, nq, nkv, cos, sin, gkr, gkrs, wq, gqn, gqr, gqrs, wuk, gkn,
                     cp_out, cs_out, krp_out, krs_out, cbt_out, q_out, k_out, *, n0):
    is_prompt = pl.program_id(0) < n0
    hn = _rms(x_ref[...], gmix[...]).astype(BF16)
    a = _dot(hn, win[...])
    hq = _rms(a[:, :Q_RANK], nq[...]).astype(BF16)
    c = _rms(a[:, Q_RANK:Q_RANK + KV_RANK], nkv[...])
    cb = c.astype(BF16)
    kr = _rope_norm(a[:, 768:896], a[:, 896:1024], gkr[...], gkrs[...], cos[...], sin[...])
    krb = kr.astype(BF16)

    @pl.when(is_prompt)
    def _():
        cp_out[...] = c
        krp_out[...] = kr
        cbt_out[...] = c.T.astype(BF16)

    @pl.when(jnp.logical_not(is_prompt))
    def _():
        cs_out[...] = c
        krs_out[...] = kr

    q_scale = jnp.where(is_prompt, ATTN_SCALE * LOG2E, 1.0)
    for h in range(N_HEADS):
        q3 = _dot(hq, wq[:, h * 384:(h + 1) * 384])
        qn = _rms(q3[:, :128], gqn[...])
        qr = _rope_norm(q3[:, 128:256], q3[:, 256:384], gqr[...], gqrs[...], cos[...], sin[...])
        q_out[h] = (jnp.concatenate([qn, qr], axis=1) * q_scale).astype(BF16)

    @pl.when(is_prompt)
    def _():
        for h in range(N_HEADS):
            kn = _rms(_dot(cb, wuk[:, h * QK_NOPE:(h + 1) * QK_NOPE]), gkn[...])
            k_out[h] = jnp.concatenate([kn.astype(BF16), krb], axis=1)


def _mla_proj(x, norm_g, params, dims, past_len, tm=512):
    n_bp, seq_p, n_bs, seq_s = dims
    t = x.shape[0]
    tp = n_bp * seq_p
    ts = t - tp
    w_in, norm_q, norm_kv, w_q, w_uk, w_uv, qn_nope, qn_rope, kn_nope, kn_rope, w_o = params
    w_kr = w_in[:, Q_RANK + KV_RANK:]
    win = jnp.concatenate([w_in[:, :Q_RANK + KV_RANK], _pad_lanes(w_kr), _pad_lanes(_swap_halves(w_kr))],
                          axis=1).astype(BF16)
    wq3 = w_q.reshape(Q_RANK, N_HEADS, QK_NOPE + QK_ROPE)
    wq_r = wq3[:, :, QK_NOPE:]
    wq = jnp.concatenate([wq3[:, :, :QK_NOPE], _pad_lanes(wq_r), _pad_lanes(_swap_halves(wq_r))], axis=2)
    wq = wq.reshape(Q_RANK, N_HEADS * 384).astype(BF16)
    wuk = w_uk.reshape(KV_RANK, N_HEADS * QK_NOPE).astype(BF16)
    cos, sin = _rope_tables(seq_p, seq_s, past_len, min(tm, ts))
    n_pt = seq_p // tm
    n0 = tp // tm
    tab = lambda i: (jnp.where(i < n0, i % n_pt, n_pt), 0)
    row = lambda v: v.reshape(1, -1).astype(F32)
    vec = lambda n: pl.BlockSpec((1, n), lambda i: (0, 0))
    full = lambda a: pl.BlockSpec(a.shape, lambda i: (0, 0))
    lo = lambda n: pl.BlockSpec((tm, n), _lo_map(n0))
    hi = lambda n: pl.BlockSpec((tm, n), _hi_map(n0))
    sds = jax.ShapeDtypeStruct
    return pl.pallas_call(
        functools.partial(_mla_proj_kernel, n0=n0), grid=(t // tm,),
        in_specs=[pl.BlockSpec((tm, D), lambda i: (i, 0)), vec(D), full(win), vec(Q_RANK), vec(KV_RANK),
                  pl.BlockSpec((tm, 128), tab), pl.BlockSpec((tm, 128), tab), vec(128), vec(128),
                  full(wq), vec(128), vec(128), vec(128), full(wuk), vec(128)],
        out_specs=[lo(KV_RANK), hi(KV_RANK), lo(128), hi(128),
                   pl.BlockSpec((KV_RANK, tm), lambda i: (0, jnp.minimum(i, n0 - 1))),
                   pl.BlockSpec((N_HEADS, tm, 256), lambda i: (0, i, 0)),
                   pl.BlockSpec((N_HEADS, tm, 256), lambda i: (0, jnp.minimum(i, n0 - 1), 0))],
        out_shape=[sds((tp, KV_RANK), F32), sds((ts, KV_RANK), F32), sds((tp, 128), F32), sds((ts, 128), F32),
                   sds((KV_RANK, tp), BF16), sds((N_HEADS, t, 256), BF16), sds((N_HEADS, tp, 256), BF16)],
        compiler_params=_cparams(("arbitrary",)), name="mla_proj",
    )(x, row(norm_g), win, row(norm_q), row(norm_kv), cos, sin,
      row(_pad_lanes(kn_rope)), row(_pad_lanes(_swap_halves(kn_rope))), wq, row(qn_nope),
      row(_pad_lanes(qn_rope)), row(_pad_lanes(_swap_halves(qn_rope))), wuk, row(kn_nope))


def _flash_kernel(qi_tab, ki_tab, q_ref, k_ref, vt_ref, o_ref, m, l, acc):
    step = pl.program_id(1)
    qi, ki = qi_tab[step], ki_tab[step]

    @pl.when(ki == 0)
    def _():
        m[...] = jnp.full_like(m, NEG_INF)
        l[...] = jnp.zeros_like(l)
        acc[...] = jnp.zeros_like(acc)

    def update(h, masked):
        s = _dot_nt(k_ref[h], q_ref[h])
        if masked:
            key = lax.broadcasted_iota(I32, s.shape, 0)
            qry = lax.broadcasted_iota(I32, s.shape, 1)
            s = jnp.where(key <= qry, s, NEG_INF)
        m_new = jnp.maximum(m[h], jnp.max(s, axis=0, keepdims=True))
        corr = jnp.exp2(m[h] - m_new)
        p = jnp.exp2(s - m_new)
        l[h] = l[h] * corr + jnp.sum(p, axis=0, keepdims=True)
        acc[h] = acc[h] * corr + _dot(vt_ref[...], p.astype(BF16))
        m[h] = m_new

    def heads(masked):
        def pair(hp, carry):
            update(2 * hp, masked)
            update(2 * hp + 1, masked)
            return carry
        lax.fori_loop(0, N_HEADS // 2, pair, 0)

    @pl.when(ki < qi)
    def _():
        heads(False)

    @pl.when(ki == qi)
    def _():
        heads(True)
        for h in range(N_HEADS):
            o_ref[h] = (acc[h] / l[h]).T.astype(o_ref.dtype)


def _mla_prompt_attn(qcat, kcat, cbt, n_b, seq, tq=512):
    nq = seq // tq
    pairs = [(qi, ki) for qi in range(nq) for ki in range(qi + 1)]
    qi_tab = jnp.array([p[0] for p in pairs], I32)
    ki_tab = jnp.array([p[1] for p in pairs], I32)
    grid_spec = pltpu.PrefetchScalarGridSpec(
        num_scalar_prefetch=2, grid=(n_b, len(pairs)),
        in_specs=[pl.BlockSpec((N_HEADS, tq, 256), lambda b, s, qt, kt: (0, b * nq + qt[s], 0)),
                  pl.BlockSpec((N_HEADS, tq, 256), lambda b, s, qt, kt: (0, b * nq + kt[s], 0)),
                  pl.BlockSpec((KV_RANK, tq), lambda b, s, qt, kt: (0, b * nq + kt[s]))],
        out_specs=pl.BlockSpec((N_HEADS, tq, KV_RANK), lambda b, s, qt, kt: (0, b * nq + qt[s], 0)),
        scratch_shapes=[pltpu.VMEM((N_HEADS, 1, tq), F32), pltpu.VMEM((N_HEADS, 1, tq), F32),
                        pltpu.VMEM((N_HEADS, KV_RANK, tq), F32)])
    return pl.pallas_call(
        _flash_kernel, grid_spec=grid_spec,
        out_shape=jax.ShapeDtypeStruct((N_HEADS, n_b * seq, KV_RANK), BF16),
        compiler_params=_cparams(("parallel", "arbitrary")), name="mla_prompt_attn",
    )(qi_tab, ki_tab, qcat, kcat, cbt)


def _sample_attn_kernel(pt_ref, *refs, npg, n_chunks, dq):
    lat = refs[:npg]
    krt = refs[npg:2 * npg]
    cnew, krtnew, q_ref, wuk_ref, wukt_ref, gkn_ref, o_ref, lhs, m, l, acc = refs[2 * npg:]
    kc = pl.program_id(1)
    nhq = N_HEADS * dq
    nkd = N_HEADS * QK_NOPE

    @pl.when(kc == 0)
    def _():
        m[...] = jnp.full_like(m, NEG_INF)
        l[...] = jnp.zeros_like(l)
        acc[...] = jnp.zeros_like(acc)
        lhs[0:nkd, :] = wukt_ref[...]
        qn = (q_ref[:, :QK_NOPE] * gkn_ref[...]).astype(BF16)
        qabs = [_dot_nt(qn[h * dq:(h + 1) * dq], wuk_ref[h]) for h in range(N_HEADS)]
        lhs[nkd:nkd + nhq, :] = jnp.concatenate(qabs, axis=0).astype(BF16)

    def attend(c_pages, krt_pages, mask):
        ct = jnp.concatenate([c.T for c in c_pages], axis=1).astype(BF16)
        cb = jnp.concatenate(c_pages, axis=0).astype(BF16)
        kt = jnp.concatenate(krt_pages, axis=1).astype(BF16)
        nk = ct.shape[1]
        r = _dot(lhs[...], ct)
        rinv = []
        for h in range(N_HEADS):
            kn = r[h * QK_NOPE:(h + 1) * QK_NOPE]
            ssq = jnp.sum(kn * kn, axis=0, keepdims=True)
            rinv.append(jnp.broadcast_to(lax.rsqrt(ssq * (1.0 / QK_NOPE) + NORM_EPS), (dq, nk)))
        qr = q_ref[:, QK_NOPE:QK_NOPE + QK_ROPE].astype(BF16)
        s = (r[nkd:] * jnp.concatenate(rinv, axis=0) + _dot(qr, kt)) * ATTN_SCALE
        if mask is not None:
            s = jnp.where(mask, s, NEG_INF)
        m_new = jnp.maximum(m[...], jnp.max(s, axis=-1, keepdims=True))
        corr = jnp.exp(m[...] - m_new)
        p = jnp.exp(s - m_new)
        l[...] = l[...] * corr + jnp.sum(p, axis=-1, keepdims=True)
        acc[...] = acc[...] * corr + _dot(p.astype(BF16), cb)
        m[...] = m_new

    @pl.when(kc < n_chunks)
    def _():
        per = max(1, min(npg, KEY_BLOCK // PAGE))
        for j in range(0, npg, per):
            attend([lat[k][...] for k in range(j, j + per)], [krt[k][...] for k in range(j, j + per)], None)

    @pl.when(kc == n_chunks)
    def _():
        key = lax.broadcasted_iota(I32, (nhq, PAGE), 1)
        qpos = lax.broadcasted_iota(I32, (nhq, PAGE), 0) % dq
        attend([cnew[...]], [krtnew[...]], key <= qpos)
        o_ref[...] = (acc[...] / l[...]).astype(o_ref.dtype)


def _mla_sample_attn(q_s, c_new, kr_new, cache_latent, cache_krope, page_table, wuk, gkn, n_b, dq):
    n_pages = page_table.shape[1]
    npg = math.gcd(n_pages, MAX_PAGES_PER_STEP)
    n_chunks = n_pages // npg
    nhq = N_HEADS * dq
    nkd = N_HEADS * QK_NOPE
    wukt = wuk.transpose(0, 2, 1).reshape(nkd, KV_RANK)
    pad = lambda a: jnp.concatenate([a, jnp.zeros((n_b, PAGE - dq, a.shape[-1]), a.dtype)], axis=1)
    krt_cache = cache_krope.swapaxes(1, 2)
    krt_new = pad(kr_new).swapaxes(1, 2)
    page = lambda j, r, w: pl.BlockSpec(
        (None, r, w), lambda b, kc, pt: (pt[b, jnp.minimum(kc, n_chunks - 1) * npg + j], 0, 0))
    per_b = lambda r, w: pl.BlockSpec((None, r, w), lambda b, kc, pt: (b, 0, 0))
    grid_spec = pltpu.PrefetchScalarGridSpec(
        num_scalar_prefetch=1, grid=(n_b, n_chunks + 1),
        in_specs=([page(j, PAGE, KV_RANK) for j in range(npg)] + [page(j, QK_ROPE, PAGE) for j in range(npg)]
                  + [per_b(PAGE, KV_RANK), per_b(QK_ROPE, PAGE), per_b(nhq, 256),
                     pl.BlockSpec((N_HEADS, KV_RANK, QK_NOPE), lambda b, kc, pt: (0, 0, 0)),
                     pl.BlockSpec((nkd, KV_RANK), lambda b, kc, pt: (0, 0)),
                     pl.BlockSpec((1, QK_NOPE), lambda b, kc, pt: (0, 0))]),
        out_specs=per_b(nhq, KV_RANK),
        scratch_shapes=[pltpu.VMEM((nkd + nhq, KV_RANK), BF16),
                        pltpu.VMEM((nhq, 1), F32), pltpu.VMEM((nhq, 1), F32), pltpu.VMEM((nhq, KV_RANK), F32)])
    return pl.pallas_call(
        functools.partial(_sample_attn_kernel, npg=npg, n_chunks=n_chunks, dq=dq), grid_spec=grid_spec,
        out_shape=jax.ShapeDtypeStruct((n_b, nhq, KV_RANK), BF16),
        compiler_params=_cparams(("parallel", "arbitrary")), name="mla_sample_attn",
    )(page_table, *([cache_latent] * npg), *([krt_cache] * npg), pad(c_new), krt_new, q_s,
      wuk, wukt, gkn.reshape(1, QK_NOPE).astype(F32))


def _mla_out_kernel(op_ref, os_ref, x_ref, wuv_ref, wo_ref, out_ref, *, n0):
    first = pl.program_id(0) < n0
    v = jnp.concatenate([_dot(jnp.where(first, op_ref[h], os_ref[h]), wuv_ref[h]) for h in range(N_HEADS)], axis=1)
    out_ref[...] = x_ref[...] + _dot(v.astype(BF16), wo_ref[...])


def _mla_out(o_p, o_s, x, w_uv, w_o, tm=512):
    t = x.shape[0]
    n0 = o_p.shape[1] // tm
    return pl.pallas_call(
        functools.partial(_mla_out_kernel, n0=n0), grid=(t // tm,),
        in_specs=[pl.BlockSpec((N_HEADS, tm, KV_RANK), lambda i: (0, jnp.minimum(i, n0 - 1), 0)),
                  pl.BlockSpec((N_HEADS, tm, KV_RANK), lambda i: (0, jnp.maximum(i - n0, 0), 0)),
                  pl.BlockSpec((tm, D), lambda i: (i, 0)),
                  pl.BlockSpec((N_HEADS, KV_RANK, 128), lambda i: (0, 0, 0)), pl.BlockSpec((D, D), lambda i: (0, 0))],
        out_specs=pl.BlockSpec((tm, D), lambda i: (i, 0)),
        out_shape=jax.ShapeDtypeStruct((t, D), F32),
        compiler_params=_cparams(("parallel",)), name="mla_out",
    )(o_p, o_s, x, w_uv.transpose(1, 0, 2).astype(BF16), w_o.astype(BF16))


def _mla_layer(x, norm_g, cache_latent, cache_krope, page_table, params, dims):
    n_bp, seq_p, n_bs, seq_s = dims
    tp = n_bp * seq_p
    past_len = page_table.shape[1] * cache_latent.shape[1]
    w_uk, w_uv, kn_nope, w_o = params[4], params[5], params[8], params[10]
    c_p, c_s, kr_p, kr_s, cbt, qcat, kcat = _mla_proj(x, norm_g, params, dims, past_len)
    c_s = c_s.reshape(n_bs, seq_s, KV_RANK)
    kr_s = kr_s[:, :QK_ROPE].reshape(n_bs, seq_s, QK_ROPE)
    o_p = _mla_prompt_attn(qcat, kcat, cbt, n_bp, seq_p)
    q_s = qcat[:, tp:].astype(F32).reshape(N_HEADS, n_bs, seq_s, 256).transpose(1, 0, 2, 3)
    q_s = q_s.reshape(n_bs, N_HEADS * seq_s, 256)
    o_s = _mla_sample_attn(q_s, c_s, kr_s, cache_latent, cache_krope, page_table,
                           w_uk.transpose(1, 0, 2).astype(BF16), kn_nope, n_bs, seq_s)
    o_s = o_s.reshape(n_bs, N_HEADS, seq_s, KV_RANK).transpose(1, 0, 2, 3).reshape(N_HEADS, n_bs * seq_s, KV_RANK)
    x = _mla_out(o_p, o_s, x, w_uv, w_o)
    state = (c_p.reshape(n_bp, seq_p, KV_RANK), c_s, kr_p[:, :QK_ROPE].reshape(n_bp, seq_p, QK_ROPE), kr_s)
    return x, state


def _pool_kernel(x_ref, prev_ref, gn_ref, w_ref, sc_ref, o_ref, st_ref, ext, *, tl, start, last):
    ti = pl.program_id(1)

    @pl.when(ti == 0)
    def _():
        ext[0:POOL_HALO, :] = prev_ref[...]

    @pl.when(ti > 0)
    def _():
        ext[0:POOL_HALO, :] = ext[tl:tl + POOL_HALO, :]

    x = x_ref[...]
    h = _rms(x, gn_ref[...])
    ext[POOL_HALO:POOL_HALO + tl, :] = h
    pos = start + ti * tl + lax.broadcasted_iota(I32, (tl, 1), 0)
    run = ext[...]
    z = []
    for g, w in enumerate(POOL_WINDOWS):
        run = run[:, (POOL_GROUP if g else 0):]
        run = run + pltpu.roll(run, w // 2, 0)
        total = run[POOL_HALO:POOL_HALO + tl, :POOL_GROUP]
        count = jnp.minimum(w, pos + 1).astype(F32)
        p = total / count - h[:, g * POOL_GROUP:(g + 1) * POOL_GROUP]
        z.append(_dot(p.astype(BF16), w_ref[g]))
    o_ref[...] = x + jnp.concatenate(z, axis=1) * sc_ref[...]

    @pl.when(ti == last)
    def _():
        st_ref[...] = ext[tl:tl + POOL_HALO, :]


def _pool_stream(x, prev, norm_g, pool_w, pool_scale, n_b, seq, row0, start, tl):
    nt = seq // tl
    blk0 = row0 // tl
    prev = jnp.concatenate([jnp.zeros((n_b, 1, D), F32), prev.astype(F32)], axis=1)
    rows = lambda b, t: (blk0 + b * nt + t, 0)
    out, st = pl.pallas_call(
        functools.partial(_pool_kernel, tl=tl, start=start, last=nt - 1), grid=(n_b, nt),
        in_specs=[pl.BlockSpec((tl, D), rows),
                  pl.BlockSpec((None, POOL_HALO, D), lambda b, t: (b, 0, 0)),
                  pl.BlockSpec((1, D), lambda b, t: (0, 0)),
                  pl.BlockSpec((4, POOL_GROUP, POOL_GROUP), lambda b, t: (0, 0, 0)),
                  pl.BlockSpec((1, D), lambda b, t: (0, 0))],
        out_specs=[pl.BlockSpec((tl, D), rows),
                   pl.BlockSpec((None, POOL_HALO, D), lambda b, t: (b, 0, 0))],
        out_shape=[jax.ShapeDtypeStruct(x.shape, F32), jax.ShapeDtypeStruct((n_b, POOL_HALO, D), F32)],
        scratch_shapes=[pltpu.VMEM((POOL_HALO + tl, D), F32)],
        input_output_aliases={0: 0},
        compiler_params=_cparams(("parallel", "arbitrary")), name="pool_mixer",
    )(x, prev, norm_g.reshape(1, D), pool_w.astype(BF16), pool_scale.reshape(1, D))
    return out, st[:, 1:]


def _pool_layer(x, norm_g, state_pool, params, dims, past_len):
    n_bp, seq_p, n_bs, seq_s = dims
    tp = n_bp * seq_p
    pool_w, pool_scale = params
    zero = jnp.zeros((n_bp, POOL_HALO - 1, D), F32)
    x, st_p = _pool_stream(x, zero, norm_g, pool_w, pool_scale, n_bp, seq_p, 0, 0, min(512, seq_p))
    x, st_s = _pool_stream(x, state_pool, norm_g, pool_w, pool_scale, n_bs, seq_s, tp, past_len, seq_s)
    return x, (st_p, st_s)


def _router_kernel(x_ref, gn_ref, wr_ref, tri_ref, hb_ref, sel_ref, gate_ref, cnt_ref):
    h = _rms(x_ref[...], gn_ref[...])
    hb_ref[...] = h.astype(BF16)
    logits = lax.dot_general(wr_ref[...], h, (((1,), (1,)), ((), ())), precision=lax.Precision.HIGHEST,
                             preferred_element_type=F32)
    idx = lax.broadcasted_iota(I32, logits.shape, 0)
    m1 = jnp.max(logits, axis=0, keepdims=True)
    i1 = jnp.min(jnp.where(logits == m1, idx, N_EXPERTS), axis=0, keepdims=True)
    first = idx == i1
    rest = jnp.where(first, -jnp.inf, logits)
    m2 = jnp.max(rest, axis=0, keepdims=True)
    i2 = jnp.min(jnp.where(rest == m2, idx, N_EXPERTS), axis=0, keepdims=True)
    second = idx == i2
    e2 = jnp.exp(m2 - m1)
    g1 = 1.0 / (1.0 + e2)
    g2 = e2 / (1.0 + e2)
    chosen = jnp.where(first | second, 1.0, 0.0)
    rank = _dot(chosen.astype(BF16), tri_ref[...])
    rank1 = jnp.sum(jnp.where(first, rank, 0.0), axis=0, keepdims=True)
    rank2 = jnp.sum(jnp.where(second, rank, 0.0), axis=0, keepdims=True)
    zero = jnp.zeros((N_EXPERTS - 4,) + i1.shape[1:], I32)
    sel_ref[...] = jnp.concatenate([i1, i2, rank1.astype(I32), rank2.astype(I32), zero], axis=0)
    gate_ref[...] = jnp.concatenate([g1, g2, jnp.zeros((N_EXPERTS - 2,) + g1.shape[1:], F32)], axis=0)
    cnt_ref[...] = jnp.broadcast_to(jnp.sum(chosen, axis=1, keepdims=True), cnt_ref.shape).astype(I32)


def _moe_route(x, norm_g, w_router):
    t = x.shape[0]
    tb = MOE_TB
    nblk = t // tb
    tri = (jnp.arange(tb)[:, None] < jnp.arange(tb)[None, :]).astype(BF16)
    blk = lambda dt: (pl.BlockSpec((None, N_EXPERTS, tb), lambda i: (i, 0, 0)),
                      jax.ShapeDtypeStruct((nblk, N_EXPERTS, tb), dt))
    (sel_spec, sel_shape), (gate_spec, gate_shape) = blk(I32), blk(F32)
    return pl.pallas_call(
        _router_kernel, grid=(nblk,),
        in_specs=[pl.BlockSpec((tb, D), lambda i: (i, 0)), pl.BlockSpec((1, D), lambda i: (0, 0)),
                  pl.BlockSpec((N_EXPERTS, D), lambda i: (0, 0)), pl.BlockSpec((tb, tb), lambda i: (0, 0))],
        out_specs=[pl.BlockSpec((tb, D), lambda i: (i, 0)), sel_spec, gate_spec,
                   pl.BlockSpec((None, N_EXPERTS, 128), lambda i: (i, 0, 0))],
        out_shape=[jax.ShapeDtypeStruct((t, D), BF16), sel_shape, gate_shape,
                   jax.ShapeDtypeStruct((nblk, N_EXPERTS, 128), I32)],
        compiler_params=_cparams(("parallel",)), name="moe_router",
    )(x, norm_g.reshape(1, D), w_router.T.astype(F32), tri)


def _moe_rows(t):
    nblk = t // MOE_TB
    worst = 2 * t + nblk * N_EXPERTS * (MOE_ALIGN - 1) + N_EXPERTS * (MOE_TM - 1)
    return (worst + MOE_TM - 1) // MOE_TM * MOE_TM


def _moe_plan(sel, counts, t):
    cnt = counts[:, :, 0]
    cpad = (cnt + MOE_ALIGN - 1) // MOE_ALIGN * MOE_ALIGN
    per_e = jnp.sum(cpad, axis=0)
    per_e_pad = (per_e + MOE_TM - 1) // MOE_TM * MOE_TM
    e_start = jnp.cumsum(per_e_pad) - per_e_pad
    off = e_start[None, :] + jnp.cumsum(cpad, axis=0) - cpad
    lend = jnp.cumsum(cpad, axis=1)
    lstart = lend - cpad
    e1, e2, r1, r2 = sel[:, 0], sel[:, 1], sel[:, 2], sel[:, 3]
    experts = jnp.arange(N_EXPERTS)
    pick = lambda tab, e: jnp.sum(jnp.where(e[:, :, None] == experts, tab[:, None, :], 0), axis=2)
    dstl = jnp.stack([pick(lstart, e1) + r1, pick(lstart, e2) + r2], axis=1).astype(I32)
    prow = jnp.arange(MOE_PIECES)[None, :] * MOE_ALIGN
    pe = jnp.minimum(jnp.sum(prow[:, :, None] >= lend[:, None, :], axis=2), N_EXPERTS - 1)
    grow = pick(off, pe) + prow - pick(lstart, pe)
    n_pieces = lend[:, -1] // MOE_ALIGN
    grow = jnp.where(jnp.arange(MOE_PIECES)[None, :] < n_pieces[:, None], grow, 0)
    n_tiles = _moe_rows(t) // MOE_TM
    e_end = e_start + per_e_pad
    tile_e = jnp.sum((jnp.arange(n_tiles) * MOE_TM)[:, None] >= e_end[None, :], axis=1)
    active = (tile_e < N_EXPERTS).astype(I32)
    tile_e = jnp.where(active == 1, tile_e, N_EXPERTS - 1)
    return dict(dstl=dstl, grow=grow.astype(I32).reshape(-1), n_pieces=n_pieces.astype(I32),
                tile_e=tile_e.astype(I32), active=active)


def _dispatch_kernel(np_ref, grow_ref, hb_ref, dst_ref, gate_ref, xs_in, xs_ref, buf, sem):
    del xs_in
    b = pl.program_id(0)
    nblk = pl.num_programs(0)
    slot = b % 2

    def copy(blk, i, s):
        r = pl.multiple_of(grow_ref[blk * MOE_PIECES + i], MOE_ALIGN)
        src = buf.at[s, pl.ds(pl.multiple_of(i * MOE_ALIGN, MOE_ALIGN), MOE_ALIGN), :]
        return pltpu.make_async_copy(src, xs_ref.at[pl.ds(r, MOE_ALIGN), :], sem.at[s])

    def drain(blk, s):
        lax.fori_loop(0, np_ref[blk], lambda i, c: (copy(blk, i, s).wait(), c)[1], 0)

    @pl.when(b >= 2)
    def _():
        drain(b - 2, slot)

    want = lax.broadcasted_iota(I32, (MOE_KL, 1), 0)
    hit1 = dst_ref[0:1, :] == want
    hit2 = dst_ref[1:2, :] == want
    onehot = jnp.where(hit1 | hit2, 1.0, 0.0).astype(BF16)
    buf[slot, :, 0:D] = _dot(onehot, hb_ref[...]).astype(BF16)
    gate = jnp.sum(jnp.where(hit1, gate_ref[0:1, :], 0.0) + jnp.where(hit2, gate_ref[1:2, :], 0.0),
                   axis=1, keepdims=True)
    g_hi = gate.astype(BF16).astype(F32)
    g_mid = (gate - g_hi).astype(BF16).astype(F32)
    g_lo = gate - g_hi - g_mid
    lane = lax.broadcasted_iota(I32, (MOE_KL, 128), 1)
    slab = jnp.where(lane == 0, g_hi, jnp.where(lane == 1, g_mid, jnp.where(lane == 2, g_lo, 0.0)))
    buf[slot, :, D:MOE_XW] = slab.astype(BF16)
    lax.fori_loop(0, np_ref[b], lambda i, c: (copy(b, i, slot).start(), c)[1], 0)

    @pl.when(b == nblk - 1)
    def _():
        @pl.when(b >= 1)
        def _():
            drain(b - 1, 1 - slot)
        drain(b, slot)


def _moe_dispatch(hb, gate, plan):
    t = hb.shape[0]
    nblk = t // MOE_TB
    rows = _moe_rows(t)
    grid_spec = pltpu.PrefetchScalarGridSpec(
        num_scalar_prefetch=2, grid=(nblk,),
        in_specs=[pl.BlockSpec((MOE_TB, D), lambda b, *_: (b, 0)),
                  pl.BlockSpec((None, 2, MOE_TB), lambda b, *_: (b, 0, 0)),
                  pl.BlockSpec((None, N_EXPERTS, MOE_TB), lambda b, *_: (b, 0, 0)),
                  pl.BlockSpec(memory_space=pl.ANY)],
        out_specs=pl.BlockSpec(memory_space=pl.ANY),
        scratch_shapes=[pltpu.VMEM((2, MOE_KL, MOE_XW), BF16), pltpu.SemaphoreType.DMA((2,))])
    return pl.pallas_call(
        _dispatch_kernel, grid_spec=grid_spec,
        out_shape=jax.ShapeDtypeStruct((rows, MOE_XW), BF16),
        input_output_aliases={5: 0},
        compiler_params=_cparams(("arbitrary",)), name="moe_dispatch",
    )(plan["n_pieces"], plan["grow"], hb, plan["dstl"], gate, jnp.zeros((rows, MOE_XW), BF16))


def _expert_kernel(te_ref, act_ref, x_ref, wg_ref, wu_ref, wo_ref, o_ref, acc, *, nf):
    i, f = pl.program_id(0), pl.program_id(1)

    @pl.when(f == 0)
    def _():
        acc[...] = jnp.zeros_like(acc)

    @pl.when(act_ref[i] == 1)
    def _():
        x = x_ref[:, 0:D]
        t = (jax.nn.silu(_dot(x, wg_ref[...])) * _dot(x, wu_ref[...])).astype(BF16)
        acc[...] += _dot(t, wo_ref[...])

    @pl.when(f == nf - 1)
    def _():
        gate = jnp.sum(x_ref[:, D:MOE_XW].astype(F32), axis=1, keepdims=True)
        o_ref[...] = (acc[...] * gate).astype(o_ref.dtype)


def _moe_experts(xs, w_in, w_out, plan, tf=512):
    rows = xs.shape[0]
    d_ff = w_out.shape[1]
    nf = d_ff // tf
    fe = lambda i, f, act: jnp.where(act[i] == 1, f, nf - 1)
    grid_spec = pltpu.PrefetchScalarGridSpec(
        num_scalar_prefetch=2, grid=(rows // MOE_TM, nf),
        in_specs=[pl.BlockSpec((MOE_TM, MOE_XW), lambda i, f, te, act: (i, 0)),
                  pl.BlockSpec((None, D, tf), lambda i, f, te, act: (te[i], 0, fe(i, f, act))),
                  pl.BlockSpec((None, D, tf), lambda i, f, te, act: (te[i], 0, fe(i, f, act) + nf)),
                  pl.BlockSpec((None, tf, D), lambda i, f, te, act: (te[i], fe(i, f, act), 0))],
        out_specs=pl.BlockSpec((MOE_TM, D), lambda i, f, te, act: (i, 0)),
        scratch_shapes=[pltpu.VMEM((MOE_TM, D), F32)])
    return pl.pallas_call(
        functools.partial(_expert_kernel, nf=nf), grid_spec=grid_spec,
        out_shape=jax.ShapeDtypeStruct((rows, D), BF16),
        compiler_params=_cparams(("parallel", "arbitrary")), name="moe_experts",
    )(plan["tile_e"], plan["active"], xs, w_in, w_in, w_out)


def _combine_kernel(np_ref, grow_ref, x_ref, dst_ref, ys_ref, *rest, n0):
    outs, (buf, sem) = rest[:-2], rest[-2:]
    b = pl.program_id(0)
    nblk = pl.num_programs(0)
    slot = b % 2

    def copy(blk, i, s):
        r = pl.multiple_of(grow_ref[blk * MOE_PIECES + i], MOE_ALIGN)
        dst = buf.at[s, pl.ds(pl.multiple_of(i * MOE_ALIGN, MOE_ALIGN), MOE_ALIGN), :]
        return pltpu.make_async_copy(ys_ref.at[pl.ds(r, MOE_ALIGN), :], dst, sem.at[s])

    def fetch(blk, s):
        lax.fori_loop(0, np_ref[blk], lambda i, c: (copy(blk, i, s).start(), c)[1], 0)

    @pl.when(b == 0)
    def _():
        buf[...] = jnp.zeros_like(buf)
        fetch(0, 0)

    @pl.when(b + 1 < nblk)
    def _():
        fetch(b + 1, 1 - slot)

    want = lax.broadcasted_iota(I32, (1, MOE_KL), 1)
    onehot = jnp.where((dst_ref[:, 0:1] == want) | (dst_ref[:, 1:2] == want), 1.0, 0.0).astype(BF16)
    lax.fori_loop(0, np_ref[b], lambda i, c: (copy(b, i, slot).wait(), c)[1], 0)
    y = x_ref[...] + _dot(onehot, buf[slot])
    if len(outs) == 1:
        outs[0][...] = y
    else:
        @pl.when(b < n0)
        def _():
            outs[0][...] = y

        @pl.when(b >= n0)
        def _():
            outs[1][...] = y


def _moe_combine(x, ys, dst_tok, plan, split_rows):
    t = x.shape[0]
    nblk = t // MOE_TB
    tile = lambda m: pl.BlockSpec((MOE_TB, D), m)
    if split_rows is None:
        n0 = nblk
        out_specs = tile(lambda b, *_: (b, 0))
        out_shape = jax.ShapeDtypeStruct((t, D), F32)
    else:
        n0 = split_rows // MOE_TB
        out_specs = [tile(lambda b, *_: (jnp.minimum(b, n0 - 1), 0)), tile(lambda b, *_: (jnp.maximum(b - n0, 0), 0))]
        out_shape = [jax.ShapeDtypeStruct((split_rows, D), F32), jax.ShapeDtypeStruct((t - split_rows, D), F32)]
    grid_spec = pltpu.PrefetchScalarGridSpec(
        num_scalar_prefetch=2, grid=(nblk,),
        in_specs=[tile(lambda b, *_: (b, 0)),
                  pl.BlockSpec((MOE_TB, 2), lambda b, *_: (b, 0)),
                  pl.BlockSpec(memory_space=pl.ANY)],
        out_specs=out_specs,
        scratch_shapes=[pltpu.VMEM((2, MOE_KL, D), BF16), pltpu.SemaphoreType.DMA((2,))])
    return pl.pallas_call(
        functools.partial(_combine_kernel, n0=n0), grid_spec=grid_spec, out_shape=out_shape,
        compiler_params=_cparams(("arbitrary",)), name="moe_combine",
    )(plan["n_pieces"], plan["grow"], x, dst_tok, ys)


def _moe_layer(x, norm_g, w_router, w_in, w_out, split_rows=None):
    t = x.shape[0]
    hb, sel, gate, counts = _moe_route(x, norm_g, w_router)
    plan = _moe_plan(sel, counts, t)
    xs = _moe_dispatch(hb, gate, plan)
    ys = _moe_experts(xs, w_in.astype(BF16), w_out.astype(BF16), plan)
    dst_tok = plan["dstl"].transpose(0, 2, 1).reshape(t, 2)
    return _moe_combine(x, ys, dst_tok, plan, split_rows)


def kernel(x_prompt, x_sample, state_l0_s5_re, state_l0_s5_im, cache_l1_latent, cache_l1_krope, state_l2_pool, state_l3_s5_re, state_l3_s5_im, page_table, l0_norm_mix, l0_s5_lambda_re, l0_s5_lambda_im, l0_s5_log_dt, l0_s5_b_re, l0_s5_b_im, l0_s5_c_re, l0_s5_c_im, l0_s5_d, l0_s5_w_glu, l0_norm_ffn, l0_ffn_w_in, l0_ffn_w_out, l1_norm_mix, l1_mla_w_in, l1_mla_norm_q, l1_mla_norm_kv, l1_mla_w_q, l1_mla_w_uk, l1_mla_w_uv, l1_mla_qn_nope, l1_mla_qn_rope, l1_mla_kn_nope, l1_mla_kn_rope, l1_mla_w_o, l1_norm_ffn, l1_moe_w_router, l1_moe_w_in, l1_moe_w_out, l2_norm_mix, l2_pool_w, l2_pool_scale, l2_norm_ffn, l2_ffn_w_in, l2_ffn_w_out, l3_norm_mix, l3_s5_lambda_re, l3_s5_lambda_im, l3_s5_log_dt, l3_s5_b_re, l3_s5_b_im, l3_s5_c_re, l3_s5_c_im, l3_s5_d, l3_s5_w_glu, l3_norm_ffn, l3_moe_w_router, l3_moe_w_in, l3_moe_w_out):
    n_bp, seq_p, _ = x_prompt.shape
    n_bs, seq_s, _ = x_sample.shape
    dims = (n_bp, seq_p, n_bs, seq_s)
    tp, ts = n_bp * seq_p, n_bs * seq_s
    past_len = page_table.shape[1] * cache_l1_latent.shape[1]

    x, st0 = _s5_layer((x_prompt.reshape(tp, D), x_sample.reshape(ts, D)), l0_norm_mix, state_l0_s5_re, state_l0_s5_im,
                       (l0_s5_lambda_re, l0_s5_lambda_im, l0_s5_log_dt, l0_s5_b_re, l0_s5_b_im,
                        l0_s5_c_re, l0_s5_c_im, l0_s5_d, l0_s5_w_glu), dims)
    x = _ffn_dense(x, l0_norm_ffn, l0_ffn_w_in, l0_ffn_w_out)

    x, st1 = _mla_layer(x, l1_norm_mix, cache_l1_latent, cache_l1_krope, page_table,
                        (l1_mla_w_in, l1_mla_norm_q, l1_mla_norm_kv, l1_mla_w_q, l1_mla_w_uk, l1_mla_w_uv,
                         l1_mla_qn_nope, l1_mla_qn_rope, l1_mla_kn_nope, l1_mla_kn_rope, l1_mla_w_o), dims)
    x = _moe_layer(x, l1_norm_ffn, l1_moe_w_router, l1_moe_w_in, l1_moe_w_out)

    x, st2 = _pool_layer(x, l2_norm_mix, state_l2_pool, (l2_pool_w, l2_pool_scale), dims, past_len)
    x = _ffn_dense(x, l2_norm_ffn, l2_ffn_w_in, l2_ffn_w_out)

    x, st3 = _s5_layer((x,), l3_norm_mix, state_l3_s5_re, state_l3_s5_im,
                       (l3_s5_lambda_re, l3_s5_lambda_im, l3_s5_log_dt, l3_s5_b_re, l3_s5_b_im,
                        l3_s5_c_re, l3_s5_c_im, l3_s5_d, l3_s5_w_glu), dims)
    y_p, y_s = _moe_layer(x, l3_norm_ffn, l3_moe_w_router, l3_moe_w_in, l3_moe_w_out, split_rows=tp)

    return (y_p.reshape(n_bp, seq_p, D), y_s.reshape(n_bs, seq_s, D)) + st0 + st1 + st2 + st3
```

```python
import functools
import math

import jax
import jax.numpy as jnp
from jax import lax
from jax.experimental import pallas as pl
from jax.experimental.pallas import tpu as pltpu

F32 = jnp.float32
BF16 = jnp.bfloat16
I32 = jnp.int32

D = 1024
NORM_EPS = 1e-6
NEG_INF = -1e30
S5_GROUP = 16
S5_STATE = 64
N_GROUPS = D // S5_GROUP
S5_T = 8
S5_BUNDLE = 8
N_BUNDLES = N_GROUPS // S5_BUNDLE
S5_W = S5_T * 128
N_HEADS = 8
QK_NOPE = 128
QK_ROPE = 64
Q_RANK = 512
KV_RANK = 256
ROPE_THETA = 10000.0
ATTN_SCALE = (QK_NOPE + QK_ROPE) ** -0.5
LOG2E = math.log2(math.e)
PAGE = 128
PAGES_PER_STEP = 16
POOL_WINDOWS = (2, 4, 8, 16)
POOL_HALO = 16
POOL_GROUP = D // 4
N_EXPERTS = 8
MOE_TB = 512
MOE_TM = 1024
MOE_ALIGN = 16
MOE_KL = (2 * MOE_TB + N_EXPERTS * (MOE_ALIGN - 1) + 127) // 128 * 128
MOE_PIECES = MOE_KL // MOE_ALIGN
MOE_XW = D + 128


def _cparams(sem, vmem_mb=48):
    return pltpu.CompilerParams(dimension_semantics=sem, vmem_limit_bytes=vmem_mb * 1024 * 1024)


def _rms(x, g):
    r = lax.rsqrt(jnp.mean(x * x, axis=-1, keepdims=True) + NORM_EPS)
    return x * r * g


def _dot(a, b):
    return jnp.dot(a, b, preferred_element_type=F32)


def _dot_nt(a, b):
    return lax.dot_general(a, b, (((1,), (1,)), ((), ())), preferred_element_type=F32)


def _lo_map(n):
    return lambda i: (jnp.minimum(i, n - 1), 0)


def _hi_map(n):
    return lambda i: (jnp.maximum(i - n, 0), 0)


def _rmsnorm_kernel(x_ref, g_ref, o_ref):
    o_ref[...] = _rms(x_ref[...], g_ref[...]).astype(o_ref.dtype)


def _rmsnorm(x, g, row0, rows, tm=1024):
    tm = min(tm, rows)
    blk0 = row0 // tm
    return pl.pallas_call(
        _rmsnorm_kernel, grid=(rows // tm,),
        in_specs=[pl.BlockSpec((tm, D), lambda i: (blk0 + i, 0)), pl.BlockSpec((1, D), lambda i: (0, 0))],
        out_specs=pl.BlockSpec((tm, D), lambda i: (i, 0)),
        out_shape=jax.ShapeDtypeStruct((rows, D), F32),
        compiler_params=_cparams(("parallel",)), name="rmsnorm",
    )(x, g.reshape(1, D))


def _s5_weights(lam_re, lam_im, log_dt, b_re, b_im, c_re, c_im, d):
    lr, li = lam_re.astype(F32), lam_im.astype(F32)
    dt = jnp.exp(log_dt.astype(F32))[:, None]
    mag = jnp.exp(lr * dt)
    ar, ai = mag * jnp.cos(li * dt), mag * jnp.sin(li * dt)
    nr, ni = ar - 1.0, ai
    den = lr * lr + li * li
    f_re = ((nr * lr + ni * li) / den)[..., None]
    f_im = ((ni * lr - nr * li) / den)[..., None]
    br, bi = b_re.astype(F32), b_im.astype(F32)
    bb_re, bb_im = f_re * br - f_im * bi, f_re * bi + f_im * br
    cr, ci = c_re.astype(F32), c_im.astype(F32)

    pr, pi = [jnp.ones_like(ar)], [jnp.zeros_like(ar)]
    for _ in range(S5_T):
        pr.append(pr[-1] * ar - pi[-1] * ai)
        pi.append(pr[-2] * ai + pi[-1] * ar)
    pr, pi = jnp.stack(pr), jnp.stack(pi)

    ab_re = pr[:S5_T, :, :, None] * bb_re - pi[:S5_T, :, :, None] * bb_im
    ab_im = pr[:S5_T, :, :, None] * bb_im + pi[:S5_T, :, :, None] * bb_re
    hi = lax.Precision.HIGHEST
    kern = (jnp.einsum("gcp,jgpd->jgcd", cr, ab_re, precision=hi)
            - jnp.einsum("gcp,jgpd->jgcd", ci, ab_im, precision=hi))

    nb, bg, t = N_BUNDLES, S5_BUNDLE, S5_T
    tt = jnp.arange(t)
    lag = tt[None, :] - tt[:, None]
    ksel = jnp.where((lag >= 0)[:, :, None, None, None], kern[jnp.clip(lag, 0, t - 1)], 0.0)
    ksel = ksel.reshape(t, t, nb, bg, S5_GROUP, S5_GROUP).transpose(2, 0, 3, 5, 1, 4)
    m = ksel.astype(BF16).reshape(nb, t * 128, 128)

    inj = jnp.stack([ab_re[::-1], ab_im[::-1]])
    inj = inj.reshape(2, t, nb, bg, S5_STATE, S5_GROUP).transpose(2, 1, 3, 5, 0, 4)
    we = inj.astype(BF16).reshape(nb, t * 128, 128)

    p1r, p1i = pr[1:, :, None, :], pi[1:, :, None, :]
    co_re = cr[None] * p1r - ci[None] * p1i
    co_im = -(cr[None] * p1i + ci[None] * p1r)
    co = jnp.stack([co_re, co_im]).reshape(2, t, nb, bg, S5_GROUP, S5_STATE).transpose(2, 0, 3, 5, 1, 4)
    ws = co.astype(BF16).reshape(nb, 2 * bg * S5_STATE, 128)

    a_t = jnp.stack([pr[t], pi[t]]).reshape(2, nb, bg * S5_STATE)
    a_t = a_t.transpose(1, 0, 2).reshape(nb, 1, 2 * bg * S5_STATE)
    dv = jnp.broadcast_to(d.astype(F32).reshape(1, nb, 128), (t, nb, 128)).transpose(1, 0, 2).reshape(nb, 1, t * 128)
    return m, we, ws, a_t, dv


def _group_spread(n_outer, inner):
    r = jnp.arange(n_outer * inner)[:, None]
    c = jnp.arange(n_outer * S5_BUNDLE * inner)[None, :]
    return ((r // inner == c // (S5_BUNDLE * inner)) & (r % inner == c % inner)).astype(BF16)


def _widen(compact, spread, row_shift, col_shift):
    wide = _dot(compact, spread)
    row_g = (lax.broadcasted_iota(I32, wide.shape, 0) >> row_shift) & (S5_BUNDLE - 1)
    col_h = (lax.broadcasted_iota(I32, wide.shape, 1) >> col_shift) & (S5_BUNDLE - 1)
    return jnp.where(row_g == col_h, wide, 0.0).astype(BF16)


def _s5_kernel(u_ref, s0_ref, mc_ref, wec_ref, wsc_ref, sp_tc, sp_rp, a_ref, d_ref, g_ref, sfin_ref,
               m_ref, we_ref, ws_ref, state, ubuf, sprev, ebuf, *, nb, cb, last):
    i = pl.program_id(1)
    half = S5_BUNDLE * S5_STATE

    @pl.when(i == 0)
    def _():
        state[...] = s0_ref[...]
        m_ref[...] = _widen(mc_ref[...], sp_tc[...], 4, 4)
        we_ref[...] = _widen(wec_ref[...], sp_rp[...], 4, 6)
        ws_ref[...] = _widen(wsc_ref[...], sp_tc[...], 6, 4)

    for c in range(cb):
        for t in range(S5_T):
            ubuf[c * nb:(c + 1) * nb, t * 128:(t + 1) * 128] = u_ref[:, c * S5_T + t, :]
    u = ubuf[...]
    ub = u.astype(BF16)
    ebuf[...] = _dot(ub, we_ref[...])
    ar, ai = a_ref[:, :half], a_ref[:, half:]
    s = state[...]
    for c in range(cb):
        rows = slice(c * nb, (c + 1) * nb)
        sprev[rows, :] = s
        sr, si = s[:, :half], s[:, half:]
        s = jnp.concatenate([ar * sr - ai * si + ebuf[rows, :half],
                             ar * si + ai * sr + ebuf[rows, half:]], axis=1)
    state[...] = s
    y = _dot(ub, m_ref[...]) + _dot(sprev[...].astype(BF16), ws_ref[...])
    ebuf[...] = jax.nn.gelu(y + d_ref[...] * u)
    for c in range(cb):
        for t in range(S5_T):
            g_ref[:, c * S5_T + t, :] = ebuf[c * nb:(c + 1) * nb, t * 128:(t + 1) * 128]

    @pl.when(i == last)
    def _():
        sfin_ref[...] = s


def _s5_scan(h, s0, tables, cb):
    m, we, ws, a_t, dv = tables
    nb, seq, _ = h.shape
    tok = S5_T * cb
    ni = seq // tok
    r = nb * cb
    w = S5_W
    kern = functools.partial(_s5_kernel, nb=nb, cb=cb, last=ni - 1)
    compact = lambda: pl.BlockSpec((None, w, 128), lambda g, i: (g, 0, 0))
    spread = lambda: pl.BlockSpec((128, w), lambda g, i: (0, 0))
    vec = lambda: pl.BlockSpec((None, 1, w), lambda g, i: (g, 0, 0))
    return pl.pallas_call(
        kern, grid=(N_BUNDLES, ni),
        in_specs=[pl.BlockSpec((nb, tok, 128), lambda g, i: (0, i, g)),
                  pl.BlockSpec((None, nb, w), lambda g, i: (g, 0, 0)),
                  compact(), compact(), compact(), spread(), spread(), vec(), vec()],
        out_specs=[pl.BlockSpec((nb, tok, 128), lambda g, i: (0, i, g)),
                   pl.BlockSpec((None, nb, w), lambda g, i: (g, 0, 0))],
        out_shape=[jax.ShapeDtypeStruct((nb, seq, D), F32),
                   jax.ShapeDtypeStruct((N_BUNDLES, nb, w), F32)],
        scratch_shapes=[pltpu.VMEM((w, w), BF16), pltpu.VMEM((w, w), BF16), pltpu.VMEM((w, w), BF16),
                        pltpu.VMEM((nb, w), F32), pltpu.VMEM((r, w), F32), pltpu.VMEM((r, w), F32),
                        pltpu.VMEM((r, w), F32)],
        compiler_params=_cparams(("parallel", "arbitrary")), name="s5_scan",
    )(h, s0, m, we, ws, _group_spread(S5_T, S5_GROUP), _group_spread(2, S5_STATE), a_t, dv)


def _state_to_bundles(s_re, s_im):
    n_b = s_re.shape[0]
    s = jnp.stack([s_re, s_im], axis=1).reshape(n_b, 2, N_BUNDLES, S5_BUNDLE * S5_STATE)
    return s.transpose(2, 0, 1, 3).reshape(N_BUNDLES, n_b, 2 * S5_BUNDLE * S5_STATE)


def _state_from_bundles(s):
    n_b = s.shape[1]
    s = s.reshape(N_BUNDLES, n_b, 2, S5_BUNDLE, S5_STATE).transpose(2, 1, 0, 3, 4)
    s = s.reshape(2, n_b, N_GROUPS, S5_STATE)
    return s[0], s[1]


def _glu_kernel(*refs, n0, two_x):
    if two_x:
        gp_ref, gs_ref, xp_ref, xs_ref, w_ref, o_ref = refs
    else:
        gp_ref, gs_ref, xp_ref, w_ref, o_ref = refs
    first = pl.program_id(0) < n0
    g = jnp.where(first, gp_ref[...], gs_ref[...])
    x = jnp.where(first, xp_ref[...], xs_ref[...]) if two_x else xp_ref[...]
    z = _dot(g.astype(BF16), w_ref[...])
    o_ref[...] = x + z[:, :D] * jax.nn.sigmoid(z[:, D:])


def _glu_residual(g_p, g_s, x_parts, w_glu, tm=512):
    t = g_p.shape[0] + g_s.shape[0]
    n0 = g_p.shape[0] // tm
    two_x = len(x_parts) == 2
    tile = lambda m: pl.BlockSpec((tm, D), m)
    x_specs = [tile(_lo_map(n0)), tile(_hi_map(n0))] if two_x else [tile(lambda i: (i, 0))]
    return pl.pallas_call(
        functools.partial(_glu_kernel, n0=n0, two_x=two_x), grid=(t // tm,),
        in_specs=[tile(_lo_map(n0)), tile(_hi_map(n0))] + x_specs + [pl.BlockSpec((D, 2 * D), lambda i: (0, 0))],
        out_specs=tile(lambda i: (i, 0)),
        out_shape=jax.ShapeDtypeStruct((t, D), F32),
        compiler_params=_cparams(("parallel",)), name="glu_residual",
    )(g_p, g_s, *x_parts, w_glu)


def _s5_layer(x_parts, norm_g, state_re, state_im, params, dims):
    n_bp, seq_p, n_bs, seq_s = dims
    tp, ts = n_bp * seq_p, n_bs * seq_s
    lam_re, lam_im, log_dt, b_re, b_im, c_re, c_im, d, w_glu = params
    tables = _s5_weights(lam_re, lam_im, log_dt, b_re, b_im, c_re, c_im, d)
    if len(x_parts) == 2:
        h_p = _rmsnorm(x_parts[0], norm_g, 0, tp)
        h_s = _rmsnorm(x_parts[1], norm_g, 0, ts)
    else:
        h_p = _rmsnorm(x_parts[0], norm_g, 0, tp)
        h_s = _rmsnorm(x_parts[0], norm_g, tp, ts)
    zeros = jnp.zeros((N_BUNDLES, n_bp, 2 * S5_BUNDLE * S5_STATE), F32)
    cb = math.gcd(seq_p // S5_T, max(1, 512 // n_bp))
    g_p, s_p = _s5_scan(h_p.reshape(n_bp, seq_p, D), zeros, tables, cb)
    g_s, s_s = _s5_scan(h_s.reshape(n_bs, seq_s, D), _state_to_bundles(state_re, state_im), tables, seq_s // S5_T)
    x = _glu_residual(g_p.reshape(tp, D), g_s.reshape(ts, D), x_parts, w_glu.astype(BF16))
    p_re, p_im = _state_from_bundles(s_p)
    s_re, s_im = _state_from_bundles(s_s)
    return x, (p_re, s_re, p_im, s_im)


def _ffn_kernel(x_ref, gn_ref, wg_ref, wu_ref, wo_ref, o_ref, h_scr, acc, *, nf):
    f = pl.program_id(1)

    @pl.when(f == 0)
    def _():
        h_scr[...] = _rms(x_ref[...], gn_ref[...]).astype(BF16)
        acc[...] = jnp.zeros_like(acc)

    h = h_scr[...]
    t = (jax.nn.silu(_dot(h, wg_ref[...])) * _dot(h, wu_ref[...])).astype(BF16)
    acc[...] += _dot(t, wo_ref[...])

    @pl.when(f == nf - 1)
    def _():
        o_ref[...] = x_ref[...] + acc[...]


def _ffn_dense(x, norm_g, w_in, w_out, tm=512, nf=2):
    t = x.shape[0]
    d_ff = w_out.shape[0]
    tf = d_ff // nf
    w_in = w_in.astype(BF16)
    return pl.pallas_call(
        functools.partial(_ffn_kernel, nf=nf), grid=(t // tm, nf),
        in_specs=[pl.BlockSpec((tm, D), lambda i, f: (i, 0)), pl.BlockSpec((1, D), lambda i, f: (0, 0)),
                  pl.BlockSpec((D, tf), lambda i, f: (0, f)), pl.BlockSpec((D, tf), lambda i, f: (0, f + nf)),
                  pl.BlockSpec((tf, D), lambda i, f: (f, 0))],
        out_specs=pl.BlockSpec((tm, D), lambda i, f: (i, 0)),
        out_shape=jax.ShapeDtypeStruct((t, D), F32),
        scratch_shapes=[pltpu.VMEM((tm, D), BF16), pltpu.VMEM((tm, D), F32)],
        compiler_params=_cparams(("parallel", "arbitrary")), name="ffn_dense",
    )(x, norm_g.reshape(1, D), w_in, w_in, w_out.astype(BF16))


def _swap_halves(w):
    half = w.shape[-1] // 2
    return jnp.concatenate([w[..., half:], w[..., :half]], axis=-1)


def _pad_lanes(w, n=128):
    return jnp.concatenate([w, jnp.zeros(w.shape[:-1] + (n - w.shape[-1],), w.dtype)], axis=-1)


def _rope_tables(seq_p, seq_s, past_len, rows_s):
    half = QK_ROPE // 2
    inv_freq = ROPE_THETA ** (-jnp.arange(half, dtype=F32) / half)
    pos = jnp.concatenate([jnp.arange(seq_p), past_len + (jnp.arange(rows_s) % seq_s)])
    ang = pos.astype(F32)[:, None] * inv_freq[None, :]
    cos, sin = jnp.cos(ang), jnp.sin(ang)
    return _pad_lanes(jnp.concatenate([cos, cos], axis=1)), _pad_lanes(jnp.concatenate([-sin, sin], axis=1))


def _rope_norm(x, x_sw, g, g_sw, cos, sin):
    r = lax.rsqrt(jnp.sum(x * x, axis=-1, keepdims=True) * (1.0 / QK_ROPE) + NORM_EPS)
    return r * (x * g * cos + x_sw * g_sw * sin)


def _mla_proj_kernel(x_ref, gmix, win, nq, nkv, cos, sin, gkr, gkrs, wq, gqn, gqr, gqrs, wuk, gkn,
                     cp_out, cs_out, krp_out, krs_out, cbt_out, q_out, k_out, *, n0):
    is_prompt = pl.program_id(0) < n0
    hn = _rms(x_ref[...], gmix[...]).astype(BF16)
    a = _dot(hn, win[...])
    hq = _rms(a[:, :Q_RANK], nq[...]).astype(BF16)
    c = _rms(a[:, Q_RANK:Q_RANK + KV_RANK], nkv[...])
    cb = c.astype(BF16)
    kr = _rope_norm(a[:, 768:896], a[:, 896:1024], gkr[...], gkrs[...], cos[...], sin[...])
    krb = kr.astype(BF16)

    @pl.when(is_prompt)
    def _():
        cp_out[...] = c
        krp_out[...] = kr
        cbt_out[...] = c.T.astype(BF16)

    @pl.when(jnp.logical_not(is_prompt))
    def _():
        cs_out[...] = c
        krs_out[...] = kr

    q_scale = jnp.where(is_prompt, ATTN_SCALE * LOG2E, 1.0)
    for h in range(N_HEADS):
        q3 = _dot(hq, wq[:, h * 384:(h + 1) * 384])
        qn = _rms(q3[:, :128], gqn[...])
        qr = _rope_norm(q3[:, 128:256], q3[:, 256:384], gqr[...], gqrs[...], cos[...], sin[...])
        q_out[h] = (jnp.concatenate([qn, qr], axis=1) * q_scale).astype(BF16)

    @pl.when(is_prompt)
    def _():
        for h in range(N_HEADS):
            kn = _rms(_dot(cb, wuk[:, h * QK_NOPE:(h + 1) * QK_NOPE]), gkn[...])
            k_out[h] = jnp.concatenate([kn.astype(BF16), krb], axis=1)


def _mla_proj(x, norm_g, params, dims, past_len, tm=512):
    n_bp, seq_p, n_bs, seq_s = dims
    t = x.shape[0]
    tp = n_bp * seq_p
    ts = t - tp
    w_in, norm_q, norm_kv, w_q, w_uk, w_uv, qn_nope, qn_rope, kn_nope, kn_rope, w_o = params
    w_kr = w_in[:, Q_RANK + KV_RANK:]
    win = jnp.concatenate([w_in[:, :Q_RANK + KV_RANK], _pad_lanes(w_kr), _pad_lanes(_swap_halves(w_kr))],
                          axis=1).astype(BF16)
    wq3 = w_q.reshape(Q_RANK, N_HEADS, QK_NOPE + QK_ROPE)
    wq_r = wq3[:, :, QK_NOPE:]
    wq = jnp.concatenate([wq3[:, :, :QK_NOPE], _pad_lanes(wq_r), _pad_lanes(_swap_halves(wq_r))], axis=2)
    wq = wq.reshape(Q_RANK, N_HEADS * 384).astype(BF16)
    wuk = w_uk.reshape(KV_RANK, N_HEADS * QK_NOPE).astype(BF16)
    cos, sin = _rope_tables(seq_p, seq_s, past_len, min(tm, ts))
    n_pt = seq_p // tm
    n0 = tp // tm
    tab = lambda i: (jnp.where(i < n0, i % n_pt, n_pt), 0)
    row = lambda v: v.reshape(1, -1).astype(F32)
    vec = lambda n: pl.BlockSpec((1, n), lambda i: (0, 0))
    full = lambda a: pl.BlockSpec(a.shape, lambda i: (0, 0))
    lo = lambda n: pl.BlockSpec((tm, n), _lo_map(n0))
    hi = lambda n: pl.BlockSpec((tm, n), _hi_map(n0))
    sds = jax.ShapeDtypeStruct
    return pl.pallas_call(
        functools.partial(_mla_proj_kernel, n0=n0), grid=(t // tm,),
        in_specs=[pl.BlockSpec((tm, D), lambda i: (i, 0)), vec(D), full(win), vec(Q_RANK), vec(KV_RANK),
                  pl.BlockSpec((tm, 128), tab), pl.BlockSpec((tm, 128), tab), vec(128), vec(128),
                  full(wq), vec(128), vec(128), vec(128), full(wuk), vec(128)],
        out_specs=[lo(KV_RANK), hi(KV_RANK), lo(128), hi(128),
                   pl.BlockSpec((KV_RANK, tm), lambda i: (0, jnp.minimum(i, n0 - 1))),
                   pl.BlockSpec((N_HEADS, tm, 256), lambda i: (0, i, 0)),
                   pl.BlockSpec((N_HEADS, tm, 256), lambda i: (0, jnp.minimum(i, n0 - 1), 0))],
        out_shape=[sds((tp, KV_RANK), F32), sds((ts, KV_RANK), F32), sds((tp, 128), F32), sds((ts, 128), F32),
                   sds((KV_RANK, tp), BF16), sds((N_HEADS, t, 256), BF16), sds((N_HEADS, tp, 256), BF16)],
        compiler_params=_cparams(("arbitrary",)), name="mla_proj",
    )(x, row(norm_g), win, row(norm_q), row(norm_kv), cos, sin,
      row(_pad_lanes(kn_rope)), row(_pad_lanes(_swap_halves(kn_rope))), wq, row(qn_nope),
      row(_pad_lanes(qn_rope)), row(_pad_lanes(_swap_halves(qn_rope))), wuk, row(kn_nope))


def _flash_kernel(qi_tab, ki_tab, q_ref, k_ref, vt_ref, o_ref, m, l, acc):
    step = pl.program_id(1)
    qi, ki = qi_tab[step], ki_tab[step]

    @pl.when(ki == 0)
    def _():
        m[...] = jnp.full_like(m, NEG_INF)
        l[...] = jnp.zeros_like(l)
        acc[...] = jnp.zeros_like(acc)

    def update(h, masked):
        s = _dot_nt(k_ref[h], q_ref[h])
        if masked:
            key = lax.broadcasted_iota(I32, s.shape, 0)
            qry = lax.broadcasted_iota(I32, s.shape, 1)
            s = jnp.where(key <= qry, s, NEG_INF)
        m_new = jnp.maximum(m[h], jnp.max(s, axis=0, keepdims=True))
        corr = jnp.exp2(m[h] - m_new)
        p = jnp.exp2(s - m_new)
        l[h] = l[h] * corr + jnp.sum(p, axis=0, keepdims=True)
        acc[h] = acc[h] * corr + _dot(vt_ref[...], p.astype(BF16))
        m[h] = m_new

    def heads(masked):
        def pair(hp, carry):
            update(2 * hp, masked)
            update(2 * hp + 1, masked)
            return carry
        lax.fori_loop(0, N_HEADS // 2, pair, 0)

    @pl.when(ki < qi)
    def _():
        heads(False)

    @pl.when(ki == qi)
    def _():
        heads(True)
        for h in range(N_HEADS):
            o_ref[h] = (acc[h] / l[h]).T.astype(o_ref.dtype)


def _mla_prompt_attn(qcat, kcat, cbt, n_b, seq, tq=512):
    nq = seq // tq
    pairs = [(qi, ki) for qi in range(nq) for ki in range(qi + 1)]
    qi_tab = jnp.array([p[0] for p in pairs], I32)
    ki_tab = jnp.array([p[1] for p in pairs], I32)
    grid_spec = pltpu.PrefetchScalarGridSpec(
        num_scalar_prefetch=2, grid=(n_b, len(pairs)),
        in_specs=[pl.BlockSpec((N_HEADS, tq, 256), lambda b, s, qt, kt: (0, b * nq + qt[s], 0)),
                  pl.BlockSpec((N_HEADS, tq, 256), lambda b, s, qt, kt: (0, b * nq + kt[s], 0)),
                  pl.BlockSpec((KV_RANK, tq), lambda b, s, qt, kt: (0, b * nq + kt[s]))],
        out_specs=pl.BlockSpec((N_HEADS, tq, KV_RANK), lambda b, s, qt, kt: (0, b * nq + qt[s], 0)),
        scratch_shapes=[pltpu.VMEM((N_HEADS, 1, tq), F32), pltpu.VMEM((N_HEADS, 1, tq), F32),
                        pltpu.VMEM((N_HEADS, KV_RANK, tq), F32)])
    return pl.pallas_call(
        _flash_kernel, grid_spec=grid_spec,
        out_shape=jax.ShapeDtypeStruct((N_HEADS, n_b * seq, KV_RANK), BF16),
        compiler_params=_cparams(("parallel", "arbitrary")), name="mla_prompt_attn",
    )(qi_tab, ki_tab, qcat, kcat, cbt)


def _sample_attn_kernel(pt_ref, lat_hbm, krt_hbm, cnew, krtnew, q_ref, wuk_ref, wukt_ref, gkn_ref, o_ref,
                        lat_buf, krt_buf, sem, lhs, m, l, acc, *, n_b, npg, n_chunks, dq):
    b, kc = pl.program_id(0), pl.program_id(1)
    nhq = N_HEADS * dq
    nkd = N_HEADS * QK_NOPE
    slot = (b * n_chunks + kc) % 2

    def page_copies(bb, cc, s):
        copies = []
        for j in range(npg):
            pid = pt_ref[bb, cc * npg + j]
            copies.append(pltpu.make_async_copy(lat_hbm.at[pid], lat_buf.at[s, j], sem.at[0, s]))
            copies.append(pltpu.make_async_copy(krt_hbm.at[pid], krt_buf.at[s, j], sem.at[1, s]))
        return copies

    @pl.when(kc == 0)
    def _():
        m[...] = jnp.full_like(m, NEG_INF)
        l[...] = jnp.zeros_like(l)
        acc[...] = jnp.zeros_like(acc)
        lhs[0:nkd, :] = wukt_ref[...]
        qn = (q_ref[:, :QK_NOPE] * gkn_ref[...]).astype(BF16)
        qabs = [_dot_nt(qn[h * dq:(h + 1) * dq], wuk_ref[h]) for h in range(N_HEADS)]
        lhs[nkd:nkd + nhq, :] = jnp.concatenate(qabs, axis=0).astype(BF16)

    def attend(c_pages, krt_pages, mask):
        ct = jnp.concatenate([c.T for c in c_pages], axis=1).astype(BF16)
        cb = jnp.concatenate(c_pages, axis=0).astype(BF16)
        kt = jnp.concatenate(krt_pages, axis=1).astype(BF16)
        nk = ct.shape[1]
        r = _dot(lhs[...], ct)
        rinv = []
        for h in range(N_HEADS):
            kn = r[h * QK_NOPE:(h + 1) * QK_NOPE]
            ssq = jnp.sum(kn * kn, axis=0, keepdims=True)
            rinv.append(jnp.broadcast_to(lax.rsqrt(ssq * (1.0 / QK_NOPE) + NORM_EPS), (dq, nk)))
        qr = q_ref[:, QK_NOPE:QK_NOPE + QK_ROPE].astype(BF16)
        s = (r[nkd:] * jnp.concatenate(rinv, axis=0) + _dot(qr, kt)) * ATTN_SCALE
        if mask is not None:
            s = jnp.where(mask, s, NEG_INF)
        m_new = jnp.maximum(m[...], jnp.max(s, axis=-1, keepdims=True))
        corr = jnp.exp(m[...] - m_new)
        p = jnp.exp(s - m_new)
        l[...] = l[...] * corr + jnp.sum(p, axis=-1, keepdims=True)
        acc[...] = acc[...] * corr + _dot(p.astype(BF16), cb)
        m[...] = m_new

    @pl.when(kc < n_chunks)
    def _():
        @pl.when((b == 0) & (kc == 0))
        def _():
            for c in page_copies(b, kc, slot):
                c.start()

        last_chunk = kc + 1 == n_chunks
        nxt_b = jnp.where(last_chunk, b + 1, b)
        nxt_c = jnp.where(last_chunk, 0, kc + 1)

        @pl.when(nxt_b < n_b)
        def _():
            for c in page_copies(nxt_b, nxt_c, 1 - slot):
                c.start()

        for c in page_copies(b, kc, slot):
            c.wait()
        attend([lat_buf[slot, j] for j in range(npg)], [krt_buf[slot, j] for j in range(npg)], None)

    @pl.when(kc == n_chunks)
    def _():
        key = lax.broadcasted_iota(I32, (nhq, PAGE), 1)
        qpos = lax.broadcasted_iota(I32, (nhq, PAGE), 0) % dq
        attend([cnew[...]], [krtnew[...]], key <= qpos)
        o_ref[...] = (acc[...] / l[...]).astype(o_ref.dtype)


def _mla_sample_attn(q_s, c_new, kr_new, cache_latent, cache_krope, page_table, wuk, gkn, n_b, dq):
    n_pages = page_table.shape[1]
    npg = math.gcd(n_pages, PAGES_PER_STEP)
    n_chunks = n_pages // npg
    nhq = N_HEADS * dq
    nkd = N_HEADS * QK_NOPE
    wukt = wuk.transpose(0, 2, 1).reshape(nkd, KV_RANK)
    pad = lambda a: jnp.concatenate([a, jnp.zeros((n_b, PAGE - dq, a.shape[-1]), a.dtype)], axis=1)
    krt_cache = cache_krope.swapaxes(1, 2)
    krt_new = pad(kr_new).swapaxes(1, 2)
    per_b = lambda r, w: pl.BlockSpec((None, r, w), lambda b, kc, pt: (b, 0, 0))
    grid_spec = pltpu.PrefetchScalarGridSpec(
        num_scalar_prefetch=1, grid=(n_b, n_chunks + 1),
        in_specs=[pl.BlockSpec(memory_space=pl.ANY), pl.BlockSpec(memory_space=pl.ANY),
                  per_b(PAGE, KV_RANK), per_b(QK_ROPE, PAGE), per_b(nhq, 256),
                  pl.BlockSpec((N_HEADS, KV_RANK, QK_NOPE), lambda b, kc, pt: (0, 0, 0)),
                  pl.BlockSpec((nkd, KV_RANK), lambda b, kc, pt: (0, 0)),
                  pl.BlockSpec((1, QK_NOPE), lambda b, kc, pt: (0, 0))],
        out_specs=per_b(nhq, KV_RANK),
        scratch_shapes=[pltpu.VMEM((2, npg, PAGE, KV_RANK), F32), pltpu.VMEM((2, npg, QK_ROPE, PAGE), F32),
                        pltpu.SemaphoreType.DMA((2, 2)), pltpu.VMEM((nkd + nhq, KV_RANK), BF16),
                        pltpu.VMEM((nhq, 1), F32), pltpu.VMEM((nhq, 1), F32), pltpu.VMEM((nhq, KV_RANK), F32)])
    return pl.pallas_call(
        functools.partial(_sample_attn_kernel, n_b=n_b, npg=npg, n_chunks=n_chunks, dq=dq), grid_spec=grid_spec,
        out_shape=jax.ShapeDtypeStruct((n_b, nhq, KV_RANK), BF16),
        compiler_params=_cparams(("arbitrary", "arbitrary")), name="mla_sample_attn",
    )(page_table, cache_latent, krt_cache, pad(c_new), krt_new, q_s,
      wuk, wukt, gkn.reshape(1, QK_NOPE).astype(F32))


def _mla_out_kernel(op_ref, os_ref, x_ref, wuv_ref, wo_ref, out_ref, *, n0):
    first = pl.program_id(0) < n0
    v = jnp.concatenate([_dot(jnp.where(first, op_ref[h], os_ref[h]), wuv_ref[h]) for h in range(N_HEADS)], axis=1)
    out_ref[...] = x_ref[...] + _dot(v.astype(BF16), wo_ref[...])


def _mla_out(o_p, o_s, x, w_uv, w_o, tm=512):
    t = x.shape[0]
    n0 = o_p.shape[1] // tm
    return pl.pallas_call(
        functools.partial(_mla_out_kernel, n0=n0), grid=(t // tm,),
        in_specs=[pl.BlockSpec((N_HEADS, tm, KV_RANK), lambda i: (0, jnp.minimum(i, n0 - 1), 0)),
                  pl.BlockSpec((N_HEADS, tm, KV_RANK), lambda i: (0, jnp.maximum(i - n0, 0), 0)),
                  pl.BlockSpec((tm, D), lambda i: (i, 0)),
                  pl.BlockSpec((N_HEADS, KV_RANK, 128), lambda i: (0, 0, 0)), pl.BlockSpec((D, D), lambda i: (0, 0))],
        out_specs=pl.BlockSpec((tm, D), lambda i: (i, 0)),
        out_shape=jax.ShapeDtypeStruct((t, D), F32),
        compiler_params=_cparams(("parallel",)), name="mla_out",
    )(o_p, o_s, x, w_uv.transpose(1, 0, 2).astype(BF16), w_o.astype(BF16))


def _mla_layer(x, norm_g, cache_latent, cache_krope, page_table, params, dims):
    n_bp, seq_p, n_bs, seq_s = dims
    tp = n_bp * seq_p
    past_len = page_table.shape[1] * cache_latent.shape[1]
    w_uk, w_uv, kn_nope, w_o = params[4], params[5], params[8], params[10]
    c_p, c_s, kr_p, kr_s, cbt, qcat, kcat = _mla_proj(x, norm_g, params, dims, past_len)
    c_s = c_s.reshape(n_bs, seq_s, KV_RANK)
    kr_s = kr_s[:, :QK_ROPE].reshape(n_bs, seq_s, QK_ROPE)
    o_p = _mla_prompt_attn(qcat, kcat, cbt, n_bp, seq_p)
    q_s = qcat[:, tp:].astype(F32).reshape(N_HEADS, n_bs, seq_s, 256).transpose(1, 0, 2, 3)
    q_s = q_s.reshape(n_bs, N_HEADS * seq_s, 256)
    o_s = _mla_sample_attn(q_s, c_s, kr_s, cache_latent, cache_krope, page_table,
                           w_uk.transpose(1, 0, 2).astype(BF16), kn_nope, n_bs, seq_s)
    o_s = o_s.reshape(n_bs, N_HEADS, seq_s, KV_RANK).transpose(1, 0, 2, 3).reshape(N_HEADS, n_bs * seq_s, KV_RANK)
    x = _mla_out(o_p, o_s, x, w_uv, w_o)
    state = (c_p.reshape(n_bp, seq_p, KV_RANK), c_s, kr_p[:, :QK_ROPE].reshape(n_bp, seq_p, QK_ROPE), kr_s)
    return x, state


def _pool_kernel(x_ref, prev_ref, gn_ref, w_ref, sc_ref, o_ref, st_ref, ext, *, tl, start, last):
    ti = pl.program_id(1)

    @pl.when(ti == 0)
    def _():
        ext[0:POOL_HALO, :] = prev_ref[...]

    @pl.when(ti > 0)
    def _():
        ext[0:POOL_HALO, :] = ext[tl:tl + POOL_HALO, :]

    x = x_ref[...]
    h = _rms(x, gn_ref[...])
    ext[POOL_HALO:POOL_HALO + tl, :] = h
    pos = start + ti * tl + lax.broadcasted_iota(I32, (tl, 1), 0)
    run = ext[...]
    z = []
    for g, w in enumerate(POOL_WINDOWS):
        run = run[:, (POOL_GROUP if g else 0):]
        run = run + pltpu.roll(run, w // 2, 0)
        total = run[POOL_HALO:POOL_HALO + tl, :POOL_GROUP]
        count = jnp.minimum(w, pos + 1).astype(F32)
        p = total / count - h[:, g * POOL_GROUP:(g + 1) * POOL_GROUP]
        z.append(_dot(p.astype(BF16), w_ref[g]))
    o_ref[...] = x + jnp.concatenate(z, axis=1) * sc_ref[...]

    @pl.when(ti == last)
    def _():
        st_ref[...] = ext[tl:tl + POOL_HALO, :]


def _pool_stream(x, prev, norm_g, pool_w, pool_scale, n_b, seq, row0, start, tl):
    nt = seq // tl
    blk0 = row0 // tl
    prev = jnp.concatenate([jnp.zeros((n_b, 1, D), F32), prev.astype(F32)], axis=1)
    rows = lambda b, t: (blk0 + b * nt + t, 0)
    out, st = pl.pallas_call(
        functools.partial(_pool_kernel, tl=tl, start=start, last=nt - 1), grid=(n_b, nt),
        in_specs=[pl.BlockSpec((tl, D), rows),
                  pl.BlockSpec((None, POOL_HALO, D), lambda b, t: (b, 0, 0)),
                  pl.BlockSpec((1, D), lambda b, t: (0, 0)),
                  pl.BlockSpec((4, POOL_GROUP, POOL_GROUP), lambda b, t: (0, 0, 0)),
                  pl.BlockSpec((1, D), lambda b, t: (0, 0))],
        out_specs=[pl.BlockSpec((tl, D), rows),
                   pl.BlockSpec((None, POOL_HALO, D), lambda b, t: (b, 0, 0))],
        out_shape=[jax.ShapeDtypeStruct(x.shape, F32), jax.ShapeDtypeStruct((n_b, POOL_HALO, D), F32)],
        scratch_shapes=[pltpu.VMEM((POOL_HALO + tl, D), F32)],
        input_output_aliases={0: 0},
        compiler_params=_cparams(("parallel", "arbitrary")), name="pool_mixer",
    )(x, prev, norm_g.reshape(1, D), pool_w.astype(BF16), pool_scale.reshape(1, D))
    return out, st[:, 1:]


def _pool_layer(x, norm_g, state_pool, params, dims, past_len):
    n_bp, seq_p, n_bs, seq_s = dims
    tp = n_bp * seq_p
    pool_w, pool_scale = params
    zero = jnp.zeros((n_bp, POOL_HALO - 1, D), F32)
    x, st_p = _pool_stream(x, zero, norm_g, pool_w, pool_scale, n_bp, seq_p, 0, 0, min(512, seq_p))
    x, st_s = _pool_stream(x, state_pool, norm_g, pool_w, pool_scale, n_bs, seq_s, tp, past_len, seq_s)
    return x, (st_p, st_s)


def _router_kernel(x_ref, gn_ref, wr_ref, tri_ref, hb_ref, sel_ref, gate_ref, cnt_ref):
    h = _rms(x_ref[...], gn_ref[...])
    hb_ref[...] = h.astype(BF16)
    logits = lax.dot_general(wr_ref[...], h, (((1,), (1,)), ((), ())), precision=lax.Precision.HIGHEST,
                             preferred_element_type=F32)
    idx = lax.broadcasted_iota(I32, logits.shape, 0)
    m1 = jnp.max(logits, axis=0, keepdims=True)
    i1 = jnp.min(jnp.where(logits == m1, idx, N_EXPERTS), axis=0, keepdims=True)
    first = idx == i1
    rest = jnp.where(first, -jnp.inf, logits)
    m2 = jnp.max(rest, axis=0, keepdims=True)
    i2 = jnp.min(jnp.where(rest == m2, idx, N_EXPERTS), axis=0, keepdims=True)
    second = idx == i2
    e2 = jnp.exp(m2 - m1)
    g1 = 1.0 / (1.0 + e2)
    g2 = e2 / (1.0 + e2)
    chosen = jnp.where(first | second, 1.0, 0.0)
    rank = _dot(chosen.astype(BF16), tri_ref[...])
    rank1 = jnp.sum(jnp.where(first, rank, 0.0), axis=0, keepdims=True)
    rank2 = jnp.sum(jnp.where(second, rank, 0.0), axis=0, keepdims=True)
    zero = jnp.zeros((N_EXPERTS - 4,) + i1.shape[1:], I32)
    sel_ref[...] = jnp.concatenate([i1, i2, rank1.astype(I32), rank2.astype(I32), zero], axis=0)
    gate_ref[...] = jnp.concatenate([g1, g2, jnp.zeros((N_EXPERTS - 2,) + g1.shape[1:], F32)], axis=0)
    cnt_ref[...] = jnp.broadcast_to(jnp.sum(chosen, axis=1, keepdims=True), cnt_ref.shape).astype(I32)


def _moe_route(x, norm_g, w_router):
    t = x.shape[0]
    tb = MOE_TB
    nblk = t // tb
    tri = (jnp.arange(tb)[:, None] < jnp.arange(tb)[None, :]).astype(BF16)
    blk = lambda dt: (pl.BlockSpec((None, N_EXPERTS, tb), lambda i: (i, 0, 0)),
                      jax.ShapeDtypeStruct((nblk, N_EXPERTS, tb), dt))
    (sel_spec, sel_shape), (gate_spec, gate_shape) = blk(I32), blk(F32)
    return pl.pallas_call(
        _router_kernel, grid=(nblk,),
        in_specs=[pl.BlockSpec((tb, D), lambda i: (i, 0)), pl.BlockSpec((1, D), lambda i: (0, 0)),
                  pl.BlockSpec((N_EXPERTS, D), lambda i: (0, 0)), pl.BlockSpec((tb, tb), lambda i: (0, 0))],
        out_specs=[pl.BlockSpec((tb, D), lambda i: (i, 0)), sel_spec, gate_spec,
                   pl.BlockSpec((None, N_EXPERTS, 128), lambda i: (i, 0, 0))],
        out_shape=[jax.ShapeDtypeStruct((t, D), BF16), sel_shape, gate_shape,
                   jax.ShapeDtypeStruct((nblk, N_EXPERTS, 128), I32)],
        compiler_params=_cparams(("parallel",)), name="moe_router",
    )(x, norm_g.reshape(1, D), w_router.T.astype(F32), tri)


def _moe_rows(t):
    nblk = t // MOE_TB
    worst = 2 * t + nblk * N_EXPERTS * (MOE_ALIGN - 1) + N_EXPERTS * (MOE_TM - 1)
    return (worst + MOE_TM - 1) // MOE_TM * MOE_TM


def _moe_plan(sel, counts, t):
    cnt = counts[:, :, 0]
    cpad = (cnt + MOE_ALIGN - 1) // MOE_ALIGN * MOE_ALIGN
    per_e = jnp.sum(cpad, axis=0)
    per_e_pad = (per_e + MOE_TM - 1) // MOE_TM * MOE_TM
    e_start = jnp.cumsum(per_e_pad) - per_e_pad
    off = e_start[None, :] + jnp.cumsum(cpad, axis=0) - cpad
    lend = jnp.cumsum(cpad, axis=1)
    lstart = lend - cpad
    e1, e2, r1, r2 = sel[:, 0], sel[:, 1], sel[:, 2], sel[:, 3]
    experts = jnp.arange(N_EXPERTS)
    pick = lambda tab, e: jnp.sum(jnp.where(e[:, :, None] == experts, tab[:, None, :], 0), axis=2)
    dstl = jnp.stack([pick(lstart, e1) + r1, pick(lstart, e2) + r2], axis=1).astype(I32)
    prow = jnp.arange(MOE_PIECES)[None, :] * MOE_ALIGN
    pe = jnp.minimum(jnp.sum(prow[:, :, None] >= lend[:, None, :], axis=2), N_EXPERTS - 1)
    grow = pick(off, pe) + prow - pick(lstart, pe)
    n_pieces = lend[:, -1] // MOE_ALIGN
    grow = jnp.where(jnp.arange(MOE_PIECES)[None, :] < n_pieces[:, None], grow, 0)
    n_tiles = _moe_rows(t) // MOE_TM
    e_end = e_start + per_e_pad
    tile_e = jnp.sum((jnp.arange(n_tiles) * MOE_TM)[:, None] >= e_end[None, :], axis=1)
    active = (tile_e < N_EXPERTS).astype(I32)
    tile_e = jnp.where(active == 1, tile_e, N_EXPERTS - 1)
    return dict(dstl=dstl, grow=grow.astype(I32).reshape(-1), n_pieces=n_pieces.astype(I32),
                tile_e=tile_e.astype(I32), active=active)


def _dispatch_kernel(np_ref, grow_ref, hb_ref, dst_ref, gate_ref, xs_in, xs_ref, buf, sem):
    del xs_in
    b = pl.program_id(0)
    nblk = pl.num_programs(0)
    slot = b % 2

    def copy(blk, i, s):
        r = pl.multiple_of(grow_ref[blk * MOE_PIECES + i], MOE_ALIGN)
        src = buf.at[s, pl.ds(pl.multiple_of(i * MOE_ALIGN, MOE_ALIGN), MOE_ALIGN), :]
        return pltpu.make_async_copy(src, xs_ref.at[pl.ds(r, MOE_ALIGN), :], sem.at[s])

    def drain(blk, s):
        lax.fori_loop(0, np_ref[blk], lambda i, c: (copy(blk, i, s).wait(), c)[1], 0)

    @pl.when(b >= 2)
    def _():
        drain(b - 2, slot)

    want = lax.broadcasted_iota(I32, (MOE_KL, 1), 0)
    hit1 = dst_ref[0:1, :] == want
    hit2 = dst_ref[1:2, :] == want
    onehot = jnp.where(hit1 | hit2, 1.0, 0.0).astype(BF16)
    buf[slot, :, 0:D] = _dot(onehot, hb_ref[...]).astype(BF16)
    gate = jnp.sum(jnp.where(hit1, gate_ref[0:1, :], 0.0) + jnp.where(hit2, gate_ref[1:2, :], 0.0),
                   axis=1, keepdims=True)
    g_hi = gate.astype(BF16).astype(F32)
    g_mid = (gate - g_hi).astype(BF16).astype(F32)
    g_lo = gate - g_hi - g_mid
    lane = lax.broadcasted_iota(I32, (MOE_KL, 128), 1)
    slab = jnp.where(lane == 0, g_hi, jnp.where(lane == 1, g_mid, jnp.where(lane == 2, g_lo, 0.0)))
    buf[slot, :, D:MOE_XW] = slab.astype(BF16)
    lax.fori_loop(0, np_ref[b], lambda i, c: (copy(b, i, slot).start(), c)[1], 0)

    @pl.when(b == nblk - 1)
    def _():
        @pl.when(b >= 1)
        def _():
            drain(b - 1, 1 - slot)
        drain(b, slot)


def _moe_dispatch(hb, gate, plan):
    t = hb.shape[0]
    nblk = t // MOE_TB
    rows = _moe_rows(t)
    grid_spec = pltpu.PrefetchScalarGridSpec(
        num_scalar_prefetch=2, grid=(nblk,),
        in_specs=[pl.BlockSpec((MOE_TB, D), lambda b, *_: (b, 0)),
                  pl.BlockSpec((None, 2, MOE_TB), lambda b, *_: (b, 0, 0)),
                  pl.BlockSpec((None, N_EXPERTS, MOE_TB), lambda b, *_: (b, 0, 0)),
                  pl.BlockSpec(memory_space=pl.ANY)],
        out_specs=pl.BlockSpec(memory_space=pl.ANY),
        scratch_shapes=[pltpu.VMEM((2, MOE_KL, MOE_XW), BF16), pltpu.SemaphoreType.DMA((2,))])
    return pl.pallas_call(
        _dispatch_kernel, grid_spec=grid_spec,
        out_shape=jax.ShapeDtypeStruct((rows, MOE_XW), BF16),
        input_output_aliases={5: 0},
        compiler_params=_cparams(("arbitrary",)), name="moe_dispatch",
    )(plan["n_pieces"], plan["grow"], hb, plan["dstl"], gate, jnp.zeros((rows, MOE_XW), BF16))


def _expert_kernel(te_ref, act_ref, x_ref, wg_ref, wu_ref, wo_ref, o_ref, acc, *, nf):
    i, f = pl.program_id(0), pl.program_id(1)

    @pl.when(f == 0)
    def _():
        acc[...] = jnp.zeros_like(acc)

    @pl.when(act_ref[i] == 1)
    def _():
        x = x_ref[:, 0:D]
        t = (jax.nn.silu(_dot(x, wg_ref[...])) * _dot(x, wu_ref[...])).astype(BF16)
        acc[...] += _dot(t, wo_ref[...].astype(BF16))

    @pl.when(f == nf - 1)
    def _():
        gate = jnp.sum(x_ref[:, D:MOE_XW].astype(F32), axis=1, keepdims=True)
        o_ref[...] = (acc[...] * gate).astype(o_ref.dtype)


def _moe_experts(xs, w_in, w_out, plan, tf=896):
    rows = xs.shape[0]
    d_ff = w_out.shape[1]
    nf = d_ff // tf
    fe = lambda i, f, act: jnp.where(act[i] == 1, f, nf - 1)
    grid_spec = pltpu.PrefetchScalarGridSpec(
        num_scalar_prefetch=2, grid=(rows // MOE_TM, nf),
        in_specs=[pl.BlockSpec((MOE_TM, MOE_XW), lambda i, f, te, act: (i, 0)),
                  pl.BlockSpec((None, D, tf), lambda i, f, te, act: (te[i], 0, fe(i, f, act))),
                  pl.BlockSpec((None, D, tf), lambda i, f, te, act: (te[i], 0, fe(i, f, act) + nf)),
                  pl.BlockSpec((None, tf, D), lambda i, f, te, act: (te[i], fe(i, f, act), 0))],
        out_specs=pl.BlockSpec((MOE_TM, D), lambda i, f, te, act: (i, 0)),
        scratch_shapes=[pltpu.VMEM((MOE_TM, D), F32)])
    return pl.pallas_call(
        functools.partial(_expert_kernel, nf=nf), grid_spec=grid_spec,
        out_shape=jax.ShapeDtypeStruct((rows, D), BF16),
        compiler_params=_cparams(("parallel", "arbitrary")), name="moe_experts",
    )(plan["tile_e"], plan["active"], xs, w_in, w_in, w_out)


def _combine_kernel(np_ref, grow_ref, x_ref, dst_ref, ys_ref, *rest, n0):
    outs, (buf, sem) = rest[:-2], rest[-2:]
    b = pl.program_id(0)
    nblk = pl.num_programs(0)
    slot = b % 2

    def copy(blk, i, s):
        r = pl.multiple_of(grow_ref[blk * MOE_PIECES + i], MOE_ALIGN)
        dst = buf.at[s, pl.ds(pl.multiple_of(i * MOE_ALIGN, MOE_ALIGN), MOE_ALIGN), :]
        return pltpu.make_async_copy(ys_ref.at[pl.ds(r, MOE_ALIGN), :], dst, sem.at[s])

    def fetch(blk, s):
        lax.fori_loop(0, np_ref[blk], lambda i, c: (copy(blk, i, s).start(), c)[1], 0)

    @pl.when(b == 0)
    def _():
        buf[...] = jnp.zeros_like(buf)
        fetch(0, 0)

    @pl.when(b + 1 < nblk)
    def _():
        fetch(b + 1, 1 - slot)

    want = lax.broadcasted_iota(I32, (1, MOE_KL), 1)
    onehot = jnp.where((dst_ref[:, 0:1] == want) | (dst_ref[:, 1:2] == want), 1.0, 0.0).astype(BF16)
    lax.fori_loop(0, np_ref[b], lambda i, c: (copy(b, i, slot).wait(), c)[1], 0)
    y = x_ref[...] + _dot(onehot, buf[slot])
    if len(outs) == 1:
        outs[0][...] = y
    else:
        @pl.when(b < n0)
        def _():
            outs[0][...] = y

        @pl.when(b >= n0)
        def _():
            outs[1][...] = y


def _moe_combine(x, ys, dst_tok, plan, split_rows):
    t = x.shape[0]
    nblk = t // MOE_TB
    tile = lambda m: pl.BlockSpec((MOE_TB, D), m)
    if split_rows is None:
        n0 = nblk
        out_specs = tile(lambda b, *_: (b, 0))
        out_shape = jax.ShapeDtypeStruct((t, D), F32)
    else:
        n0 = split_rows // MOE_TB
        out_specs = [tile(lambda b, *_: (jnp.minimum(b, n0 - 1), 0)), tile(lambda b, *_: (jnp.maximum(b - n0, 0), 0))]
        out_shape = [jax.ShapeDtypeStruct((split_rows, D), F32), jax.ShapeDtypeStruct((t - split_rows, D), F32)]
    grid_spec = pltpu.PrefetchScalarGridSpec(
        num_scalar_prefetch=2, grid=(nblk,),
        in_specs=[tile(lambda b, *_: (b, 0)),
                  pl.BlockSpec((MOE_TB, 2), lambda b, *_: (b, 0)),
                  pl.BlockSpec(memory_space=pl.ANY)],
        out_specs=out_specs,
        scratch_shapes=[pltpu.VMEM((2, MOE_KL, D), BF16), pltpu.SemaphoreType.DMA((2,))])
    return pl.pallas_call(
        functools.partial(_combine_kernel, n0=n0), grid_spec=grid_spec, out_shape=out_shape,
        compiler_params=_cparams(("arbitrary",)), name="moe_combine",
    )(plan["n_pieces"], plan["grow"], x, dst_tok, ys)


def _moe_layer(x, norm_g, w_router, w_in, w_out, split_rows=None):
    t = x.shape[0]
    hb, sel, gate, counts = _moe_route(x, norm_g, w_router)
    plan = _moe_plan(sel, counts, t)
    xs = _moe_dispatch(hb, gate, plan)
    ys = _moe_experts(xs, w_in.astype(BF16), w_out, plan)
    dst_tok = plan["dstl"].transpose(0, 2, 1).reshape(t, 2)
    return _moe_combine(x, ys, dst_tok, plan, split_rows)


def kernel(x_prompt, x_sample, state_l0_s5_re, state_l0_s5_im, cache_l1_latent, cache_l1_krope, state_l2_pool, state_l3_s5_re, state_l3_s5_im, page_table, l0_norm_mix, l0_s5_lambda_re, l0_s5_lambda_im, l0_s5_log_dt, l0_s5_b_re, l0_s5_b_im, l0_s5_c_re, l0_s5_c_im, l0_s5_d, l0_s5_w_glu, l0_norm_ffn, l0_ffn_w_in, l0_ffn_w_out, l1_norm_mix, l1_mla_w_in, l1_mla_norm_q, l1_mla_norm_kv, l1_mla_w_q, l1_mla_w_uk, l1_mla_w_uv, l1_mla_qn_nope, l1_mla_qn_rope, l1_mla_kn_nope, l1_mla_kn_rope, l1_mla_w_o, l1_norm_ffn, l1_moe_w_router, l1_moe_w_in, l1_moe_w_out, l2_norm_mix, l2_pool_w, l2_pool_scale, l2_norm_ffn, l2_ffn_w_in, l2_ffn_w_out, l3_norm_mix, l3_s5_lambda_re, l3_s5_lambda_im, l3_s5_log_dt, l3_s5_b_re, l3_s5_b_im, l3_s5_c_re, l3_s5_c_im, l3_s5_d, l3_s5_w_glu, l3_norm_ffn, l3_moe_w_router, l3_moe_w_in, l3_moe_w_out):
    n_bp, seq_p, _ = x_prompt.shape
    n_bs, seq_s, _ = x_sample.shape
    dims = (n_bp, seq_p, n_bs, seq_s)
    tp, ts = n_bp * seq_p, n_bs * seq_s
    past_len = page_table.shape[1] * cache_l1_latent.shape[1]

    x, st0 = _s5_layer((x_prompt.reshape(tp, D), x_sample.reshape(ts, D)), l0_norm_mix, state_l0_s5_re, state_l0_s5_im,
                       (l0_s5_lambda_re, l0_s5_lambda_im, l0_s5_log_dt, l0_s5_b_re, l0_s5_b_im,
                        l0_s5_c_re, l0_s5_c_im, l0_s5_d, l0_s5_w_glu), dims)
    x = _ffn_dense(x, l0_norm_ffn, l0_ffn_w_in, l0_ffn_w_out)

    x, st1 = _mla_layer(x, l1_norm_mix, cache_l1_latent, cache_l1_krope, page_table,
                        (l1_mla_w_in, l1_mla_norm_q, l1_mla_norm_kv, l1_mla_w_q, l1_mla_w_uk, l1_mla_w_uv,
                         l1_mla_qn_nope, l1_mla_qn_rope, l1_mla_kn_nope, l1_mla_kn_rope, l1_mla_w_o), dims)
    x = _moe_layer(x, l1_norm_ffn, l1_moe_w_router, l1_moe_w_in, l1_moe_w_out)

    x, st2 = _pool_layer(x, l2_norm_mix, state_l2_pool, (l2_pool_w, l2_pool_scale), dims, past_len)
    x = _ffn_dense(x, l2_norm_ffn, l2_ffn_w_in, l2_ffn_w_out)

    x, st3 = _s5_layer((x,), l3_norm_mix, state_l3_s5_re, state_l3_s5_im,
                       (l3_s5_lambda_re, l3_s5_lambda_im, l3_s5_log_dt, l3_s5_b_re, l3_s5_b_im,
                        l3_s5_c_re, l3_s5_c_im, l3_s5_d, l3_s5_w_glu), dims)
    y_p, y_s = _moe_layer(x, l3_norm_ffn, l3_moe_w_router, l3_moe_w_in, l3_moe_w_out, split_rows=tp)

    return (y_p.reshape(n_bp, seq_p, D), y_s.reshape(n_bs, seq_s, D)) + st0 + st1 + st2 + st3
```

```python
import functools
import math

import jax
import jax.numpy as jnp
from jax import lax
from jax.experimental import pallas as pl
from jax.experimental.pallas import tpu as pltpu

F32 = jnp.float32
BF16 = jnp.bfloat16
I32 = jnp.int32

D = 1024
NORM_EPS = 1e-6
NEG_INF = -1e30
S5_GROUP = 16
S5_STATE = 64
N_GROUPS = D // S5_GROUP
S5_T = 8
S5_BUNDLE = 8
N_BUNDLES = N_GROUPS // S5_BUNDLE
S5_W = S5_T * 128
N_HEADS = 8
QK_NOPE = 128
QK_ROPE = 64
Q_RANK = 512
KV_RANK = 256
ROPE_THETA = 10000.0
ATTN_SCALE = (QK_NOPE + QK_ROPE) ** -0.5
LOG2E = math.log2(math.e)
PAGE = 128
PAGES_PER_STEP = 32
ATTEND_PAGES = 16
POOL_WINDOWS = (2, 4, 8, 16)
POOL_HALO = 16
POOL_GROUP = D // 4
N_EXPERTS = 8
MOE_TB = 512
MOE_TM = 1024
MOE_ALIGN = 16
MOE_KL = (2 * MOE_TB + N_EXPERTS * (MOE_ALIGN - 1) + 127) // 128 * 128
MOE_PIECES = MOE_KL // MOE_ALIGN
MOE_XW = D + 128


def _cparams(sem, vmem_mb=48):
    return pltpu.CompilerParams(dimension_semantics=sem, vmem_limit_bytes=vmem_mb * 1024 * 1024)


def _rms(x, g):
    r = lax.rsqrt(jnp.mean(x * x, axis=-1, keepdims=True) + NORM_EPS)
    return x * r * g


def _dot(a, b):
    return jnp.dot(a, b, preferred_element_type=F32)


def _dot_nt(a, b):
    return lax.dot_general(a, b, (((1,), (1,)), ((), ())), preferred_element_type=F32)


def _lo_map(n):
    return lambda i: (jnp.minimum(i, n - 1), 0)


def _hi_map(n):
    return lambda i: (jnp.maximum(i - n, 0), 0)


def _rmsnorm_kernel(x_ref, g_ref, o_ref):
    o_ref[...] = _rms(x_ref[...], g_ref[...]).astype(o_ref.dtype)


def _rmsnorm(x, g, row0, rows, tm=1024):
    tm = min(tm, rows)
    blk0 = row0 // tm
    return pl.pallas_call(
        _rmsnorm_kernel, grid=(rows // tm,),
        in_specs=[pl.BlockSpec((tm, D), lambda i: (blk0 + i, 0)), pl.BlockSpec((1, D), lambda i: (0, 0))],
        out_specs=pl.BlockSpec((tm, D), lambda i: (i, 0)),
        out_shape=jax.ShapeDtypeStruct((rows, D), F32),
        compiler_params=_cparams(("parallel",)), name="rmsnorm",
    )(x, g.reshape(1, D))


def _s5_weights(lam_re, lam_im, log_dt, b_re, b_im, c_re, c_im, d):
    lr, li = lam_re.astype(F32), lam_im.astype(F32)
    dt = jnp.exp(log_dt.astype(F32))[:, None]
    mag = jnp.exp(lr * dt)
    ar, ai = mag * jnp.cos(li * dt), mag * jnp.sin(li * dt)
    nr, ni = ar - 1.0, ai
    den = lr * lr + li * li
    f_re = ((nr * lr + ni * li) / den)[..., None]
    f_im = ((ni * lr - nr * li) / den)[..., None]
    br, bi = b_re.astype(F32), b_im.astype(F32)
    bb_re, bb_im = f_re * br - f_im * bi, f_re * bi + f_im * br
    cr, ci = c_re.astype(F32), c_im.astype(F32)

    pr, pi = [jnp.ones_like(ar)], [jnp.zeros_like(ar)]
    for _ in range(S5_T):
        pr.append(pr[-1] * ar - pi[-1] * ai)
        pi.append(pr[-2] * ai + pi[-1] * ar)
    pr, pi = jnp.stack(pr), jnp.stack(pi)

    ab_re = pr[:S5_T, :, :, None] * bb_re - pi[:S5_T, :, :, None] * bb_im
    ab_im = pr[:S5_T, :, :, None] * bb_im + pi[:S5_T, :, :, None] * bb_re
    hi = lax.Precision.HIGHEST
    kern = (jnp.einsum("gcp,jgpd->jgcd", cr, ab_re, precision=hi)
            - jnp.einsum("gcp,jgpd->jgcd", ci, ab_im, precision=hi))

    nb, bg, t = N_BUNDLES, S5_BUNDLE, S5_T
    tt = jnp.arange(t)
    lag = tt[None, :] - tt[:, None]
    ksel = jnp.where((lag >= 0)[:, :, None, None, None], kern[jnp.clip(lag, 0, t - 1)], 0.0)
    ksel = ksel.reshape(t, t, nb, bg, S5_GROUP, S5_GROUP).transpose(2, 0, 3, 5, 1, 4)
    m = ksel.astype(BF16).reshape(nb, t * 128, 128)

    inj = jnp.stack([ab_re[::-1], ab_im[::-1]])
    inj = inj.reshape(2, t, nb, bg, S5_STATE, S5_GROUP).transpose(2, 1, 3, 5, 0, 4)
    we = inj.astype(BF16).reshape(nb, t * 128, 128)

    p1r, p1i = pr[1:, :, None, :], pi[1:, :, None, :]
    co_re = cr[None] * p1r - ci[None] * p1i
    co_im = -(cr[None] * p1i + ci[None] * p1r)
    co = jnp.stack([co_re, co_im]).reshape(2, t, nb, bg, S5_GROUP, S5_STATE).transpose(2, 0, 3, 5, 1, 4)
    ws = co.astype(BF16).reshape(nb, 2 * bg * S5_STATE, 128)

    a_t = jnp.stack([pr[t], pi[t]]).reshape(2, nb, bg * S5_STATE)
    a_t = a_t.transpose(1, 0, 2).reshape(nb, 1, 2 * bg * S5_STATE)
    dv = jnp.broadcast_to(d.astype(F32).reshape(1, nb, 128), (t, nb, 128)).transpose(1, 0, 2).reshape(nb, 1, t * 128)
    return m, we, ws, a_t, dv


def _group_spread(n_outer, inner):
    r = jnp.arange(n_outer * inner)[:, None]
    c = jnp.arange(n_outer * S5_BUNDLE * inner)[None, :]
    return ((r // inner == c // (S5_BUNDLE * inner)) & (r % inner == c % inner)).astype(BF16)


def _widen(compact, spread, row_shift, col_shift):
    wide = _dot(compact, spread)
    row_g = (lax.broadcasted_iota(I32, wide.shape, 0) >> row_shift) & (S5_BUNDLE - 1)
    col_h = (lax.broadcasted_iota(I32, wide.shape, 1) >> col_shift) & (S5_BUNDLE - 1)
    return jnp.where(row_g == col_h, wide, 0.0).astype(BF16)


def _s5_kernel(u_ref, s0_ref, mc_ref, wec_ref, wsc_ref, sp_tc, sp_rp, a_ref, d_ref, g_ref, sfin_ref,
               m_ref, we_ref, ws_ref, state, ubuf, sprev, ebuf, *, nb, cb, last):
    i = pl.program_id(1)
    half = S5_BUNDLE * S5_STATE

    @pl.when(i == 0)
    def _():
        state[...] = s0_ref[...]
        m_ref[...] = _widen(mc_ref[...], sp_tc[...], 4, 4)
        we_ref[...] = _widen(wec_ref[...], sp_rp[...], 4, 6)
        ws_ref[...] = _widen(wsc_ref[...], sp_tc[...], 6, 4)

    for c in range(cb):
        for t in range(S5_T):
            ubuf[c * nb:(c + 1) * nb, t * 128:(t + 1) * 128] = u_ref[:, c * S5_T + t, :]
    u = ubuf[...]
    ub = u.astype(BF16)
    ebuf[...] = _dot(ub, we_ref[...])
    ar, ai = a_ref[:, :half], a_ref[:, half:]
    s = state[...]
    for c in range(cb):
        rows = slice(c * nb, (c + 1) * nb)
        sprev[rows, :] = s
        sr, si = s[:, :half], s[:, half:]
        s = jnp.concatenate([ar * sr - ai * si + ebuf[rows, :half],
                             ar * si + ai * sr + ebuf[rows, half:]], axis=1)
    state[...] = s
    y = _dot(ub, m_ref[...]) + _dot(sprev[...].astype(BF16), ws_ref[...])
    ebuf[...] = jax.nn.gelu(y + d_ref[...] * u)
    for c in range(cb):
        for t in range(S5_T):
            g_ref[:, c * S5_T + t, :] = ebuf[c * nb:(c + 1) * nb, t * 128:(t + 1) * 128]

    @pl.when(i == last)
    def _():
        sfin_ref[...] = s


def _s5_scan(h, s0, tables, cb):
    m, we, ws, a_t, dv = tables
    nb, seq, _ = h.shape
    tok = S5_T * cb
    ni = seq // tok
    r = nb * cb
    w = S5_W
    kern = functools.partial(_s5_kernel, nb=nb, cb=cb, last=ni - 1)
    compact = lambda: pl.BlockSpec((None, w, 128), lambda g, i: (g, 0, 0))
    spread = lambda: pl.BlockSpec((128, w), lambda g, i: (0, 0))
    vec = lambda: pl.BlockSpec((None, 1, w), lambda g, i: (g, 0, 0))
    return pl.pallas_call(
        kern, grid=(N_BUNDLES, ni),
        in_specs=[pl.BlockSpec((nb, tok, 128), lambda g, i: (0, i, g)),
                  pl.BlockSpec((None, nb, w), lambda g, i: (g, 0, 0)),
                  compact(), compact(), compact(), spread(), spread(), vec(), vec()],
        out_specs=[pl.BlockSpec((nb, tok, 128), lambda g, i: (0, i, g)),
                   pl.BlockSpec((None, nb, w), lambda g, i: (g, 0, 0))],
        out_shape=[jax.ShapeDtypeStruct((nb, seq, D), F32),
                   jax.ShapeDtypeStruct((N_BUNDLES, nb, w), F32)],
        scratch_shapes=[pltpu.VMEM((w, w), BF16), pltpu.VMEM((w, w), BF16), pltpu.VMEM((w, w), BF16),
                        pltpu.VMEM((nb, w), F32), pltpu.VMEM((r, w), F32), pltpu.VMEM((r, w), F32),
                        pltpu.VMEM((r, w), F32)],
        compiler_params=_cparams(("parallel", "arbitrary")), name="s5_scan",
    )(h, s0, m, we, ws, _group_spread(S5_T, S5_GROUP), _group_spread(2, S5_STATE), a_t, dv)


def _state_to_bundles(s_re, s_im):
    n_b = s_re.shape[0]
    s = jnp.stack([s_re, s_im], axis=1).reshape(n_b, 2, N_BUNDLES, S5_BUNDLE * S5_STATE)
    return s.transpose(2, 0, 1, 3).reshape(N_BUNDLES, n_b, 2 * S5_BUNDLE * S5_STATE)


def _state_from_bundles(s):
    n_b = s.shape[1]
    s = s.reshape(N_BUNDLES, n_b, 2, S5_BUNDLE, S5_STATE).transpose(2, 1, 0, 3, 4)
    s = s.reshape(2, n_b, N_GROUPS, S5_STATE)
    return s[0], s[1]


def _glu_kernel(*refs, n0, two_x):
    if two_x:
        gp_ref, gs_ref, xp_ref, xs_ref, w_ref, o_ref = refs
    else:
        gp_ref, gs_ref, xp_ref, w_ref, o_ref = refs
    first = pl.program_id(0) < n0
    g = jnp.where(first, gp_ref[...], gs_ref[...])
    x = jnp.where(first, xp_ref[...], xs_ref[...]) if two_x else xp_ref[...]
    z = _dot(g.astype(BF16), w_ref[...])
    o_ref[...] = x + z[:, :D] * jax.nn.sigmoid(z[:, D:])


def _glu_residual(g_p, g_s, x_parts, w_glu, tm=512):
    t = g_p.shape[0] + g_s.shape[0]
    n0 = g_p.shape[0] // tm
    two_x = len(x_parts) == 2
    tile = lambda m: pl.BlockSpec((tm, D), m)
    x_specs = [tile(_lo_map(n0)), tile(_hi_map(n0))] if two_x else [tile(lambda i: (i, 0))]
    return pl.pallas_call(
        functools.partial(_glu_kernel, n0=n0, two_x=two_x), grid=(t // tm,),
        in_specs=[tile(_lo_map(n0)), tile(_hi_map(n0))] + x_specs + [pl.BlockSpec((D, 2 * D), lambda i: (0, 0))],
        out_specs=tile(lambda i: (i, 0)),
        out_shape=jax.ShapeDtypeStruct((t, D), F32),
        compiler_params=_cparams(("parallel",)), name="glu_residual",
    )(g_p, g_s, *x_parts, w_glu)


def _s5_layer(x_parts, norm_g, state_re, state_im, params, dims):
    n_bp, seq_p, n_bs, seq_s = dims
    tp, ts = n_bp * seq_p, n_bs * seq_s
    lam_re, lam_im, log_dt, b_re, b_im, c_re, c_im, d, w_glu = params
    tables = _s5_weights(lam_re, lam_im, log_dt, b_re, b_im, c_re, c_im, d)
    if len(x_parts) == 2:
        h_p = _rmsnorm(x_parts[0], norm_g, 0, tp)
        h_s = _rmsnorm(x_parts[1], norm_g, 0, ts)
    else:
        h_p = _rmsnorm(x_parts[0], norm_g, 0, tp)
        h_s = _rmsnorm(x_parts[0], norm_g, tp, ts)
    zeros = jnp.zeros((N_BUNDLES, n_bp, 2 * S5_BUNDLE * S5_STATE), F32)
    cb = math.gcd(seq_p // S5_T, max(1, 512 // n_bp))
    g_p, s_p = _s5_scan(h_p.reshape(n_bp, seq_p, D), zeros, tables, cb)
    g_s, s_s = _s5_scan(h_s.reshape(n_bs, seq_s, D), _state_to_bundles(state_re, state_im), tables, seq_s // S5_T)
    x = _glu_residual(g_p.reshape(tp, D), g_s.reshape(ts, D), x_parts, w_glu.astype(BF16))
    p_re, p_im = _state_from_bundles(s_p)
    s_re, s_im = _state_from_bundles(s_s)
    return x, (p_re, s_re, p_im, s_im)


def _ffn_kernel(x_ref, gn_ref, wg_ref, wu_ref, wo_ref, o_ref, *, nf):
    x = x_ref[...]
    h = _rms(x, gn_ref[...]).astype(BF16)
    y = x
    for f in range(nf):
        t = (jax.nn.silu(_dot(h, wg_ref[f])) * _dot(h, wu_ref[f])).astype(BF16)
        y = y + _dot(t, wo_ref[f])
    o_ref[...] = y


def _ffn_dense(x, norm_g, w_in, w_out, tm=512, nf=2):
    t = x.shape[0]
    d_ff = w_out.shape[0]
    tf = d_ff // nf
    slabs = lambda w: w.reshape(D, nf, tf).transpose(1, 0, 2).astype(BF16)
    resident = lambda shape: pl.BlockSpec(shape, lambda i: (0, 0, 0), pipeline_mode=pl.Buffered(1))
    return pl.pallas_call(
        functools.partial(_ffn_kernel, nf=nf), grid=(t // tm,),
        in_specs=[pl.BlockSpec((tm, D), lambda i: (i, 0)), pl.BlockSpec((1, D), lambda i: (0, 0)),
                  resident((nf, D, tf)), resident((nf, D, tf)), resident((nf, tf, D))],
        out_specs=pl.BlockSpec((tm, D), lambda i: (i, 0)),
        out_shape=jax.ShapeDtypeStruct((t, D), F32),
        compiler_params=_cparams(("parallel",)), name="ffn_dense",
    )(x, norm_g.reshape(1, D), slabs(w_in[:, :d_ff]), slabs(w_in[:, d_ff:]),
      w_out.astype(BF16).reshape(nf, tf, D))


def _swap_halves(w):
    half = w.shape[-1] // 2
    return jnp.concatenate([w[..., half:], w[..., :half]], axis=-1)


def _pad_lanes(w, n=128):
    return jnp.concatenate([w, jnp.zeros(w.shape[:-1] + (n - w.shape[-1],), w.dtype)], axis=-1)


def _rope_tables(seq_p, seq_s, past_len, rows_s):
    half = QK_ROPE // 2
    inv_freq = ROPE_THETA ** (-jnp.arange(half, dtype=F32) / half)
    pos = jnp.concatenate([jnp.arange(seq_p), past_len + (jnp.arange(rows_s) % seq_s)])
    ang = pos.astype(F32)[:, None] * inv_freq[None, :]
    cos, sin = jnp.cos(ang), jnp.sin(ang)
    return _pad_lanes(jnp.concatenate([cos, cos], axis=1)), _pad_lanes(jnp.concatenate([-sin, sin], axis=1))


def _rope_norm(x, x_sw, g, g_sw, cos, sin):
    r = lax.rsqrt(jnp.sum(x * x, axis=-1, keepdims=True) * (1.0 / QK_ROPE) + NORM_EPS)
    return r * (x * g * cos + x_sw * g_sw * sin)


def _mla_proj_kernel(x_ref, gmix, win, nq, nkv, cos, sin, gkr, gkrs, wq, gqn, gqr, gqrs, wuk, gkn,
                     cp_out, cs_out, krp_out, krs_out, cbt_out, q_out, k_out, *, n0):
    is_prompt = pl.program_id(0) < n0
    hn = _rms(x_ref[...], gmix[...]).astype(BF16)
    a = _dot(hn, win[...])
    hq = _rms(a[:, :Q_RANK], nq[...]).astype(BF16)
    c = _rms(a[:, Q_RANK:Q_RANK + KV_RANK], nkv[...])
    cb = c.astype(BF16)
    kr = _rope_norm(a[:, 768:896], a[:, 896:1024], gkr[...], gkrs[...], cos[...], sin[...])
    krb = kr.astype(BF16)

    @pl.when(is_prompt)
    def _():
        cp_out[...] = c
        krp_out[...] = kr
        cbt_out[...] = c.T.astype(BF16)

    @pl.when(jnp.logical_not(is_prompt))
    def _():
        cs_out[...] = c
        krs_out[...] = kr

    q_scale = jnp.where(is_prompt, ATTN_SCALE * LOG2E, 1.0)
    for h in range(N_HEADS):
        q3 = _dot(hq, wq[:, h * 384:(h + 1) * 384])
        qn = _rms(q3[:, :128], gqn[...])
        qr = _rope_norm(q3[:, 128:256], q3[:, 256:384], gqr[...], gqrs[...], cos[...], sin[...])
        q_out[h] = (jnp.concatenate([qn, qr], axis=1) * q_scale).astype(BF16)

    @pl.when(is_prompt)
    def _():
        for h in range(N_HEADS):
            kn = _rms(_dot(cb, wuk[:, h * QK_NOPE:(h + 1) * QK_NOPE]), gkn[...])
            k_out[h] = jnp.concatenate([kn.astype(BF16), krb], axis=1)


def _mla_proj(x, norm_g, params, dims, past_len, tm=512):
    n_bp, seq_p, n_bs, seq_s = dims
    t = x.shape[0]
    tp = n_bp * seq_p
    ts = t - tp
    w_in, norm_q, norm_kv, w_q, w_uk, w_uv, qn_nope, qn_rope, kn_nope, kn_rope, w_o = params
    w_kr = w_in[:, Q_RANK + KV_RANK:]
    win = jnp.concatenate([w_in[:, :Q_RANK + KV_RANK], _pad_lanes(w_kr), _pad_lanes(_swap_halves(w_kr))],
                          axis=1).astype(BF16)
    wq3 = w_q.reshape(Q_RANK, N_HEADS, QK_NOPE + QK_ROPE)
    wq_r = wq3[:, :, QK_NOPE:]
    wq = jnp.concatenate([wq3[:, :, :QK_NOPE], _pad_lanes(wq_r), _pad_lanes(_swap_halves(wq_r))], axis=2)
    wq = wq.reshape(Q_RANK, N_HEADS * 384).astype(BF16)
    wuk = w_uk.reshape(KV_RANK, N_HEADS * QK_NOPE).astype(BF16)
    cos, sin = _rope_tables(seq_p, seq_s, past_len, min(tm, ts))
    n_pt = seq_p // tm
    n0 = tp // tm
    tab = lambda i: (jnp.where(i < n0, i % n_pt, n_pt), 0)
    row = lambda v: v.reshape(1, -1).astype(F32)
    vec = lambda n: pl.BlockSpec((1, n), lambda i: (0, 0))
    full = lambda a: pl.BlockSpec(a.shape, lambda i: (0, 0))
    lo = lambda n: pl.BlockSpec((tm, n), _lo_map(n0))
    hi = lambda n: pl.BlockSpec((tm, n), _hi_map(n0))
    sds = jax.ShapeDtypeStruct
    return pl.pallas_call(
        functools.partial(_mla_proj_kernel, n0=n0), grid=(t // tm,),
        in_specs=[pl.BlockSpec((tm, D), lambda i: (i, 0)), vec(D), full(win), vec(Q_RANK), vec(KV_RANK),
                  pl.BlockSpec((tm, 128), tab), pl.BlockSpec((tm, 128), tab), vec(128), vec(128),
                  full(wq), vec(128), vec(128), vec(128), full(wuk), vec(128)],
        out_specs=[lo(KV_RANK), hi(KV_RANK), lo(128), hi(128),
                   pl.BlockSpec((KV_RANK, tm), lambda i: (0, jnp.minimum(i, n0 - 1))),
                   pl.BlockSpec((N_HEADS, tm, 256), lambda i: (0, i, 0)),
                   pl.BlockSpec((N_HEADS, tm, 256), lambda i: (0, jnp.minimum(i, n0 - 1), 0))],
        out_shape=[sds((tp, KV_RANK), F32), sds((ts, KV_RANK), F32), sds((tp, 128), F32), sds((ts, 128), F32),
                   sds((KV_RANK, tp), BF16), sds((N_HEADS, t, 256), BF16), sds((N_HEADS, tp, 256), BF16)],
        compiler_params=_cparams(("arbitrary",)), name="mla_proj",
    )(x, row(norm_g), win, row(norm_q), row(norm_kv), cos, sin,
      row(_pad_lanes(kn_rope)), row(_pad_lanes(_swap_halves(kn_rope))), wq, row(qn_nope),
      row(_pad_lanes(qn_rope)), row(_pad_lanes(_swap_halves(qn_rope))), wuk, row(kn_nope))


def _flash_kernel(qi_tab, ki_tab, q_ref, k_ref, vt_ref, o_ref, m, l, acc):
    step = pl.program_id(1)
    qi, ki = qi_tab[step], ki_tab[step]

    @pl.when(ki == 0)
    def _():
        m[...] = jnp.full_like(m, NEG_INF)
        l[...] = jnp.zeros_like(l)
        acc[...] = jnp.zeros_like(acc)

    def update(h, masked):
        s = _dot_nt(k_ref[h], q_ref[h])
        if masked:
            key = lax.broadcasted_iota(I32, s.shape, 0)
            qry = lax.broadcasted_iota(I32, s.shape, 1)
            s = jnp.where(key <= qry, s, NEG_INF)
        m_new = jnp.maximum(m[h], jnp.max(s, axis=0, keepdims=True))
        corr = jnp.exp2(m[h] - m_new)
        p = jnp.exp2(s - m_new)
        l[h] = l[h] * corr + jnp.sum(p, axis=0, keepdims=True)
        acc[h] = acc[h] * corr + _dot(vt_ref[...], p.astype(BF16))
        m[h] = m_new

    def heads(masked):
        def pair(hp, carry):
            update(2 * hp, masked)
            update(2 * hp + 1, masked)
            return carry
        lax.fori_loop(0, N_HEADS // 2, pair, 0)

    @pl.when(ki < qi)
    def _():
        heads(False)

    @pl.when(ki == qi)
    def _():
        heads(True)
        for h in range(N_HEADS):
            o_ref[h] = (acc[h] / l[h]).T.astype(o_ref.dtype)


def _mla_prompt_attn(qcat, kcat, cbt, n_b, seq, tq=512):
    nq = seq // tq
    pairs = [(qi, ki) for qi in range(nq) for ki in range(qi + 1)]
    qi_tab = jnp.array([p[0] for p in pairs], I32)
    ki_tab = jnp.array([p[1] for p in pairs], I32)
    grid_spec = pltpu.PrefetchScalarGridSpec(
        num_scalar_prefetch=2, grid=(n_b, len(pairs)),
        in_specs=[pl.BlockSpec((N_HEADS, tq, 256), lambda b, s, qt, kt: (0, b * nq + qt[s], 0)),
                  pl.BlockSpec((N_HEADS, tq, 256), lambda b, s, qt, kt: (0, b * nq + kt[s], 0)),
                  pl.BlockSpec((KV_RANK, tq), lambda b, s, qt, kt: (0, b * nq + kt[s]))],
        out_specs=pl.BlockSpec((N_HEADS, tq, KV_RANK), lambda b, s, qt, kt: (0, b * nq + qt[s], 0)),
        scratch_shapes=[pltpu.VMEM((N_HEADS, 1, tq), F32), pltpu.VMEM((N_HEADS, 1, tq), F32),
                        pltpu.VMEM((N_HEADS, KV_RANK, tq), F32)])
    return pl.pallas_call(
        _flash_kernel, grid_spec=grid_spec,
        out_shape=jax.ShapeDtypeStruct((N_HEADS, n_b * seq, KV_RANK), BF16),
        compiler_params=_cparams(("parallel", "arbitrary")), name="mla_prompt_attn",
    )(qi_tab, ki_tab, qcat, kcat, cbt)


def _sample_attn_kernel(pt_ref, lat_hbm, krt_hbm, cnew, krtnew, q_ref, wuk_ref, wukt_ref, gkn_ref, o_ref,
                        lat_buf, krt_buf, sem, lhs, m, l, acc, *, n_b, npg, n_chunks, dq):
    b, kc = pl.program_id(0), pl.program_id(1)
    nhq = N_HEADS * dq
    nkd = N_HEADS * QK_NOPE
    slot = (b * n_chunks + kc) % 2

    def page_copies(bb, cc, s):
        copies = []
        for j in range(npg):
            pid = pt_ref[bb, cc * npg + j]
            copies.append(pltpu.make_async_copy(lat_hbm.at[pid], lat_buf.at[s, j], sem.at[0, s]))
            copies.append(pltpu.make_async_copy(krt_hbm.at[pid], krt_buf.at[s, j], sem.at[1, s]))
        return copies

    @pl.when(kc == 0)
    def _():
        m[...] = jnp.full_like(m, NEG_INF)
        l[...] = jnp.zeros_like(l)
        acc[...] = jnp.zeros_like(acc)
        lhs[0:nkd, :] = wukt_ref[...]
        qn = (q_ref[:, :QK_NOPE] * gkn_ref[...]).astype(BF16)
        qabs = [_dot_nt(qn[h * dq:(h + 1) * dq], wuk_ref[h]) for h in range(N_HEADS)]
        lhs[nkd:nkd + nhq, :] = jnp.concatenate(qabs, axis=0).astype(BF16)

    def attend(c_pages, krt_pages, mask):
        ct = jnp.concatenate([c.T for c in c_pages], axis=1).astype(BF16)
        cb = jnp.concatenate(c_pages, axis=0).astype(BF16)
        kt = jnp.concatenate(krt_pages, axis=1).astype(BF16)
        nk = ct.shape[1]
        r = _dot(lhs[...], ct)
        rinv = []
        for h in range(N_HEADS):
            kn = r[h * QK_NOPE:(h + 1) * QK_NOPE]
            ssq = jnp.sum(kn * kn, axis=0, keepdims=True)
            rinv.append(jnp.broadcast_to(lax.rsqrt(ssq * (1.0 / QK_NOPE) + NORM_EPS), (dq, nk)))
        qr = q_ref[:, QK_NOPE:QK_NOPE + QK_ROPE].astype(BF16)
        s = (r[nkd:] * jnp.concatenate(rinv, axis=0) + _dot(qr, kt)) * ATTN_SCALE
        if mask is not None:
            s = jnp.where(mask, s, NEG_INF)
        m_new = jnp.maximum(m[...], jnp.max(s, axis=-1, keepdims=True))
        corr = jnp.exp(m[...] - m_new)
        p = jnp.exp(s - m_new)
        l[...] = l[...] * corr + jnp.sum(p, axis=-1, keepdims=True)
        acc[...] = acc[...] * corr + _dot(p.astype(BF16), cb)
        m[...] = m_new

    @pl.when(kc < n_chunks)
    def _():
        @pl.when((b == 0) & (kc == 0))
        def _():
            for c in page_copies(b, kc, slot):
                c.start()

        last_chunk = kc + 1 == n_chunks
        nxt_b = jnp.where(last_chunk, b + 1, b)
        nxt_c = jnp.where(last_chunk, 0, kc + 1)

        @pl.when(nxt_b < n_b)
        def _():
            for c in page_copies(nxt_b, nxt_c, 1 - slot):
                c.start()

        for c in page_copies(b, kc, slot):
            c.wait()
        group = min(npg, ATTEND_PAGES)
        for j0 in range(0, npg, group):
            pages = range(j0, j0 + group)
            attend([lat_buf[slot, j] for j in pages], [krt_buf[slot, j] for j in pages], None)

    @pl.when(kc == n_chunks)
    def _():
        key = lax.broadcasted_iota(I32, (nhq, PAGE), 1)
        qpos = lax.broadcasted_iota(I32, (nhq, PAGE), 0) % dq
        attend([cnew[...]], [krtnew[...]], key <= qpos)
        o_ref[...] = (acc[...] / l[...]).astype(o_ref.dtype)


def _mla_sample_attn(q_s, c_new, kr_new, cache_latent, cache_krope, page_table, wuk, gkn, n_b, dq):
    n_pages = page_table.shape[1]
    npg = math.gcd(n_pages, PAGES_PER_STEP)
    n_chunks = n_pages // npg
    nhq = N_HEADS * dq
    nkd = N_HEADS * QK_NOPE
    wukt = wuk.transpose(0, 2, 1).reshape(nkd, KV_RANK)
    pad = lambda a: jnp.concatenate([a, jnp.zeros((n_b, PAGE - dq, a.shape[-1]), a.dtype)], axis=1)
    krt_cache = cache_krope.swapaxes(1, 2)
    krt_new = pad(kr_new).swapaxes(1, 2)
    per_b = lambda r, w: pl.BlockSpec((None, r, w), lambda b, kc, pt: (b, 0, 0))
    grid_spec = pltpu.PrefetchScalarGridSpec(
        num_scalar_prefetch=1, grid=(n_b, n_chunks + 1),
        in_specs=[pl.BlockSpec(memory_space=pl.ANY), pl.BlockSpec(memory_space=pl.ANY),
                  per_b(PAGE, KV_RANK), per_b(QK_ROPE, PAGE), per_b(nhq, 256),
                  pl.BlockSpec((N_HEADS, KV_RANK, QK_NOPE), lambda b, kc, pt: (0, 0, 0)),
                  pl.BlockSpec((nkd, KV_RANK), lambda b, kc, pt: (0, 0)),
                  pl.BlockSpec((1, QK_NOPE), lambda b, kc, pt: (0, 0))],
        out_specs=per_b(nhq, KV_RANK),
        scratch_shapes=[pltpu.VMEM((2, npg, PAGE, KV_RANK), F32), pltpu.VMEM((2, npg, QK_ROPE, PAGE), F32),
                        pltpu.SemaphoreType.DMA((2, 2)), pltpu.VMEM((nkd + nhq, KV_RANK), BF16),
                        pltpu.VMEM((nhq, 1), F32), pltpu.VMEM((nhq, 1), F32), pltpu.VMEM((nhq, KV_RANK), F32)])
    return pl.pallas_call(
        functools.partial(_sample_attn_kernel, n_b=n_b, npg=npg, n_chunks=n_chunks, dq=dq), grid_spec=grid_spec,
        out_shape=jax.ShapeDtypeStruct((n_b, nhq, KV_RANK), BF16),
        compiler_params=_cparams(("arbitrary", "arbitrary")), name="mla_sample_attn",
    )(page_table, cache_latent, krt_cache, pad(c_new), krt_new, q_s,
      wuk, wukt, gkn.reshape(1, QK_NOPE).astype(F32))


def _mla_out_kernel(op_ref, os_ref, x_ref, wuv_ref, wo_ref, out_ref, *, n0):
    first = pl.program_id(0) < n0
    v = jnp.concatenate([_dot(jnp.where(first, op_ref[h], os_ref[h]), wuv_ref[h]) for h in range(N_HEADS)], axis=1)
    out_ref[...] = x_ref[...] + _dot(v.astype(BF16), wo_ref[...])


def _mla_out(o_p, o_s, x, w_uv, w_o, tm=512):
    t = x.shape[0]
    n0 = o_p.shape[1] // tm
    return pl.pallas_call(
        functools.partial(_mla_out_kernel, n0=n0), grid=(t // tm,),
        in_specs=[pl.BlockSpec((N_HEADS, tm, KV_RANK), lambda i: (0, jnp.minimum(i, n0 - 1), 0)),
                  pl.BlockSpec((N_HEADS, tm, KV_RANK), lambda i: (0, jnp.maximum(i - n0, 0), 0)),
                  pl.BlockSpec((tm, D), lambda i: (i, 0)),
                  pl.BlockSpec((N_HEADS, KV_RANK, 128), lambda i: (0, 0, 0)), pl.BlockSpec((D, D), lambda i: (0, 0))],
        out_specs=pl.BlockSpec((tm, D), lambda i: (i, 0)),
        out_shape=jax.ShapeDtypeStruct((t, D), F32),
        compiler_params=_cparams(("parallel",)), name="mla_out",
    )(o_p, o_s, x, w_uv.transpose(1, 0, 2).astype(BF16), w_o.astype(BF16))


def _mla_layer(x, norm_g, cache_latent, cache_krope, page_table, params, dims):
    n_bp, seq_p, n_bs, seq_s = dims
    tp = n_bp * seq_p
    past_len = page_table.shape[1] * cache_latent.shape[1]
    w_uk, w_uv, kn_nope, w_o = params[4], params[5], params[8], params[10]
    c_p, c_s, kr_p, kr_s, cbt, qcat, kcat = _mla_proj(x, norm_g, params, dims, past_len)
    c_s = c_s.reshape(n_bs, seq_s, KV_RANK)
    kr_s = kr_s[:, :QK_ROPE].reshape(n_bs, seq_s, QK_ROPE)
    o_p = _mla_prompt_attn(qcat, kcat, cbt, n_bp, seq_p)
    q_s = qcat[:, tp:].astype(F32).reshape(N_HEADS, n_bs, seq_s, 256).transpose(1, 0, 2, 3)
    q_s = q_s.reshape(n_bs, N_HEADS * seq_s, 256)
    o_s = _mla_sample_attn(q_s, c_s, kr_s, cache_latent, cache_krope, page_table,
                           w_uk.transpose(1, 0, 2).astype(BF16), kn_nope, n_bs, seq_s)
    o_s = o_s.reshape(n_bs, N_HEADS, seq_s, KV_RANK).transpose(1, 0, 2, 3).reshape(N_HEADS, n_bs * seq_s, KV_RANK)
    x = _mla_out(o_p, o_s, x, w_uv, w_o)
    state = (c_p.reshape(n_bp, seq_p, KV_RANK), c_s, kr_p[:, :QK_ROPE].reshape(n_bp, seq_p, QK_ROPE), kr_s)
    return x, state


def _pool_kernel(x_ref, prev_ref, gn_ref, w_ref, sc_ref, o_ref, st_ref, ext, *, tl, start, last):
    ti = pl.program_id(1)

    @pl.when(ti == 0)
    def _():
        ext[0:POOL_HALO, :] = prev_ref[...]

    @pl.when(ti > 0)
    def _():
        ext[0:POOL_HALO, :] = ext[tl:tl + POOL_HALO, :]

    x = x_ref[...]
    h = _rms(x, gn_ref[...])
    ext[POOL_HALO:POOL_HALO + tl, :] = h
    pos = start + ti * tl + lax.broadcasted_iota(I32, (tl, 1), 0)
    run = ext[...]
    z = []
    for g, w in enumerate(POOL_WINDOWS):
        run = run[:, (POOL_GROUP if g else 0):]
        run = run + pltpu.roll(run, w // 2, 0)
        total = run[POOL_HALO:POOL_HALO + tl, :POOL_GROUP]
        count = jnp.minimum(w, pos + 1).astype(F32)
        p = total / count - h[:, g * POOL_GROUP:(g + 1) * POOL_GROUP]
        z.append(_dot(p.astype(BF16), w_ref[g]))
    o_ref[...] = x + jnp.concatenate(z, axis=1) * sc_ref[...]

    @pl.when(ti == last)
    def _():
        st_ref[...] = ext[tl:tl + POOL_HALO, :]


def _pool_stream(x, prev, norm_g, pool_w, pool_scale, n_b, seq, row0, start, tl):
    nt = seq // tl
    blk0 = row0 // tl
    prev = jnp.concatenate([jnp.zeros((n_b, 1, D), F32), prev.astype(F32)], axis=1)
    rows = lambda b, t: (blk0 + b * nt + t, 0)
    out, st = pl.pallas_call(
        functools.partial(_pool_kernel, tl=tl, start=start, last=nt - 1), grid=(n_b, nt),
        in_specs=[pl.BlockSpec((tl, D), rows),
                  pl.BlockSpec((None, POOL_HALO, D), lambda b, t: (b, 0, 0)),
                  pl.BlockSpec((1, D), lambda b, t: (0, 0)),
                  pl.BlockSpec((4, POOL_GROUP, POOL_GROUP), lambda b, t: (0, 0, 0)),
                  pl.BlockSpec((1, D), lambda b, t: (0, 0))],
        out_specs=[pl.BlockSpec((tl, D), rows),
                   pl.BlockSpec((None, POOL_HALO, D), lambda b, t: (b, 0, 0))],
        out_shape=[jax.ShapeDtypeStruct(x.shape, F32), jax.ShapeDtypeStruct((n_b, POOL_HALO, D), F32)],
        scratch_shapes=[pltpu.VMEM((POOL_HALO + tl, D), F32)],
        input_output_aliases={0: 0},
        compiler_params=_cparams(("parallel", "arbitrary")), name="pool_mixer",
    )(x, prev, norm_g.reshape(1, D), pool_w.astype(BF16), pool_scale.reshape(1, D))
    return out, st[:, 1:]


def _pool_layer(x, norm_g, state_pool, params, dims, past_len):
    n_bp, seq_p, n_bs, seq_s = dims
    tp = n_bp * seq_p
    pool_w, pool_scale = params
    zero = jnp.zeros((n_bp, POOL_HALO - 1, D), F32)
    x, st_p = _pool_stream(x, zero, norm_g, pool_w, pool_scale, n_bp, seq_p, 0, 0, min(512, seq_p))
    x, st_s = _pool_stream(x, state_pool, norm_g, pool_w, pool_scale, n_bs, seq_s, tp, past_len, seq_s)
    return x, (st_p, st_s)


def _router_kernel(x_ref, gn_ref, wr_ref, tri_ref, hb_ref, sel_ref, gate_ref, cnt_ref):
    h = _rms(x_ref[...], gn_ref[...])
    hb_ref[...] = h.astype(BF16)
    logits = lax.dot_general(wr_ref[...], h, (((1,), (1,)), ((), ())), precision=lax.Precision.HIGHEST,
                             preferred_element_type=F32)
    idx = lax.broadcasted_iota(I32, logits.shape, 0)
    m1 = jnp.max(logits, axis=0, keepdims=True)
    i1 = jnp.min(jnp.where(logits == m1, idx, N_EXPERTS), axis=0, keepdims=True)
    first = idx == i1
    rest = jnp.where(first, -jnp.inf, logits)
    m2 = jnp.max(rest, axis=0, keepdims=True)
    i2 = jnp.min(jnp.where(rest == m2, idx, N_EXPERTS), axis=0, keepdims=True)
    second = idx == i2
    e2 = jnp.exp(m2 - m1)
    g1 = 1.0 / (1.0 + e2)
    g2 = e2 / (1.0 + e2)
    chosen = jnp.where(first | second, 1.0, 0.0)
    rank = _dot(chosen.astype(BF16), tri_ref[...])
    rank1 = jnp.sum(jnp.where(first, rank, 0.0), axis=0, keepdims=True)
    rank2 = jnp.sum(jnp.where(second, rank, 0.0), axis=0, keepdims=True)
    zero = jnp.zeros((N_EXPERTS - 4,) + i1.shape[1:], I32)
    sel_ref[...] = jnp.concatenate([i1, i2, rank1.astype(I32), rank2.astype(I32), zero], axis=0)
    gate_ref[...] = jnp.concatenate([g1, g2, jnp.zeros((N_EXPERTS - 2,) + g1.shape[1:], F32)], axis=0)
    cnt_ref[...] = jnp.broadcast_to(jnp.sum(chosen, axis=1, keepdims=True), cnt_ref.shape).astype(I32)


def _moe_route(x, norm_g, w_router):
    t = x.shape[0]
    tb = MOE_TB
    nblk = t // tb
    tri = (jnp.arange(tb)[:, None] < jnp.arange(tb)[None, :]).astype(BF16)
    blk = lambda dt: (pl.BlockSpec((None, N_EXPERTS, tb), lambda i: (i, 0, 0)),
                      jax.ShapeDtypeStruct((nblk, N_EXPERTS, tb), dt))
    (sel_spec, sel_shape), (gate_spec, gate_shape) = blk(I32), blk(F32)
    return pl.pallas_call(
        _router_kernel, grid=(nblk,),
        in_specs=[pl.BlockSpec((tb, D), lambda i: (i, 0)), pl.BlockSpec((1, D), lambda i: (0, 0)),
                  pl.BlockSpec((N_EXPERTS, D), lambda i: (0, 0)), pl.BlockSpec((tb, tb), lambda i: (0, 0))],
        out_specs=[pl.BlockSpec((tb, D), lambda i: (i, 0)), sel_spec, gate_spec,
                   pl.BlockSpec((None, N_EXPERTS, 128), lambda i: (i, 0, 0))],
        out_shape=[jax.ShapeDtypeStruct((t, D), BF16), sel_shape, gate_shape,
                   jax.ShapeDtypeStruct((nblk, N_EXPERTS, 128), I32)],
        compiler_params=_cparams(("parallel",)), name="moe_router",
    )(x, norm_g.reshape(1, D), w_router.T.astype(F32), tri)


def _moe_rows(t):
    nblk = t // MOE_TB
    worst = 2 * t + nblk * N_EXPERTS * (MOE_ALIGN - 1) + N_EXPERTS * (MOE_TM - 1)
    return (worst + MOE_TM - 1) // MOE_TM * MOE_TM


def _moe_plan(sel, counts, t):
    cnt = counts[:, :, 0]
    cpad = (cnt + MOE_ALIGN - 1) // MOE_ALIGN * MOE_ALIGN
    per_e = jnp.sum(cpad, axis=0)
    per_e_pad = (per_e + MOE_TM - 1) // MOE_TM * MOE_TM
    e_start = jnp.cumsum(per_e_pad) - per_e_pad
    off = e_start[None, :] + jnp.cumsum(cpad, axis=0) - cpad
    lend = jnp.cumsum(cpad, axis=1)
    lstart = lend - cpad
    e1, e2, r1, r2 = sel[:, 0], sel[:, 1], sel[:, 2], sel[:, 3]
    experts = jnp.arange(N_EXPERTS)
    pick = lambda tab, e: jnp.sum(jnp.where(e[:, :, None] == experts, tab[:, None, :], 0), axis=2)
    dstl = jnp.stack([pick(lstart, e1) + r1, pick(lstart, e2) + r2], axis=1).astype(I32)
    prow = jnp.arange(MOE_PIECES)[None, :] * MOE_ALIGN
    pe = jnp.minimum(jnp.sum(prow[:, :, None] >= lend[:, None, :], axis=2), N_EXPERTS - 1)
    grow = pick(off, pe) + prow - pick(lstart, pe)
    n_pieces = lend[:, -1] // MOE_ALIGN
    grow = jnp.where(jnp.arange(MOE_PIECES)[None, :] < n_pieces[:, None], grow, 0)
    n_tiles = _moe_rows(t) // MOE_TM
    e_end = e_start + per_e_pad
    tile_e = jnp.sum((jnp.arange(n_tiles) * MOE_TM)[:, None] >= e_end[None, :], axis=1)
    active = (tile_e < N_EXPERTS).astype(I32)
    tile_e = jnp.where(active == 1, tile_e, N_EXPERTS - 1)
    return dict(dstl=dstl, grow=grow.astype(I32).reshape(-1), n_pieces=n_pieces.astype(I32),
                tile_e=tile_e.astype(I32), active=active)


def _dispatch_kernel(np_ref, grow_ref, hb_ref, dst_ref, gate_ref, xs_in, xs_ref, buf, sem):
    del xs_in
    b = pl.program_id(0)
    nblk = pl.num_programs(0)
    slot = b % 2

    def copy(blk, i, s):
        r = pl.multiple_of(grow_ref[blk * MOE_PIECES + i], MOE_ALIGN)
        src = buf.at[s, pl.ds(pl.multiple_of(i * MOE_ALIGN, MOE_ALIGN), MOE_ALIGN), :]
        return pltpu.make_async_copy(src, xs_ref.at[pl.ds(r, MOE_ALIGN), :], sem.at[s])

    def drain(blk, s):
        lax.fori_loop(0, np_ref[blk], lambda i, c: (copy(blk, i, s).wait(), c)[1], 0)

    @pl.when(b >= 2)
    def _():
        drain(b - 2, slot)

    want = lax.broadcasted_iota(I32, (MOE_KL, 1), 0)
    hit1 = dst_ref[0:1, :] == want
    hit2 = dst_ref[1:2, :] == want
    onehot = jnp.where(hit1 | hit2, 1.0, 0.0).astype(BF16)
    buf[slot, :, 0:D] = _dot(onehot, hb_ref[...]).astype(BF16)
    gate = jnp.sum(jnp.where(hit1, gate_ref[0:1, :], 0.0) + jnp.where(hit2, gate_ref[1:2, :], 0.0),
                   axis=1, keepdims=True)
    g_hi = gate.astype(BF16).astype(F32)
    g_mid = (gate - g_hi).astype(BF16).astype(F32)
    g_lo = gate - g_hi - g_mid
    lane = lax.broadcasted_iota(I32, (MOE_KL, 128), 1)
    slab = jnp.where(lane == 0, g_hi, jnp.where(lane == 1, g_mid, jnp.where(lane == 2, g_lo, 0.0)))
    buf[slot, :, D:MOE_XW] = slab.astype(BF16)
    lax.fori_loop(0, np_ref[b], lambda i, c: (copy(b, i, slot).start(), c)[1], 0)

    @pl.when(b == nblk - 1)
    def _():
        @pl.when(b >= 1)
        def _():
            drain(b - 1, 1 - slot)
        drain(b, slot)


def _moe_dispatch(hb, gate, plan):
    t = hb.shape[0]
    nblk = t // MOE_TB
    rows = _moe_rows(t)
    grid_spec = pltpu.PrefetchScalarGridSpec(
        num_scalar_prefetch=2, grid=(nblk,),
        in_specs=[pl.BlockSpec((MOE_TB, D), lambda b, *_: (b, 0)),
                  pl.BlockSpec((None, 2, MOE_TB), lambda b, *_: (b, 0, 0)),
                  pl.BlockSpec((None, N_EXPERTS, MOE_TB), lambda b, *_: (b, 0, 0)),
                  pl.BlockSpec(memory_space=pl.ANY)],
        out_specs=pl.BlockSpec(memory_space=pl.ANY),
        scratch_shapes=[pltpu.VMEM((2, MOE_KL, MOE_XW), BF16), pltpu.SemaphoreType.DMA((2,))])
    return pl.pallas_call(
        _dispatch_kernel, grid_spec=grid_spec,
        out_shape=jax.ShapeDtypeStruct((rows, MOE_XW), BF16),
        input_output_aliases={5: 0},
        compiler_params=_cparams(("arbitrary",)), name="moe_dispatch",
    )(plan["n_pieces"], plan["grow"], hb, plan["dstl"], gate, jnp.zeros((rows, MOE_XW), BF16))


def _expert_kernel(te_ref, act_ref, x_ref, wg_ref, wu_ref, wo_ref, o_ref, acc, *, nf):
    i, f = pl.program_id(0), pl.program_id(1)

    @pl.when(f == 0)
    def _():
        acc[...] = jnp.zeros_like(acc)

    @pl.when(act_ref[i] == 1)
    def _():
        x = x_ref[:, 0:D]
        t = (jax.nn.silu(_dot(x, wg_ref[...])) * _dot(x, wu_ref[...])).astype(BF16)
        acc[...] += _dot(t, wo_ref[...])

    @pl.when(f == nf - 1)
    def _():
        gate = jnp.sum(x_ref[:, D:MOE_XW].astype(F32), axis=1, keepdims=True)
        o_ref[...] = (acc[...] * gate).astype(o_ref.dtype)


def _moe_experts(xs, w_in, w_out, plan, tf=512):
    rows = xs.shape[0]
    d_ff = w_out.shape[1]
    nf = d_ff // tf
    w_in = w_in.reshape(N_EXPERTS, D, 2, nf, tf).transpose(0, 2, 3, 1, 4).astype(BF16)
    w_out = w_out.astype(BF16)
    fe = lambda i, f, act: jnp.where(act[i] == 1, f, nf - 1)
    slab = lambda half: pl.BlockSpec((None, None, None, D, tf),
                                     lambda i, f, te, act: (te[i], half, fe(i, f, act), 0, 0))
    grid_spec = pltpu.PrefetchScalarGridSpec(
        num_scalar_prefetch=2, grid=(rows // MOE_TM, nf),
        in_specs=[pl.BlockSpec((MOE_TM, MOE_XW), lambda i, f, te, act: (i, 0)),
                  slab(0), slab(1),
                  pl.BlockSpec((None, tf, D), lambda i, f, te, act: (te[i], fe(i, f, act), 0))],
        out_specs=pl.BlockSpec((MOE_TM, D), lambda i, f, te, act: (i, 0)),
        scratch_shapes=[pltpu.VMEM((MOE_TM, D), F32)])
    return pl.pallas_call(
        functools.partial(_expert_kernel, nf=nf), grid_spec=grid_spec,
        out_shape=jax.ShapeDtypeStruct((rows, D), BF16),
        compiler_params=_cparams(("parallel", "arbitrary")), name="moe_experts",
    )(plan["tile_e"], plan["active"], xs, w_in, w_in, w_out)


def _combine_kernel(np_ref, grow_ref, x_ref, dst_ref, ys_ref, *rest, n0):
    outs, (buf, sem) = rest[:-2], rest[-2:]
    b = pl.program_id(0)
    nblk = pl.num_programs(0)
    slot = b % 2

    def copy(blk, i, s):
        r = pl.multiple_of(grow_ref[blk * MOE_PIECES + i], MOE_ALIGN)
        dst = buf.at[s, pl.ds(pl.multiple_of(i * MOE_ALIGN, MOE_ALIGN), MOE_ALIGN), :]
        return pltpu.make_async_copy(ys_ref.at[pl.ds(r, MOE_ALIGN), :], dst, sem.at[s])

    def fetch(blk, s):
        lax.fori_loop(0, np_ref[blk], lambda i, c: (copy(blk, i, s).start(), c)[1], 0)

    @pl.when(b == 0)
    def _():
        buf[...] = jnp.zeros_like(buf)
        fetch(0, 0)

    @pl.when(b + 1 < nblk)
    def _():
        fetch(b + 1, 1 - slot)

    want = lax.broadcasted_iota(I32, (1, MOE_KL), 1)
    onehot = jnp.where((dst_ref[:, 0:1] == want) | (dst_ref[:, 1:2] == want), 1.0, 0.0).astype(BF16)
    lax.fori_loop(0, np_ref[b], lambda i, c: (copy(b, i, slot).wait(), c)[1], 0)
    y = x_ref[...] + _dot(onehot, buf[slot])
    if len(outs) == 1:
        outs[0][...] = y
    else:
        @pl.when(b < n0)
        def _():
            outs[0][...] = y

        @pl.when(b >= n0)
        def _():
            outs[1][...] = y


def _moe_combine(x, ys, dst_tok, plan, split_rows):
    t = x.shape[0]
    nblk = t // MOE_TB
    tile = lambda m: pl.BlockSpec((MOE_TB, D), m)
    if split_rows is None:
        n0 = nblk
        out_specs = tile(lambda b, *_: (b, 0))
        out_shape = jax.ShapeDtypeStruct((t, D), F32)
    else:
        n0 = split_rows // MOE_TB
        out_specs = [tile(lambda b, *_: (jnp.minimum(b, n0 - 1), 0)), tile(lambda b, *_: (jnp.maximum(b - n0, 0), 0))]
        out_shape = [jax.ShapeDtypeStruct((split_rows, D), F32), jax.ShapeDtypeStruct((t - split_rows, D), F32)]
    grid_spec = pltpu.PrefetchScalarGridSpec(
        num_scalar_prefetch=2, grid=(nblk,),
        in_specs=[tile(lambda b, *_: (b, 0)),
                  pl.BlockSpec((MOE_TB, 2), lambda b, *_: (b, 0)),
                  pl.BlockSpec(memory_space=pl.ANY)],
        out_specs=out_specs,
        scratch_shapes=[pltpu.VMEM((2, MOE_KL, D), BF16), pltpu.SemaphoreType.DMA((2,))])
    return pl.pallas_call(
        functools.partial(_combine_kernel, n0=n0), grid_spec=grid_spec, out_shape=out_shape,
        compiler_params=_cparams(("arbitrary",)), name="moe_combine",
    )(plan["n_pieces"], plan["grow"], x, dst_tok, ys)


def _moe_layer(x, norm_g, w_router, w_in, w_out, split_rows=None):
    t = x.shape[0]
    hb, sel, gate, counts = _moe_route(x, norm_g, w_router)
    plan = _moe_plan(sel, counts, t)
    xs = _moe_dispatch(hb, gate, plan)
    ys = _moe_experts(xs, w_in, w_out, plan)
    dst_tok = plan["dstl"].transpose(0, 2, 1).reshape(t, 2)
    return _moe_combine(x, ys, dst_tok, plan, split_rows)


def kernel(x_prompt, x_sample, state_l0_s5_re, state_l0_s5_im, cache_l1_latent, cache_l1_krope, state_l2_pool, state_l3_s5_re, state_l3_s5_im, page_table, l0_norm_mix, l0_s5_lambda_re, l0_s5_lambda_im, l0_s5_log_dt, l0_s5_b_re, l0_s5_b_im, l0_s5_c_re, l0_s5_c_im, l0_s5_d, l0_s5_w_glu, l0_norm_ffn, l0_ffn_w_in, l0_ffn_w_out, l1_norm_mix, l1_mla_w_in, l1_mla_norm_q, l1_mla_norm_kv, l1_mla_w_q, l1_mla_w_uk, l1_mla_w_uv, l1_mla_qn_nope, l1_mla_qn_rope, l1_mla_kn_nope, l1_mla_kn_rope, l1_mla_w_o, l1_norm_ffn, l1_moe_w_router, l1_moe_w_in, l1_moe_w_out, l2_norm_mix, l2_pool_w, l2_pool_scale, l2_norm_ffn, l2_ffn_w_in, l2_ffn_w_out, l3_norm_mix, l3_s5_lambda_re, l3_s5_lambda_im, l3_s5_log_dt, l3_s5_b_re, l3_s5_b_im, l3_s5_c_re, l3_s5_c_im, l3_s5_d, l3_s5_w_glu, l3_norm_ffn, l3_moe_w_router, l3_moe_w_in, l3_moe_w_out):
    n_bp, seq_p, _ = x_prompt.shape
    n_bs, seq_s, _ = x_sample.shape
    dims = (n_bp, seq_p, n_bs, seq_s)
    tp, ts = n_bp * seq_p, n_bs * seq_s
    past_len = page_table.shape[1] * cache_l1_latent.shape[1]

    x, st0 = _s5_layer((x_prompt.reshape(tp, D), x_sample.reshape(ts, D)), l0_norm_mix, state_l0_s5_re, state_l0_s5_im,
                       (l0_s5_lambda_re, l0_s5_lambda_im, l0_s5_log_dt, l0_s5_b_re, l0_s5_b_im,
                        l0_s5_c_re, l0_s5_c_im, l0_s5_d, l0_s5_w_glu), dims)
    x = _ffn_dense(x, l0_norm_ffn, l0_ffn_w_in, l0_ffn_w_out)

    x, st1 = _mla_layer(x, l1_norm_mix, cache_l1_latent, cache_l1_krope, page_table,
                        (l1_mla_w_in, l1_mla_norm_q, l1_mla_norm_kv, l1_mla_w_q, l1_mla_w_uk, l1_mla_w_uv,
                         l1_mla_qn_nope, l1_mla_qn_rope, l1_mla_kn_nope, l1_mla_kn_rope, l1_mla_w_o), dims)
    x = _moe_layer(x, l1_norm_ffn, l1_moe_w_router, l1_moe_w_in, l1_moe_w_out)

    x, st2 = _pool_layer(x, l2_norm_mix, state_l2_pool, (l2_pool_w, l2_pool_scale), dims, past_len)
    x = _ffn_dense(x, l2_norm_ffn, l2_ffn_w_in, l2_ffn_w_out)

    x, st3 = _s5_layer((x,), l3_norm_mix, state_l3_s5_re, state_l3_s5_im,
                       (l3_s5_lambda_re, l3_s5_lambda_im, l3_s5_log_dt, l3_s5_b_re, l3_s5_b_im,
                        l3_s5_c_re, l3_s5_c_im, l3_s5_d, l3_s5_w_glu), dims)
    y_p, y_s = _moe_layer(x, l3_norm_ffn, l3_moe_w_router, l3_moe_w_in, l3_moe_w_out, split_rows=tp)

    return (y_p.reshape(n_bp, seq_p, D), y_s.reshape(n_bs, seq_s, D)) + st0 + st1 + st2 + st3
```

```python
import functools
import math

import jax
import jax.numpy as jnp
from jax import lax
from jax.experimental import pallas as pl
from jax.experimental.pallas import tpu as pltpu

F32 = jnp.float32
BF16 = jnp.bfloat16
I32 = jnp.int32

D = 1024
NORM_EPS = 1e-6
NEG_INF = -1e30
S5_GROUP = 16
S5_STATE = 64
N_GROUPS = D // S5_GROUP
S5_T = 8
S5_BUNDLE = 8
N_BUNDLES = N_GROUPS // S5_BUNDLE
S5_W = S5_T * 128
N_HEADS = 8
QK_NOPE = 128
QK_ROPE = 64
Q_RANK = 512
KV_RANK = 256
ROPE_THETA = 10000.0
ATTN_SCALE = (QK_NOPE + QK_ROPE) ** -0.5
LOG2E = math.log2(math.e)
PAGE = 128
PAGES_PER_STEP = 32
ATTEND_PAGES = 16
POOL_WINDOWS = (2, 4, 8, 16)
POOL_HALO = 16
POOL_GROUP = D // 4
N_EXPERTS = 8
MOE_TB = 512
MOE_TM = 1024
MOE_ALIGN = 16
MOE_KL = (2 * MOE_TB + N_EXPERTS * (MOE_ALIGN - 1) + 127) // 128 * 128
MOE_PIECES = MOE_KL // MOE_ALIGN
MOE_XW = D + 128


def _cparams(sem, vmem_mb=48):
    return pltpu.CompilerParams(dimension_semantics=sem, vmem_limit_bytes=vmem_mb * 1024 * 1024)


def _rms(x, g):
    r = lax.rsqrt(jnp.mean(x * x, axis=-1, keepdims=True) + NORM_EPS)
    return x * r * g


def _dot(a, b):
    return jnp.dot(a, b, preferred_element_type=F32)


def _dot_nt(a, b):
    return lax.dot_general(a, b, (((1,), (1,)), ((), ())), preferred_element_type=F32)


def _lo_map(n):
    return lambda i: (jnp.minimum(i, n - 1), 0)


def _hi_map(n):
    return lambda i: (jnp.maximum(i - n, 0), 0)


def _rmsnorm_kernel(x_ref, g_ref, o_ref):
    o_ref[...] = _rms(x_ref[...], g_ref[...]).astype(o_ref.dtype)


def _rmsnorm(x, g, row0, rows, tm=1024):
    tm = min(tm, rows)
    blk0 = row0 // tm
    return pl.pallas_call(
        _rmsnorm_kernel, grid=(rows // tm,),
        in_specs=[pl.BlockSpec((tm, D), lambda i: (blk0 + i, 0)), pl.BlockSpec((1, D), lambda i: (0, 0))],
        out_specs=pl.BlockSpec((tm, D), lambda i: (i, 0)),
        out_shape=jax.ShapeDtypeStruct((rows, D), F32),
        compiler_params=_cparams(("parallel",)), name="rmsnorm",
    )(x, g.reshape(1, D))


def _s5_weights(lam_re, lam_im, log_dt, b_re, b_im, c_re, c_im, d):
    lr, li = lam_re.astype(F32), lam_im.astype(F32)
    dt = jnp.exp(log_dt.astype(F32))[:, None]
    mag = jnp.exp(lr * dt)
    ar, ai = mag * jnp.cos(li * dt), mag * jnp.sin(li * dt)
    nr, ni = ar - 1.0, ai
    den = lr * lr + li * li
    f_re = ((nr * lr + ni * li) / den)[..., None]
    f_im = ((ni * lr - nr * li) / den)[..., None]
    br, bi = b_re.astype(F32), b_im.astype(F32)
    bb_re, bb_im = f_re * br - f_im * bi, f_re * bi + f_im * br
    cr, ci = c_re.astype(F32), c_im.astype(F32)

    pr, pi = [jnp.ones_like(ar)], [jnp.zeros_like(ar)]
    for _ in range(S5_T):
        pr.append(pr[-1] * ar - pi[-1] * ai)
        pi.append(pr[-2] * ai + pi[-1] * ar)
    pr, pi = jnp.stack(pr), jnp.stack(pi)

    ab_re = pr[:S5_T, :, :, None] * bb_re - pi[:S5_T, :, :, None] * bb_im
    ab_im = pr[:S5_T, :, :, None] * bb_im + pi[:S5_T, :, :, None] * bb_re
    hi = lax.Precision.HIGHEST
    kern = (jnp.einsum("gcp,jgpd->jgcd", cr, ab_re, precision=hi)
            - jnp.einsum("gcp,jgpd->jgcd", ci, ab_im, precision=hi))

    nb, bg, t = N_BUNDLES, S5_BUNDLE, S5_T
    tt = jnp.arange(t)
    lag = tt[None, :] - tt[:, None]
    ksel = jnp.where((lag >= 0)[:, :, None, None, None], kern[jnp.clip(lag, 0, t - 1)], 0.0)
    ksel = ksel.reshape(t, t, nb, bg, S5_GROUP, S5_GROUP).transpose(2, 0, 3, 5, 1, 4)
    m = ksel.astype(BF16).reshape(nb, t * 128, 128)

    inj = jnp.stack([ab_re[::-1], ab_im[::-1]])
    inj = inj.reshape(2, t, nb, bg, S5_STATE, S5_GROUP).transpose(2, 1, 3, 5, 0, 4)
    we = inj.astype(BF16).reshape(nb, t * 128, 128)

    p1r, p1i = pr[1:, :, None, :], pi[1:, :, None, :]
    co_re = cr[None] * p1r - ci[None] * p1i
    co_im = -(cr[None] * p1i + ci[None] * p1r)
    co = jnp.stack([co_re, co_im]).reshape(2, t, nb, bg, S5_GROUP, S5_STATE).transpose(2, 0, 3, 5, 1, 4)
    ws = co.astype(BF16).reshape(nb, 2 * bg * S5_STATE, 128)

    a_t = jnp.stack([pr[t], pi[t]]).reshape(2, nb, bg * S5_STATE)
    a_t = a_t.transpose(1, 0, 2).reshape(nb, 1, 2 * bg * S5_STATE)
    dv = jnp.broadcast_to(d.astype(F32).reshape(1, nb, 128), (t, nb, 128)).transpose(1, 0, 2).reshape(nb, 1, t * 128)
    return m, we, ws, a_t, dv


def _group_spread(n_outer, inner):
    r = jnp.arange(n_outer * inner)[:, None]
    c = jnp.arange(n_outer * S5_BUNDLE * inner)[None, :]
    return ((r // inner == c // (S5_BUNDLE * inner)) & (r % inner == c % inner)).astype(BF16)


def _widen(compact, spread, row_shift, col_shift):
    wide = _dot(compact, spread)
    row_g = (lax.broadcasted_iota(I32, wide.shape, 0) >> row_shift) & (S5_BUNDLE - 1)
    col_h = (lax.broadcasted_iota(I32, wide.shape, 1) >> col_shift) & (S5_BUNDLE - 1)
    return jnp.where(row_g == col_h, wide, 0.0).astype(BF16)


def _s5_kernel(u_ref, s0_ref, mc_ref, wec_ref, wsc_ref, sp_tc, sp_rp, a_ref, d_ref, g_ref, sfin_ref,
               m_ref, we_ref, ws_ref, state, ubuf, sprev, ebuf, *, nb, cb, last):
    i = pl.program_id(1)
    half = S5_BUNDLE * S5_STATE

    @pl.when(i == 0)
    def _():
        state[...] = s0_ref[...]
        m_ref[...] = _widen(mc_ref[...], sp_tc[...], 4, 4)
        we_ref[...] = _widen(wec_ref[...], sp_rp[...], 4, 6)
        ws_ref[...] = _widen(wsc_ref[...], sp_tc[...], 6, 4)

    for c in range(cb):
        for t in range(S5_T):
            ubuf[c * nb:(c + 1) * nb, t * 128:(t + 1) * 128] = u_ref[:, c * S5_T + t, :]
    u = ubuf[...]
    ub = u.astype(BF16)
    ebuf[...] = _dot(ub, we_ref[...])
    ar, ai = a_ref[:, :half], a_ref[:, half:]
    s = state[...]
    for c in range(cb):
        rows = slice(c * nb, (c + 1) * nb)
        sprev[rows, :] = s
        sr, si = s[:, :half], s[:, half:]
        s = jnp.concatenate([ar * sr - ai * si + ebuf[rows, :half],
                             ar * si + ai * sr + ebuf[rows, half:]], axis=1)
    state[...] = s
    y = _dot(ub, m_ref[...]) + _dot(sprev[...].astype(BF16), ws_ref[...])
    ebuf[...] = jax.nn.gelu(y + d_ref[...] * u)
    for c in range(cb):
        for t in range(S5_T):
            g_ref[:, c * S5_T + t, :] = ebuf[c * nb:(c + 1) * nb, t * 128:(t + 1) * 128]

    @pl.when(i == last)
    def _():
        sfin_ref[...] = s


def _s5_scan(h, s0, tables, cb):
    m, we, ws, a_t, dv = tables
    nb, seq, _ = h.shape
    tok = S5_T * cb
    ni = seq // tok
    r = nb * cb
    w = S5_W
    kern = functools.partial(_s5_kernel, nb=nb, cb=cb, last=ni - 1)
    compact = lambda: pl.BlockSpec((None, w, 128), lambda g, i: (g, 0, 0))
    spread = lambda: pl.BlockSpec((128, w), lambda g, i: (0, 0))
    vec = lambda: pl.BlockSpec((None, 1, w), lambda g, i: (g, 0, 0))
    return pl.pallas_call(
        kern, grid=(N_BUNDLES, ni),
        in_specs=[pl.BlockSpec((nb, tok, 128), lambda g, i: (0, i, g)),
                  pl.BlockSpec((None, nb, w), lambda g, i: (g, 0, 0)),
                  compact(), compact(), compact(), spread(), spread(), vec(), vec()],
        out_specs=[pl.BlockSpec((nb, tok, 128), lambda g, i: (0, i, g)),
                   pl.BlockSpec((None, nb, w), lambda g, i: (g, 0, 0))],
        out_shape=[jax.ShapeDtypeStruct((nb, seq, D), F32),
                   jax.ShapeDtypeStruct((N_BUNDLES, nb, w), F32)],
        scratch_shapes=[pltpu.VMEM((w, w), BF16), pltpu.VMEM((w, w), BF16), pltpu.VMEM((w, w), BF16),
                        pltpu.VMEM((nb, w), F32), pltpu.VMEM((r, w), F32), pltpu.VMEM((r, w), F32),
                        pltpu.VMEM((r, w), F32)],
        compiler_params=_cparams(("parallel", "arbitrary")), name="s5_scan",
    )(h, s0, m, we, ws, _group_spread(S5_T, S5_GROUP), _group_spread(2, S5_STATE), a_t, dv)


def _state_to_bundles(s_re, s_im):
    n_b = s_re.shape[0]
    s = jnp.stack([s_re, s_im], axis=1).reshape(n_b, 2, N_BUNDLES, S5_BUNDLE * S5_STATE)
    return s.transpose(2, 0, 1, 3).reshape(N_BUNDLES, n_b, 2 * S5_BUNDLE * S5_STATE)


def _state_from_bundles(s):
    n_b = s.shape[1]
    s = s.reshape(N_BUNDLES, n_b, 2, S5_BUNDLE, S5_STATE).transpose(2, 1, 0, 3, 4)
    s = s.reshape(2, n_b, N_GROUPS, S5_STATE)
    return s[0], s[1]


def _glu_kernel(*refs, n0, two_x):
    if two_x:
        gp_ref, gs_ref, xp_ref, xs_ref, w_ref, o_ref = refs
    else:
        gp_ref, gs_ref, xp_ref, w_ref, o_ref = refs
    first = pl.program_id(0) < n0
    g = jnp.where(first, gp_ref[...], gs_ref[...])
    x = jnp.where(first, xp_ref[...], xs_ref[...]) if two_x else xp_ref[...]
    z = _dot(g.astype(BF16), w_ref[...])
    o_ref[...] = x + z[:, :D] * jax.nn.sigmoid(z[:, D:])


def _glu_residual(g_p, g_s, x_parts, w_glu, tm=512):
    t = g_p.shape[0] + g_s.shape[0]
    n0 = g_p.shape[0] // tm
    two_x = len(x_parts) == 2
    tile = lambda m: pl.BlockSpec((tm, D), m)
    x_specs = [tile(_lo_map(n0)), tile(_hi_map(n0))] if two_x else [tile(lambda i: (i, 0))]
    return pl.pallas_call(
        functools.partial(_glu_kernel, n0=n0, two_x=two_x), grid=(t // tm,),
        in_specs=[tile(_lo_map(n0)), tile(_hi_map(n0))] + x_specs + [pl.BlockSpec((D, 2 * D), lambda i: (0, 0))],
        out_specs=tile(lambda i: (i, 0)),
        out_shape=jax.ShapeDtypeStruct((t, D), F32),
        compiler_params=_cparams(("parallel",)), name="glu_residual",
    )(g_p, g_s, *x_parts, w_glu)


def _s5_layer(x_parts, norm_g, state_re, state_im, params, dims, h_parts=None):
    n_bp, seq_p, n_bs, seq_s = dims
    tp, ts = n_bp * seq_p, n_bs * seq_s
    lam_re, lam_im, log_dt, b_re, b_im, c_re, c_im, d, w_glu = params
    tables = _s5_weights(lam_re, lam_im, log_dt, b_re, b_im, c_re, c_im, d)
    if h_parts is not None:
        h_p, h_s = h_parts
    elif len(x_parts) == 2:
        h_p = _rmsnorm(x_parts[0], norm_g, 0, tp)
        h_s = _rmsnorm(x_parts[1], norm_g, 0, ts)
    else:
        h_p = _rmsnorm(x_parts[0], norm_g, 0, tp)
        h_s = _rmsnorm(x_parts[0], norm_g, tp, ts)
    zeros = jnp.zeros((N_BUNDLES, n_bp, 2 * S5_BUNDLE * S5_STATE), F32)
    cb = math.gcd(seq_p // S5_T, max(1, 512 // n_bp))
    g_p, s_p = _s5_scan(h_p.reshape(n_bp, seq_p, D), zeros, tables, cb)
    g_s, s_s = _s5_scan(h_s.reshape(n_bs, seq_s, D), _state_to_bundles(state_re, state_im), tables, seq_s // S5_T)
    x = _glu_residual(g_p.reshape(tp, D), g_s.reshape(ts, D), x_parts, w_glu.astype(BF16))
    p_re, p_im = _state_from_bundles(s_p)
    s_re, s_im = _state_from_bundles(s_s)
    return x, (p_re, s_re, p_im, s_im)


def _ffn_kernel(x_ref, gn_ref, win_ref, wo_ref, *rest, nf, n0):
    d_ff = wo_ref.shape[0]
    tf = d_ff // nf
    x = x_ref[...]
    h = _rms(x, gn_ref[...]).astype(BF16)
    y = x
    for f in range(nf):
        cols = slice(f * tf, (f + 1) * tf)
        up = slice(d_ff + f * tf, d_ff + (f + 1) * tf)
        t = (jax.nn.silu(_dot(h, win_ref[:, cols])) * _dot(h, win_ref[:, up])).astype(BF16)
        y = y + _dot(t, wo_ref[cols, :])
    if n0 is None:
        rest[0][...] = y
        return
    gnext_ref, o_ref, hp_ref, hs_ref = rest
    o_ref[...] = y
    hn = _rms(y, gnext_ref[...])
    first = pl.program_id(0) < n0

    @pl.when(first)
    def _():
        hp_ref[...] = hn

    @pl.when(jnp.logical_not(first))
    def _():
        hs_ref[...] = hn


def _ffn_dense(x, norm_g, w_in, w_out, next_norm=None, split_rows=None, tm=512, nf=2):
    t = x.shape[0]
    tile = lambda m: pl.BlockSpec((tm, D), m)
    vec = pl.BlockSpec((1, D), lambda i: (0, 0))
    resident = lambda a: pl.BlockSpec(a.shape, lambda i: (0, 0), pipeline_mode=pl.Buffered(1))
    w_in, w_out = w_in.astype(BF16), w_out.astype(BF16)
    args = [x, norm_g.reshape(1, D), w_in, w_out]
    in_specs = [tile(lambda i: (i, 0)), vec, resident(w_in), resident(w_out)]
    out_specs, out_shape, n0 = tile(lambda i: (i, 0)), jax.ShapeDtypeStruct((t, D), F32), None
    if next_norm is not None:
        n0 = split_rows // tm
        args.append(next_norm.reshape(1, D))
        in_specs.append(vec)
        out_specs = [out_specs, tile(_lo_map(n0)), tile(_hi_map(n0))]
        out_shape = [out_shape, jax.ShapeDtypeStruct((split_rows, D), F32),
                     jax.ShapeDtypeStruct((t - split_rows, D), F32)]
    return pl.pallas_call(
        functools.partial(_ffn_kernel, nf=nf, n0=n0), grid=(t // tm,),
        in_specs=in_specs, out_specs=out_specs, out_shape=out_shape,
        compiler_params=_cparams(("arbitrary",)), name="ffn_dense",
    )(*args)


def _swap_halves(w):
    half = w.shape[-1] // 2
    return jnp.concatenate([w[..., half:], w[..., :half]], axis=-1)


def _pad_lanes(w, n=128):
    return jnp.concatenate([w, jnp.zeros(w.shape[:-1] + (n - w.shape[-1],), w.dtype)], axis=-1)


def _rope_tables(seq_p, seq_s, past_len, rows_s):
    half = QK_ROPE // 2
    inv_freq = ROPE_THETA ** (-jnp.arange(half, dtype=F32) / half)
    pos = jnp.concatenate([jnp.arange(seq_p), past_len + (jnp.arange(rows_s) % seq_s)])
    ang = pos.astype(F32)[:, None] * inv_freq[None, :]
    cos, sin = jnp.cos(ang), jnp.sin(ang)
    return _pad_lanes(jnp.concatenate([cos, cos], axis=1)), _pad_lanes(jnp.concatenate([-sin, sin], axis=1))


def _rope_norm(x, x_sw, g, g_sw, cos, sin):
    r = lax.rsqrt(jnp.sum(x * x, axis=-1, keepdims=True) * (1.0 / QK_ROPE) + NORM_EPS)
    return r * (x * g * cos + x_sw * g_sw * sin)


def _mla_proj_kernel(x_ref, gmix, win, nq, nkv, cos, sin, gkr, gkrs, wq, gqn, gqr, gqrs, wuk, gkn,
                     cp_out, cs_out, krp_out, krs_out, cbt_out, q_out, k_out, *, n0):
    is_prompt = pl.program_id(0) < n0
    hn = _rms(x_ref[...], gmix[...]).astype(BF16)
    a = _dot(hn, win[...])
    hq = _rms(a[:, :Q_RANK], nq[...]).astype(BF16)
    c = _rms(a[:, Q_RANK:Q_RANK + KV_RANK], nkv[...])
    cb = c.astype(BF16)
    kr = _rope_norm(a[:, 768:896], a[:, 896:1024], gkr[...], gkrs[...], cos[...], sin[...])
    krb = kr.astype(BF16)

    @pl.when(is_prompt)
    def _():
        cp_out[...] = c
        krp_out[...] = kr
        cbt_out[...] = c.T.astype(BF16)

    @pl.when(jnp.logical_not(is_prompt))
    def _():
        cs_out[...] = c
        krs_out[...] = kr

    q_scale = jnp.where(is_prompt, ATTN_SCALE * LOG2E, 1.0)
    for h in range(N_HEADS):
        q3 = _dot(hq, wq[:, h * 384:(h + 1) * 384])
        qn = _rms(q3[:, :128], gqn[...])
        qr = _rope_norm(q3[:, 128:256], q3[:, 256:384], gqr[...], gqrs[...], cos[...], sin[...])
        q_out[h] = (jnp.concatenate([qn, qr], axis=1) * q_scale).astype(BF16)

    @pl.when(is_prompt)
    def _():
        for h in range(N_HEADS):
            kn = _rms(_dot(cb, wuk[:, h * QK_NOPE:(h + 1) * QK_NOPE]), gkn[...])
            k_out[h] = jnp.concatenate([kn.astype(BF16), krb], axis=1)


def _mla_proj(x, norm_g, params, dims, past_len, tm=512):
    n_bp, seq_p, n_bs, seq_s = dims
    t = x.shape[0]
    tp = n_bp * seq_p
    ts = t - tp
    w_in, norm_q, norm_kv, w_q, w_uk, w_uv, qn_nope, qn_rope, kn_nope, kn_rope, w_o = params
    w_kr = w_in[:, Q_RANK + KV_RANK:]
    win = jnp.concatenate([w_in[:, :Q_RANK + KV_RANK], _pad_lanes(w_kr), _pad_lanes(_swap_halves(w_kr))],
                          axis=1).astype(BF16)
    wq3 = w_q.reshape(Q_RANK, N_HEADS, QK_NOPE + QK_ROPE)
    wq_r = wq3[:, :, QK_NOPE:]
    wq = jnp.concatenate([wq3[:, :, :QK_NOPE], _pad_lanes(wq_r), _pad_lanes(_swap_halves(wq_r))], axis=2)
    wq = wq.reshape(Q_RANK, N_HEADS * 384).astype(BF16)
    wuk = w_uk.reshape(KV_RANK, N_HEADS * QK_NOPE).astype(BF16)
    cos, sin = _rope_tables(seq_p, seq_s, past_len, min(tm, ts))
    n_pt = seq_p // tm
    n0 = tp // tm
    tab = lambda i: (jnp.where(i < n0, i % n_pt, n_pt), 0)
    row = lambda v: v.reshape(1, -1).astype(F32)
    vec = lambda n: pl.BlockSpec((1, n), lambda i: (0, 0))
    full = lambda a: pl.BlockSpec(a.shape, lambda i: (0, 0))
    lo = lambda n: pl.BlockSpec((tm, n), _lo_map(n0))
    hi = lambda n: pl.BlockSpec((tm, n), _hi_map(n0))
    sds = jax.ShapeDtypeStruct
    return pl.pallas_call(
        functools.partial(_mla_proj_kernel, n0=n0), grid=(t // tm,),
        in_specs=[pl.BlockSpec((tm, D), lambda i: (i, 0)), vec(D), full(win), vec(Q_RANK), vec(KV_RANK),
                  pl.BlockSpec((tm, 128), tab), pl.BlockSpec((tm, 128), tab), vec(128), vec(128),
                  full(wq), vec(128), vec(128), vec(128), full(wuk), vec(128)],
        out_specs=[lo(KV_RANK), hi(KV_RANK), lo(128), hi(128),
                   pl.BlockSpec((KV_RANK, tm), lambda i: (0, jnp.minimum(i, n0 - 1))),
                   pl.BlockSpec((N_HEADS, tm, 256), lambda i: (0, i, 0)),
                   pl.BlockSpec((N_HEADS, tm, 256), lambda i: (0, jnp.minimum(i, n0 - 1), 0))],
        out_shape=[sds((tp, KV_RANK), F32), sds((ts, KV_RANK), F32), sds((tp, 128), F32), sds((ts, 128), F32),
                   sds((KV_RANK, tp), BF16), sds((N_HEADS, t, 256), BF16), sds((N_HEADS, tp, 256), BF16)],
        compiler_params=_cparams(("arbitrary",)), name="mla_proj",
    )(x, row(norm_g), win, row(norm_q), row(norm_kv), cos, sin,
      row(_pad_lanes(kn_rope)), row(_pad_lanes(_swap_halves(kn_rope))), wq, row(qn_nope),
      row(_pad_lanes(qn_rope)), row(_pad_lanes(_swap_halves(qn_rope))), wuk, row(kn_nope))


def _flash_kernel(qi_tab, ki_tab, q_ref, k_ref, vt_ref, o_ref, m, l, acc):
    step = pl.program_id(1)
    qi, ki = qi_tab[step], ki_tab[step]

    @pl.when(ki == 0)
    def _():
        m[...] = jnp.full_like(m, NEG_INF)
        l[...] = jnp.zeros_like(l)
        acc[...] = jnp.zeros_like(acc)

    def update(h, masked):
        s = _dot_nt(k_ref[h], q_ref[h])
        if masked:
            key = lax.broadcasted_iota(I32, s.shape, 0)
            qry = lax.broadcasted_iota(I32, s.shape, 1)
            s = jnp.where(key <= qry, s, NEG_INF)
        m_new = jnp.maximum(m[h], jnp.max(s, axis=0, keepdims=True))
        corr = jnp.exp2(m[h] - m_new)
        p = jnp.exp2(s - m_new)
        l[h] = l[h] * corr + jnp.sum(p, axis=0, keepdims=True)
        acc[h] = acc[h] * corr + _dot(vt_ref[...], p.astype(BF16))
        m[h] = m_new

    def heads(masked):
        def pair(hp, carry):
            update(2 * hp, masked)
            update(2 * hp + 1, masked)
            return carry
        lax.fori_loop(0, N_HEADS // 2, pair, 0)

    @pl.when(ki < qi)
    def _():
        heads(False)

    @pl.when(ki == qi)
    def _():
        heads(True)
        for h in range(N_HEADS):
            o_ref[h] = (acc[h] / l[h]).T.astype(o_ref.dtype)


def _mla_prompt_attn(qcat, kcat, cbt, n_b, seq, tq=512):
    nq = seq // tq
    pairs = [(qi, ki) for qi in range(nq) for ki in range(qi + 1)]
    qi_tab = jnp.array([p[0] for p in pairs], I32)
    ki_tab = jnp.array([p[1] for p in pairs], I32)
    grid_spec = pltpu.PrefetchScalarGridSpec(
        num_scalar_prefetch=2, grid=(n_b, len(pairs)),
        in_specs=[pl.BlockSpec((N_HEADS, tq, 256), lambda b, s, qt, kt: (0, b * nq + qt[s], 0)),
                  pl.BlockSpec((N_HEADS, tq, 256), lambda b, s, qt, kt: (0, b * nq + kt[s], 0)),
                  pl.BlockSpec((KV_RANK, tq), lambda b, s, qt, kt: (0, b * nq + kt[s]))],
        out_specs=pl.BlockSpec((N_HEADS, tq, KV_RANK), lambda b, s, qt, kt: (0, b * nq + qt[s], 0)),
        scratch_shapes=[pltpu.VMEM((N_HEADS, 1, tq), F32), pltpu.VMEM((N_HEADS, 1, tq), F32),
                        pltpu.VMEM((N_HEADS, KV_RANK, tq), F32)])
    return pl.pallas_call(
        _flash_kernel, grid_spec=grid_spec,
        out_shape=jax.ShapeDtypeStruct((N_HEADS, n_b * seq, KV_RANK), BF16),
        compiler_params=_cparams(("parallel", "arbitrary")), name="mla_prompt_attn",
    )(qi_tab, ki_tab, qcat, kcat, cbt)


def _sample_attn_kernel(pt_ref, lat_hbm, krt_hbm, cnew, krtnew, q_ref, wuk_ref, wukt_ref, gkn_ref, o_ref,
                        lat_buf, krt_buf, sem, lhs, m, l, acc, *, n_b, npg, n_chunks, dq):
    b, kc = pl.program_id(0), pl.program_id(1)
    nhq = N_HEADS * dq
    nkd = N_HEADS * QK_NOPE
    slot = (b * n_chunks + kc) % 2

    def page_copies(bb, cc, s):
        copies = []
        for j in range(npg):
            pid = pt_ref[bb, cc * npg + j]
            copies.append(pltpu.make_async_copy(lat_hbm.at[pid], lat_buf.at[s, j], sem.at[0, s]))
            copies.append(pltpu.make_async_copy(krt_hbm.at[pid], krt_buf.at[s, j], sem.at[1, s]))
        return copies

    @pl.when(kc == 0)
    def _():
        m[...] = jnp.full_like(m, NEG_INF)
        l[...] = jnp.zeros_like(l)
        acc[...] = jnp.zeros_like(acc)
        lhs[0:nkd, :] = wukt_ref[...]
        qn = (q_ref[:, :QK_NOPE] * gkn_ref[...]).astype(BF16)
        qabs = [_dot_nt(qn[h * dq:(h + 1) * dq], wuk_ref[h]) for h in range(N_HEADS)]
        lhs[nkd:nkd + nhq, :] = jnp.concatenate(qabs, axis=0).astype(BF16)

    def attend(c_pages, krt_pages, mask):
        ct = jnp.concatenate([c.T for c in c_pages], axis=1).astype(BF16)
        cb = jnp.concatenate(c_pages, axis=0).astype(BF16)
        kt = jnp.concatenate(krt_pages, axis=1).astype(BF16)
        nk = ct.shape[1]
        r = _dot(lhs[...], ct)
        rinv = []
        for h in range(N_HEADS):
            kn = r[h * QK_NOPE:(h + 1) * QK_NOPE]
            ssq = jnp.sum(kn * kn, axis=0, keepdims=True)
            rinv.append(jnp.broadcast_to(lax.rsqrt(ssq * (1.0 / QK_NOPE) + NORM_EPS), (dq, nk)))
        qr = q_ref[:, QK_NOPE:QK_NOPE + QK_ROPE].astype(BF16)
        s = (r[nkd:] * jnp.concatenate(rinv, axis=0) + _dot(qr, kt)) * ATTN_SCALE
        if mask is not None:
            s = jnp.where(mask, s, NEG_INF)
        m_new = jnp.maximum(m[...], jnp.max(s, axis=-1, keepdims=True))
        corr = jnp.exp(m[...] - m_new)
        p = jnp.exp(s - m_new)
        l[...] = l[...] * corr + jnp.sum(p, axis=-1, keepdims=True)
        acc[...] = acc[...] * corr + _dot(p.astype(BF16), cb)
        m[...] = m_new

    @pl.when(kc < n_chunks)
    def _():
        @pl.when((b == 0) & (kc == 0))
        def _():
            for c in page_copies(b, kc, slot):
                c.start()

        last_chunk = kc + 1 == n_chunks
        nxt_b = jnp.where(last_chunk, b + 1, b)
        nxt_c = jnp.where(last_chunk, 0, kc + 1)

        @pl.when(nxt_b < n_b)
        def _():
            for c in page_copies(nxt_b, nxt_c, 1 - slot):
                c.start()

        for c in page_copies(b, kc, slot):
            c.wait()
        group = min(npg, ATTEND_PAGES)
        for j0 in range(0, npg, group):
            pages = range(j0, j0 + group)
            attend([lat_buf[slot, j] for j in pages], [krt_buf[slot, j] for j in pages], None)

    @pl.when(kc == n_chunks)
    def _():
        key = lax.broadcasted_iota(I32, (nhq, PAGE), 1)
        qpos = lax.broadcasted_iota(I32, (nhq, PAGE), 0) % dq
        attend([cnew[...]], [krtnew[...]], key <= qpos)
        o_ref[...] = (acc[...] / l[...]).astype(o_ref.dtype)


def _mla_sample_attn(q_s, c_new, kr_new, cache_latent, cache_krope, page_table, wuk, gkn, n_b, dq):
    n_pages = page_table.shape[1]
    npg = math.gcd(n_pages, PAGES_PER_STEP)
    n_chunks = n_pages // npg
    nhq = N_HEADS * dq
    nkd = N_HEADS * QK_NOPE
    wukt = wuk.transpose(0, 2, 1).reshape(nkd, KV_RANK)
    pad = lambda a: jnp.concatenate([a, jnp.zeros((n_b, PAGE - dq, a.shape[-1]), a.dtype)], axis=1)
    krt_cache = cache_krope.swapaxes(1, 2)
    krt_new = pad(kr_new).swapaxes(1, 2)
    per_b = lambda r, w: pl.BlockSpec((None, r, w), lambda b, kc, pt: (b, 0, 0))
    grid_spec = pltpu.PrefetchScalarGridSpec(
        num_scalar_prefetch=1, grid=(n_b, n_chunks + 1),
        in_specs=[pl.BlockSpec(memory_space=pl.ANY), pl.BlockSpec(memory_space=pl.ANY),
                  per_b(PAGE, KV_RANK), per_b(QK_ROPE, PAGE), per_b(nhq, 256),
                  pl.BlockSpec((N_HEADS, KV_RANK, QK_NOPE), lambda b, kc, pt: (0, 0, 0)),
                  pl.BlockSpec((nkd, KV_RANK), lambda b, kc, pt: (0, 0)),
                  pl.BlockSpec((1, QK_NOPE), lambda b, kc, pt: (0, 0))],
        out_specs=per_b(nhq, KV_RANK),
        scratch_shapes=[pltpu.VMEM((2, npg, PAGE, KV_RANK), F32), pltpu.VMEM((2, npg, QK_ROPE, PAGE), F32),
                        pltpu.SemaphoreType.DMA((2, 2)), pltpu.VMEM((nkd + nhq, KV_RANK), BF16),
                        pltpu.VMEM((nhq, 1), F32), pltpu.VMEM((nhq, 1), F32), pltpu.VMEM((nhq, KV_RANK), F32)])
    return pl.pallas_call(
        functools.partial(_sample_attn_kernel, n_b=n_b, npg=npg, n_chunks=n_chunks, dq=dq), grid_spec=grid_spec,
        out_shape=jax.ShapeDtypeStruct((n_b, nhq, KV_RANK), BF16),
        compiler_params=_cparams(("arbitrary", "arbitrary")), name="mla_sample_attn",
    )(page_table, cache_latent, krt_cache, pad(c_new), krt_new, q_s,
      wuk, wukt, gkn.reshape(1, QK_NOPE).astype(F32))


def _mla_out_kernel(op_ref, os_ref, x_ref, wuv_ref, wo_ref, out_ref, *, n0):
    first = pl.program_id(0) < n0
    v = jnp.concatenate([_dot(jnp.where(first, op_ref[h], os_ref[h]), wuv_ref[h]) for h in range(N_HEADS)], axis=1)
    out_ref[...] = x_ref[...] + _dot(v.astype(BF16), wo_ref[...])


def _mla_out(o_p, o_s, x, w_uv, w_o, tm=512):
    t = x.shape[0]
    n0 = o_p.shape[1] // tm
    return pl.pallas_call(
        functools.partial(_mla_out_kernel, n0=n0), grid=(t // tm,),
        in_specs=[pl.BlockSpec((N_HEADS, tm, KV_RANK), lambda i: (0, jnp.minimum(i, n0 - 1), 0)),
                  pl.BlockSpec((N_HEADS, tm, KV_RANK), lambda i: (0, jnp.maximum(i - n0, 0), 0)),
                  pl.BlockSpec((tm, D), lambda i: (i, 0)),
                  pl.BlockSpec((N_HEADS, KV_RANK, 128), lambda i: (0, 0, 0)), pl.BlockSpec((D, D), lambda i: (0, 0))],
        out_specs=pl.BlockSpec((tm, D), lambda i: (i, 0)),
        out_shape=jax.ShapeDtypeStruct((t, D), F32),
        compiler_params=_cparams(("parallel",)), name="mla_out",
    )(o_p, o_s, x, w_uv.transpose(1, 0, 2).astype(BF16), w_o.astype(BF16))


def _mla_layer(x, norm_g, cache_latent, cache_krope, page_table, params, dims):
    n_bp, seq_p, n_bs, seq_s = dims
    tp = n_bp * seq_p
    past_len = page_table.shape[1] * cache_latent.shape[1]
    w_uk, w_uv, kn_nope, w_o = params[4], params[5], params[8], params[10]
    c_p, c_s, kr_p, kr_s, cbt, qcat, kcat = _mla_proj(x, norm_g, params, dims, past_len)
    c_s = c_s.reshape(n_bs, seq_s, KV_RANK)
    kr_s = kr_s[:, :QK_ROPE].reshape(n_bs, seq_s, QK_ROPE)
    o_p = _mla_prompt_attn(qcat, kcat, cbt, n_bp, seq_p)
    q_s = qcat[:, tp:].astype(F32).reshape(N_HEADS, n_bs, seq_s, 256).transpose(1, 0, 2, 3)
    q_s = q_s.reshape(n_bs, N_HEADS * seq_s, 256)
    o_s = _mla_sample_attn(q_s, c_s, kr_s, cache_latent, cache_krope, page_table,
                           w_uk.transpose(1, 0, 2).astype(BF16), kn_nope, n_bs, seq_s)
    o_s = o_s.reshape(n_bs, N_HEADS, seq_s, KV_RANK).transpose(1, 0, 2, 3).reshape(N_HEADS, n_bs * seq_s, KV_RANK)
    x = _mla_out(o_p, o_s, x, w_uv, w_o)
    state = (c_p.reshape(n_bp, seq_p, KV_RANK), c_s, kr_p[:, :QK_ROPE].reshape(n_bp, seq_p, QK_ROPE), kr_s)
    return x, state


def _pool_kernel(x_ref, prev_ref, gn_ref, w_ref, sc_ref, o_ref, st_ref, ext, *, tl, start, last):
    ti = pl.program_id(1)

    @pl.when(ti == 0)
    def _():
        ext[0:POOL_HALO, :] = prev_ref[...]

    @pl.when(ti > 0)
    def _():
        ext[0:POOL_HALO, :] = ext[tl:tl + POOL_HALO, :]

    x = x_ref[...]
    h = _rms(x, gn_ref[...])
    ext[POOL_HALO:POOL_HALO + tl, :] = h
    pos = start + ti * tl + lax.broadcasted_iota(I32, (tl, 1), 0)
    run = ext[...]
    z = []
    for g, w in enumerate(POOL_WINDOWS):
        run = run[:, (POOL_GROUP if g else 0):]
        run = run + pltpu.roll(run, w // 2, 0)
        total = run[POOL_HALO:POOL_HALO + tl, :POOL_GROUP]
        count = jnp.minimum(w, pos + 1).astype(F32)
        p = total / count - h[:, g * POOL_GROUP:(g + 1) * POOL_GROUP]
        z.append(_dot(p.astype(BF16), w_ref[g]))
    o_ref[...] = x + jnp.concatenate(z, axis=1) * sc_ref[...]

    @pl.when(ti == last)
    def _():
        st_ref[...] = ext[tl:tl + POOL_HALO, :]


def _pool_stream(x, prev, norm_g, pool_w, pool_scale, n_b, seq, row0, start, tl):
    nt = seq // tl
    blk0 = row0 // tl
    prev = jnp.concatenate([jnp.zeros((n_b, 1, D), F32), prev.astype(F32)], axis=1)
    rows = lambda b, t: (blk0 + b * nt + t, 0)
    out, st = pl.pallas_call(
        functools.partial(_pool_kernel, tl=tl, start=start, last=nt - 1), grid=(n_b, nt),
        in_specs=[pl.BlockSpec((tl, D), rows),
                  pl.BlockSpec((None, POOL_HALO, D), lambda b, t: (b, 0, 0)),
                  pl.BlockSpec((1, D), lambda b, t: (0, 0)),
                  pl.BlockSpec((4, POOL_GROUP, POOL_GROUP), lambda b, t: (0, 0, 0)),
                  pl.BlockSpec((1, D), lambda b, t: (0, 0))],
        out_specs=[pl.BlockSpec((tl, D), rows),
                   pl.BlockSpec((None, POOL_HALO, D), lambda b, t: (b, 0, 0))],
        out_shape=[jax.ShapeDtypeStruct(x.shape, F32), jax.ShapeDtypeStruct((n_b, POOL_HALO, D), F32)],
        scratch_shapes=[pltpu.VMEM((POOL_HALO + tl, D), F32)],
        input_output_aliases={0: 0},
        compiler_params=_cparams(("parallel", "arbitrary")), name="pool_mixer",
    )(x, prev, norm_g.reshape(1, D), pool_w.astype(BF16), pool_scale.reshape(1, D))
    return out, st[:, 1:]


def _pool_layer(x, norm_g, state_pool, params, dims, past_len):
    n_bp, seq_p, n_bs, seq_s = dims
    tp = n_bp * seq_p
    pool_w, pool_scale = params
    zero = jnp.zeros((n_bp, POOL_HALO - 1, D), F32)
    x, st_p = _pool_stream(x, zero, norm_g, pool_w, pool_scale, n_bp, seq_p, 0, 0, min(512, seq_p))
    x, st_s = _pool_stream(x, state_pool, norm_g, pool_w, pool_scale, n_bs, seq_s, tp, past_len, seq_s)
    return x, (st_p, st_s)


def _router_kernel(x_ref, gn_ref, wr_ref, tri_ref, hb_ref, sel_ref, gate_ref, cnt_ref):
    h = _rms(x_ref[...], gn_ref[...])
    hb_ref[...] = h.astype(BF16)
    logits = lax.dot_general(wr_ref[...], h, (((1,), (1,)), ((), ())), precision=lax.Precision.HIGHEST,
                             preferred_element_type=F32)
    idx = lax.broadcasted_iota(I32, logits.shape, 0)
    m1 = jnp.max(logits, axis=0, keepdims=True)
    i1 = jnp.min(jnp.where(logits == m1, idx, N_EXPERTS), axis=0, keepdims=True)
    first = idx == i1
    rest = jnp.where(first, -jnp.inf, logits)
    m2 = jnp.max(rest, axis=0, keepdims=True)
    i2 = jnp.min(jnp.where(rest == m2, idx, N_EXPERTS), axis=0, keepdims=True)
    second = idx == i2
    e2 = jnp.exp(m2 - m1)
    g1 = 1.0 / (1.0 + e2)
    g2 = e2 / (1.0 + e2)
    chosen = jnp.where(first | second, 1.0, 0.0)
    rank = _dot(chosen.astype(BF16), tri_ref[...])
    rank1 = jnp.sum(jnp.where(first, rank, 0.0), axis=0, keepdims=True)
    rank2 = jnp.sum(jnp.where(second, rank, 0.0), axis=0, keepdims=True)
    zero = jnp.zeros((N_EXPERTS - 4,) + i1.shape[1:], I32)
    sel_ref[...] = jnp.concatenate([i1, i2, rank1.astype(I32), rank2.astype(I32), zero], axis=0)
    gate_ref[...] = jnp.concatenate([g1, g2, jnp.zeros((N_EXPERTS - 2,) + g1.shape[1:], F32)], axis=0)
    cnt_ref[...] = jnp.broadcast_to(jnp.sum(chosen, axis=1, keepdims=True), cnt_ref.shape).astype(I32)


def _moe_route(x, norm_g, w_router):
    t = x.shape[0]
    tb = MOE_TB
    nblk = t // tb
    tri = (jnp.arange(tb)[:, None] < jnp.arange(tb)[None, :]).astype(BF16)
    blk = lambda dt: (pl.BlockSpec((None, N_EXPERTS, tb), lambda i: (i, 0, 0)),
                      jax.ShapeDtypeStruct((nblk, N_EXPERTS, tb), dt))
    (sel_spec, sel_shape), (gate_spec, gate_shape) = blk(I32), blk(F32)
    return pl.pallas_call(
        _router_kernel, grid=(nblk,),
        in_specs=[pl.BlockSpec((tb, D), lambda i: (i, 0)), pl.BlockSpec((1, D), lambda i: (0, 0)),
                  pl.BlockSpec((N_EXPERTS, D), lambda i: (0, 0)), pl.BlockSpec((tb, tb), lambda i: (0, 0))],
        out_specs=[pl.BlockSpec((tb, D), lambda i: (i, 0)), sel_spec, gate_spec,
                   pl.BlockSpec((None, N_EXPERTS, 128), lambda i: (i, 0, 0))],
        out_shape=[jax.ShapeDtypeStruct((t, D), BF16), sel_shape, gate_shape,
                   jax.ShapeDtypeStruct((nblk, N_EXPERTS, 128), I32)],
        compiler_params=_cparams(("parallel",)), name="moe_router",
    )(x, norm_g.reshape(1, D), w_router.T.astype(F32), tri)


def _moe_rows(t):
    nblk = t // MOE_TB
    worst = 2 * t + nblk * N_EXPERTS * (MOE_ALIGN - 1) + N_EXPERTS * (MOE_TM - 1)
    return (worst + MOE_TM - 1) // MOE_TM * MOE_TM


def _moe_plan(sel, counts, t):
    cnt = counts[:, :, 0]
    cpad = (cnt + MOE_ALIGN - 1) // MOE_ALIGN * MOE_ALIGN
    per_e = jnp.sum(cpad, axis=0)
    per_e_pad = (per_e + MOE_TM - 1) // MOE_TM * MOE_TM
    e_start = jnp.cumsum(per_e_pad) - per_e_pad
    off = e_start[None, :] + jnp.cumsum(cpad, axis=0) - cpad
    lend = jnp.cumsum(cpad, axis=1)
    lstart = lend - cpad
    e1, e2, r1, r2 = sel[:, 0], sel[:, 1], sel[:, 2], sel[:, 3]
    experts = jnp.arange(N_EXPERTS)
    pick = lambda tab, e: jnp.sum(jnp.where(e[:, :, None] == experts, tab[:, None, :], 0), axis=2)
    dstl = jnp.stack([pick(lstart, e1) + r1, pick(lstart, e2) + r2], axis=1).astype(I32)
    prow = jnp.arange(MOE_PIECES)[None, :] * MOE_ALIGN
    pe = jnp.minimum(jnp.sum(prow[:, :, None] >= lend[:, None, :], axis=2), N_EXPERTS - 1)
    grow = pick(off, pe) + prow - pick(lstart, pe)
    n_pieces = lend[:, -1] // MOE_ALIGN
    grow = jnp.where(jnp.arange(MOE_PIECES)[None, :] < n_pieces[:, None], grow, 0)
    n_tiles = _moe_rows(t) // MOE_TM
    e_end = e_start + per_e_pad
    tile_e = jnp.sum((jnp.arange(n_tiles) * MOE_TM)[:, None] >= e_end[None, :], axis=1)
    active = (tile_e < N_EXPERTS).astype(I32)
    tile_e = jnp.where(active == 1, tile_e, N_EXPERTS - 1)
    return dict(dstl=dstl, grow=grow.astype(I32).reshape(-1), n_pieces=n_pieces.astype(I32),
                tile_e=tile_e.astype(I32), active=active)


def _dispatch_kernel(np_ref, grow_ref, hb_ref, dst_ref, gate_ref, xs_in, xs_ref, buf, sem):
    del xs_in
    b = pl.program_id(0)
    nblk = pl.num_programs(0)
    slot = b % 2

    def copy(blk, i, s):
        r = pl.multiple_of(grow_ref[blk * MOE_PIECES + i], MOE_ALIGN)
        src = buf.at[s, pl.ds(pl.multiple_of(i * MOE_ALIGN, MOE_ALIGN), MOE_ALIGN), :]
        return pltpu.make_async_copy(src, xs_ref.at[pl.ds(r, MOE_ALIGN), :], sem.at[s])

    def drain(blk, s):
        lax.fori_loop(0, np_ref[blk], lambda i, c: (copy(blk, i, s).wait(), c)[1], 0)

    @pl.when(b >= 2)
    def _():
        drain(b - 2, slot)

    want = lax.broadcasted_iota(I32, (MOE_KL, 1), 0)
    hit1 = dst_ref[0:1, :] == want
    hit2 = dst_ref[1:2, :] == want
    onehot = jnp.where(hit1 | hit2, 1.0, 0.0).astype(BF16)
    buf[slot, :, 0:D] = _dot(onehot, hb_ref[...]).astype(BF16)
    gate = jnp.sum(jnp.where(hit1, gate_ref[0:1, :], 0.0) + jnp.where(hit2, gate_ref[1:2, :], 0.0),
                   axis=1, keepdims=True)
    g_hi = gate.astype(BF16).astype(F32)
    g_mid = (gate - g_hi).astype(BF16).astype(F32)
    g_lo = gate - g_hi - g_mid
    lane = lax.broadcasted_iota(I32, (MOE_KL, 128), 1)
    slab = jnp.where(lane == 0, g_hi, jnp.where(lane == 1, g_mid, jnp.where(lane == 2, g_lo, 0.0)))
    buf[slot, :, D:MOE_XW] = slab.astype(BF16)
    lax.fori_loop(0, np_ref[b], lambda i, c: (copy(b, i, slot).start(), c)[1], 0)

    @pl.when(b == nblk - 1)
    def _():
        @pl.when(b >= 1)
        def _():
            drain(b - 1, 1 - slot)
        drain(b, slot)


def _moe_dispatch(hb, gate, plan):
    t = hb.shape[0]
    nblk = t // MOE_TB
    rows = _moe_rows(t)
    grid_spec = pltpu.PrefetchScalarGridSpec(
        num_scalar_prefetch=2, grid=(nblk,),
        in_specs=[pl.BlockSpec((MOE_TB, D), lambda b, *_: (b, 0)),
                  pl.BlockSpec((None, 2, MOE_TB), lambda b, *_: (b, 0, 0)),
                  pl.BlockSpec((None, N_EXPERTS, MOE_TB), lambda b, *_: (b, 0, 0)),
                  pl.BlockSpec(memory_space=pl.ANY)],
        out_specs=pl.BlockSpec(memory_space=pl.ANY),
        scratch_shapes=[pltpu.VMEM((2, MOE_KL, MOE_XW), BF16), pltpu.SemaphoreType.DMA((2,))])
    return pl.pallas_call(
        _dispatch_kernel, grid_spec=grid_spec,
        out_shape=jax.ShapeDtypeStruct((rows, MOE_XW), BF16),
        input_output_aliases={5: 0},
        compiler_params=_cparams(("arbitrary",)), name="moe_dispatch",
    )(plan["n_pieces"], plan["grow"], hb, plan["dstl"], gate, jnp.zeros((rows, MOE_XW), BF16))


def _expert_kernel(te_ref, act_ref, x_ref, wg_ref, wu_ref, wo_ref, o_ref, acc, *, nf):
    i, f = pl.program_id(0), pl.program_id(1)

    @pl.when(f == 0)
    def _():
        acc[...] = jnp.zeros_like(acc)

    @pl.when(act_ref[i] == 1)
    def _():
        x = x_ref[:, 0:D]
        t = (jax.nn.silu(_dot(x, wg_ref[...])) * _dot(x, wu_ref[...])).astype(BF16)
        acc[...] += _dot(t, wo_ref[...])

    @pl.when(f == nf - 1)
    def _():
        gate = jnp.sum(x_ref[:, D:MOE_XW].astype(F32), axis=1, keepdims=True)
        o_ref[...] = (acc[...] * gate).astype(o_ref.dtype)


def _moe_experts(xs, w_in, w_out, plan, tf=512):
    rows = xs.shape[0]
    d_ff = w_out.shape[1]
    nf = d_ff // tf
    w_in, w_out = w_in.astype(BF16), w_out.astype(BF16)
    fe = lambda i, f, act: jnp.where(act[i] == 1, f, nf - 1)
    slab = lambda half: pl.BlockSpec((None, D, tf), lambda i, f, te, act: (te[i], 0, half * nf + fe(i, f, act)))
    grid_spec = pltpu.PrefetchScalarGridSpec(
        num_scalar_prefetch=2, grid=(rows // MOE_TM, nf),
        in_specs=[pl.BlockSpec((MOE_TM, MOE_XW), lambda i, f, te, act: (i, 0)),
                  slab(0), slab(1),
                  pl.BlockSpec((None, tf, D), lambda i, f, te, act: (te[i], fe(i, f, act), 0))],
        out_specs=pl.BlockSpec((MOE_TM, D), lambda i, f, te, act: (i, 0)),
        scratch_shapes=[pltpu.VMEM((MOE_TM, D), F32)])
    return pl.pallas_call(
        functools.partial(_expert_kernel, nf=nf), grid_spec=grid_spec,
        out_shape=jax.ShapeDtypeStruct((rows, D), BF16),
        compiler_params=_cparams(("parallel", "arbitrary")), name="moe_experts",
    )(plan["tile_e"], plan["active"], xs, w_in, w_in, w_out)


def _combine_kernel(np_ref, grow_ref, x_ref, dst_ref, ys_ref, *rest, n0):
    outs, (buf, sem) = rest[:-2], rest[-2:]
    b = pl.program_id(0)
    nblk = pl.num_programs(0)
    slot = b % 2

    def copy(blk, i, s):
        r = pl.multiple_of(grow_ref[blk * MOE_PIECES + i], MOE_ALIGN)
        dst = buf.at[s, pl.ds(pl.multiple_of(i * MOE_ALIGN, MOE_ALIGN), MOE_ALIGN), :]
        return pltpu.make_async_copy(ys_ref.at[pl.ds(r, MOE_ALIGN), :], dst, sem.at[s])

    def fetch(blk, s):
        lax.fori_loop(0, np_ref[blk], lambda i, c: (copy(blk, i, s).start(), c)[1], 0)

    @pl.when(b == 0)
    def _():
        buf[...] = jnp.zeros_like(buf)
        fetch(0, 0)

    @pl.when(b + 1 < nblk)
    def _():
        fetch(b + 1, 1 - slot)

    want = lax.broadcasted_iota(I32, (1, MOE_KL), 1)
    onehot = jnp.where((dst_ref[:, 0:1] == want) | (dst_ref[:, 1:2] == want), 1.0, 0.0).astype(BF16)
    lax.fori_loop(0, np_ref[b], lambda i, c: (copy(b, i, slot).wait(), c)[1], 0)
    y = x_ref[...] + _dot(onehot, buf[slot])
    if len(outs) == 1:
        outs[0][...] = y
    else:
        @pl.when(b < n0)
        def _():
            outs[0][...] = y

        @pl.when(b >= n0)
        def _():
            outs[1][...] = y


def _moe_combine(x, ys, dst_tok, plan, split_rows):
    t = x.shape[0]
    nblk = t // MOE_TB
    tile = lambda m: pl.BlockSpec((MOE_TB, D), m)
    if split_rows is None:
        n0 = nblk
        out_specs = tile(lambda b, *_: (b, 0))
        out_shape = jax.ShapeDtypeStruct((t, D), F32)
    else:
        n0 = split_rows // MOE_TB
        out_specs = [tile(lambda b, *_: (jnp.minimum(b, n0 - 1), 0)), tile(lambda b, *_: (jnp.maximum(b - n0, 0), 0))]
        out_shape = [jax.ShapeDtypeStruct((split_rows, D), F32), jax.ShapeDtypeStruct((t - split_rows, D), F32)]
    grid_spec = pltpu.PrefetchScalarGridSpec(
        num_scalar_prefetch=2, grid=(nblk,),
        in_specs=[tile(lambda b, *_: (b, 0)),
                  pl.BlockSpec((MOE_TB, 2), lambda b, *_: (b, 0)),
                  pl.BlockSpec(memory_space=pl.ANY)],
        out_specs=out_specs,
        scratch_shapes=[pltpu.VMEM((2, MOE_KL, D), BF16), pltpu.SemaphoreType.DMA((2,))])
    return pl.pallas_call(
        functools.partial(_combine_kernel, n0=n0), grid_spec=grid_spec, out_shape=out_shape,
        compiler_params=_cparams(("arbitrary",)), name="moe_combine",
    )(plan["n_pieces"], plan["grow"], x, dst_tok, ys)


def _moe_layer(x, norm_g, w_router, w_in, w_out, split_rows=None):
    t = x.shape[0]
    hb, sel, gate, counts = _moe_route(x, norm_g, w_router)
    plan = _moe_plan(sel, counts, t)
    xs = _moe_dispatch(hb, gate, plan)
    ys = _moe_experts(xs, w_in, w_out, plan)
    dst_tok = plan["dstl"].transpose(0, 2, 1).reshape(t, 2)
    return _moe_combine(x, ys, dst_tok, plan, split_rows)


def kernel(x_prompt, x_sample, state_l0_s5_re, state_l0_s5_im, cache_l1_latent, cache_l1_krope, state_l2_pool, state_l3_s5_re, state_l3_s5_im, page_table, l0_norm_mix, l0_s5_lambda_re, l0_s5_lambda_im, l0_s5_log_dt, l0_s5_b_re, l0_s5_b_im, l0_s5_c_re, l0_s5_c_im, l0_s5_d, l0_s5_w_glu, l0_norm_ffn, l0_ffn_w_in, l0_ffn_w_out, l1_norm_mix, l1_mla_w_in, l1_mla_norm_q, l1_mla_norm_kv, l1_mla_w_q, l1_mla_w_uk, l1_mla_w_uv, l1_mla_qn_nope, l1_mla_qn_rope, l1_mla_kn_nope, l1_mla_kn_rope, l1_mla_w_o, l1_norm_ffn, l1_moe_w_router, l1_moe_w_in, l1_moe_w_out, l2_norm_mix, l2_pool_w, l2_pool_scale, l2_norm_ffn, l2_ffn_w_in, l2_ffn_w_out, l3_norm_mix, l3_s5_lambda_re, l3_s5_lambda_im, l3_s5_log_dt, l3_s5_b_re, l3_s5_b_im, l3_s5_c_re, l3_s5_c_im, l3_s5_d, l3_s5_w_glu, l3_norm_ffn, l3_moe_w_router, l3_moe_w_in, l3_moe_w_out):
    n_bp, seq_p, _ = x_prompt.shape
    n_bs, seq_s, _ = x_sample.shape
    dims = (n_bp, seq_p, n_bs, seq_s)
    tp, ts = n_bp * seq_p, n_bs * seq_s
    past_len = page_table.shape[1] * cache_l1_latent.shape[1]

    x, st0 = _s5_layer((x_prompt.reshape(tp, D), x_sample.reshape(ts, D)), l0_norm_mix, state_l0_s5_re, state_l0_s5_im,
                       (l0_s5_lambda_re, l0_s5_lambda_im, l0_s5_log_dt, l0_s5_b_re, l0_s5_b_im,
                        l0_s5_c_re, l0_s5_c_im, l0_s5_d, l0_s5_w_glu), dims)
    x = _ffn_dense(x, l0_norm_ffn, l0_ffn_w_in, l0_ffn_w_out)

    x, st1 = _mla_layer(x, l1_norm_mix, cache_l1_latent, cache_l1_krope, page_table,
                        (l1_mla_w_in, l1_mla_norm_q, l1_mla_norm_kv, l1_mla_w_q, l1_mla_w_uk, l1_mla_w_uv,
                         l1_mla_qn_nope, l1_mla_qn_rope, l1_mla_kn_nope, l1_mla_kn_rope, l1_mla_w_o), dims)
    x = _moe_layer(x, l1_norm_ffn, l1_moe_w_router, l1_moe_w_in, l1_moe_w_out)

    x, st2 = _pool_layer(x, l2_norm_mix, state_l2_pool, (l2_pool_w, l2_pool_scale), dims, past_len)
    x, h_p, h_s = _ffn_dense(x, l2_norm_ffn, l2_ffn_w_in, l2_ffn_w_out, next_norm=l3_norm_mix, split_rows=tp)

    x, st3 = _s5_layer((x,), l3_norm_mix, state_l3_s5_re, state_l3_s5_im,
                       (l3_s5_lambda_re, l3_s5_lambda_im, l3_s5_log_dt, l3_s5_b_re, l3_s5_b_im,
                        l3_s5_c_re, l3_s5_c_im, l3_s5_d, l3_s5_w_glu), dims, h_parts=(h_p, h_s))
    y_p, y_s = _moe_layer(x, l3_norm_ffn, l3_moe_w_router, l3_moe_w_in, l3_moe_w_out, split_rows=tp)

    return (y_p.reshape(n_bp, seq_p, D), y_s.reshape(n_bs, seq_s, D)) + st0 + st1 + st2 + st3
```

```python
import functools
import math

import jax
import jax.numpy as jnp
from jax import lax
from jax.experimental import pallas as pl
from jax.experimental.pallas import tpu as pltpu

F32 = jnp.float32
BF16 = jnp.bfloat16
I32 = jnp.int32

D = 1024
NORM_EPS = 1e-6
NEG_INF = -1e30
S5_GROUP = 16
S5_STATE = 64
N_GROUPS = D // S5_GROUP
S5_T = 8
S5_BUNDLE = 8
N_BUNDLES = N_GROUPS // S5_BUNDLE
S5_W = S5_T * 128
N_HEADS = 8
QK_NOPE = 128
QK_ROPE = 64
Q_RANK = 512
KV_RANK = 256
ROPE_THETA = 10000.0
ATTN_SCALE = (QK_NOPE + QK_ROPE) ** -0.5
LOG2E = math.log2(math.e)
PAGE = 128
PAGES_PER_STEP = 32
ATTEND_PAGES = 16
POOL_WINDOWS = (2, 4, 8, 16)
POOL_HALO = 16
POOL_GROUP = D // 4
N_EXPERTS = 8
MOE_TB = 512
MOE_TM = 1024
MOE_ALIGN = 16
MOE_KL = (2 * MOE_TB + N_EXPERTS * (MOE_ALIGN - 1) + 127) // 128 * 128
MOE_PIECES = MOE_KL // MOE_ALIGN
MOE_XW = D + 128


def _cparams(sem, vmem_mb=48):
    return pltpu.CompilerParams(dimension_semantics=sem, vmem_limit_bytes=vmem_mb * 1024 * 1024)


def _rms(x, g):
    r = lax.rsqrt(jnp.mean(x * x, axis=-1, keepdims=True) + NORM_EPS)
    return x * r * g


def _dot(a, b):
    return jnp.dot(a, b, preferred_element_type=F32)


def _dot_nt(a, b):
    return lax.dot_general(a, b, (((1,), (1,)), ((), ())), preferred_element_type=F32)


def _lo_map(n):
    return lambda i: (jnp.minimum(i, n - 1), 0)


def _hi_map(n):
    return lambda i: (jnp.maximum(i - n, 0), 0)


def _rmsnorm_kernel(x_ref, g_ref, o_ref):
    o_ref[...] = _rms(x_ref[...], g_ref[...]).astype(o_ref.dtype)


def _rmsnorm(x, g, row0, rows, tm=1024):
    tm = min(tm, rows)
    blk0 = row0 // tm
    return pl.pallas_call(
        _rmsnorm_kernel, grid=(rows // tm,),
        in_specs=[pl.BlockSpec((tm, D), lambda i: (blk0 + i, 0)), pl.BlockSpec((1, D), lambda i: (0, 0))],
        out_specs=pl.BlockSpec((tm, D), lambda i: (i, 0)),
        out_shape=jax.ShapeDtypeStruct((rows, D), F32),
        compiler_params=_cparams(("parallel",)), name="rmsnorm",
    )(x, g.reshape(1, D))


def _s5_weights(lam_re, lam_im, log_dt, b_re, b_im, c_re, c_im, d):
    lr, li = lam_re.astype(F32), lam_im.astype(F32)
    dt = jnp.exp(log_dt.astype(F32))[:, None]
    mag = jnp.exp(lr * dt)
    ar, ai = mag * jnp.cos(li * dt), mag * jnp.sin(li * dt)
    nr, ni = ar - 1.0, ai
    den = lr * lr + li * li
    f_re = ((nr * lr + ni * li) / den)[..., None]
    f_im = ((ni * lr - nr * li) / den)[..., None]
    br, bi = b_re.astype(F32), b_im.astype(F32)
    bb_re, bb_im = f_re * br - f_im * bi, f_re * bi + f_im * br
    cr, ci = c_re.astype(F32), c_im.astype(F32)

    pr, pi = [jnp.ones_like(ar)], [jnp.zeros_like(ar)]
    for _ in range(S5_T):
        pr.append(pr[-1] * ar - pi[-1] * ai)
        pi.append(pr[-2] * ai + pi[-1] * ar)
    pr, pi = jnp.stack(pr), jnp.stack(pi)

    ab_re = pr[:S5_T, :, :, None] * bb_re - pi[:S5_T, :, :, None] * bb_im
    ab_im = pr[:S5_T, :, :, None] * bb_im + pi[:S5_T, :, :, None] * bb_re
    hi = lax.Precision.HIGHEST
    kern = (jnp.einsum("gcp,jgpd->jgcd", cr, ab_re, precision=hi)
            - jnp.einsum("gcp,jgpd->jgcd", ci, ab_im, precision=hi))

    nb, bg, t = N_BUNDLES, S5_BUNDLE, S5_T
    tt = jnp.arange(t)
    lag = tt[None, :] - tt[:, None]
    ksel = jnp.where((lag >= 0)[:, :, None, None, None], kern[jnp.clip(lag, 0, t - 1)], 0.0)
    ksel = ksel.reshape(t, t, nb, bg, S5_GROUP, S5_GROUP).transpose(2, 0, 3, 5, 1, 4)
    m = ksel.astype(BF16).reshape(nb, t * 128, 128)

    inj = jnp.stack([ab_re[::-1], ab_im[::-1]])
    inj = inj.reshape(2, t, nb, bg, S5_STATE, S5_GROUP).transpose(2, 1, 3, 5, 0, 4)
    we = inj.astype(BF16).reshape(nb, t * 128, 128)

    p1r, p1i = pr[1:, :, None, :], pi[1:, :, None, :]
    co_re = cr[None] * p1r - ci[None] * p1i
    co_im = -(cr[None] * p1i + ci[None] * p1r)
    co = jnp.stack([co_re, co_im]).reshape(2, t, nb, bg, S5_GROUP, S5_STATE).transpose(2, 0, 3, 5, 1, 4)
    ws = co.astype(BF16).reshape(nb, 2 * bg * S5_STATE, 128)

    a_t = jnp.stack([pr[t], pi[t]]).reshape(2, nb, bg * S5_STATE)
    a_t = a_t.transpose(1, 0, 2).reshape(nb, 1, 2 * bg * S5_STATE)
    dv = jnp.broadcast_to(d.astype(F32).reshape(1, nb, 128), (t, nb, 128)).transpose(1, 0, 2).reshape(nb, 1, t * 128)
    return m, we, ws, a_t, dv


def _group_spread(n_outer, inner):
    r = jnp.arange(n_outer * inner)[:, None]
    c = jnp.arange(n_outer * S5_BUNDLE * inner)[None, :]
    return ((r // inner == c // (S5_BUNDLE * inner)) & (r % inner == c % inner)).astype(BF16)


def _widen(compact, spread, row_shift, col_shift):
    wide = _dot(compact, spread)
    row_g = (lax.broadcasted_iota(I32, wide.shape, 0) >> row_shift) & (S5_BUNDLE - 1)
    col_h = (lax.broadcasted_iota(I32, wide.shape, 1) >> col_shift) & (S5_BUNDLE - 1)
    return jnp.where(row_g == col_h, wide, 0.0).astype(BF16)


def _s5_kernel(u_ref, s0_ref, mc_ref, wec_ref, wsc_ref, sp_tc, sp_rp, a_ref, d_ref, g_ref, sfin_ref,
               m_ref, we_ref, ws_ref, state, ubuf, sprev, ebuf, *, nb, cb, last):
    i = pl.program_id(1)
    half = S5_BUNDLE * S5_STATE

    @pl.when(i == 0)
    def _():
        state[...] = s0_ref[...]
        m_ref[...] = _widen(mc_ref[...], sp_tc[...], 4, 4)
        we_ref[...] = _widen(wec_ref[...], sp_rp[...], 4, 6)
        ws_ref[...] = _widen(wsc_ref[...], sp_tc[...], 6, 4)

    for c in range(cb):
        for t in range(S5_T):
            ubuf[c * nb:(c + 1) * nb, t * 128:(t + 1) * 128] = u_ref[:, c * S5_T + t, :]
    u = ubuf[...]
    ub = u.astype(BF16)
    ebuf[...] = _dot(ub, we_ref[...])
    ar, ai = a_ref[:, :half], a_ref[:, half:]
    s = state[...]
    for c in range(cb):
        rows = slice(c * nb, (c + 1) * nb)
        sprev[rows, :] = s
        sr, si = s[:, :half], s[:, half:]
        s = jnp.concatenate([ar * sr - ai * si + ebuf[rows, :half],
                             ar * si + ai * sr + ebuf[rows, half:]], axis=1)
    state[...] = s
    y = jnp.concatenate([_dot(ub[:, :256 * (j + 1)], m_ref[0:256 * (j + 1), 256 * j:256 * (j + 1)])
                         for j in range(S5_W // 256)], axis=1)
    y = y + _dot(sprev[...].astype(BF16), ws_ref[...])
    ebuf[...] = jax.nn.gelu(y + d_ref[...] * u)
    for c in range(cb):
        for t in range(S5_T):
            g_ref[:, c * S5_T + t, :] = ebuf[c * nb:(c + 1) * nb, t * 128:(t + 1) * 128]

    @pl.when(i == last)
    def _():
        sfin_ref[...] = s


def _s5_scan(h, s0, tables, cb):
    m, we, ws, a_t, dv = tables
    nb, seq, _ = h.shape
    tok = S5_T * cb
    ni = seq // tok
    r = nb * cb
    w = S5_W
    kern = functools.partial(_s5_kernel, nb=nb, cb=cb, last=ni - 1)
    compact = lambda: pl.BlockSpec((None, w, 128), lambda g, i: (g, 0, 0))
    spread = lambda: pl.BlockSpec((128, w), lambda g, i: (0, 0))
    vec = lambda: pl.BlockSpec((None, 1, w), lambda g, i: (g, 0, 0))
    return pl.pallas_call(
        kern, grid=(N_BUNDLES, ni),
        in_specs=[pl.BlockSpec((nb, tok, 128), lambda g, i: (0, i, g)),
                  pl.BlockSpec((None, nb, w), lambda g, i: (g, 0, 0)),
                  compact(), compact(), compact(), spread(), spread(), vec(), vec()],
        out_specs=[pl.BlockSpec((nb, tok, 128), lambda g, i: (0, i, g)),
                   pl.BlockSpec((None, nb, w), lambda g, i: (g, 0, 0))],
        out_shape=[jax.ShapeDtypeStruct((nb, seq, D), F32),
                   jax.ShapeDtypeStruct((N_BUNDLES, nb, w), F32)],
        scratch_shapes=[pltpu.VMEM((w, w), BF16), pltpu.VMEM((w, w), BF16), pltpu.VMEM((w, w), BF16),
                        pltpu.VMEM((nb, w), F32), pltpu.VMEM((r, w), F32), pltpu.VMEM((r, w), F32),
                        pltpu.VMEM((r, w), F32)],
        compiler_params=_cparams(("parallel", "arbitrary")), name="s5_scan",
    )(h, s0, m, we, ws, _group_spread(S5_T, S5_GROUP), _group_spread(2, S5_STATE), a_t, dv)


def _state_to_bundles(s_re, s_im):
    n_b = s_re.shape[0]
    s = jnp.stack([s_re, s_im], axis=1).reshape(n_b, 2, N_BUNDLES, S5_BUNDLE * S5_STATE)
    return s.transpose(2, 0, 1, 3).reshape(N_BUNDLES, n_b, 2 * S5_BUNDLE * S5_STATE)


def _state_from_bundles(s):
    n_b = s.shape[1]
    s = s.reshape(N_BUNDLES, n_b, 2, S5_BUNDLE, S5_STATE).transpose(2, 1, 0, 3, 4)
    s = s.reshape(2, n_b, N_GROUPS, S5_STATE)
    return s[0], s[1]


def _glu_kernel(*refs, n0, two_x):
    if two_x:
        gp_ref, gs_ref, xp_ref, xs_ref, w_ref, o_ref = refs
    else:
        gp_ref, gs_ref, xp_ref, w_ref, o_ref = refs
    first = pl.program_id(0) < n0
    g = jnp.where(first, gp_ref[...], gs_ref[...])
    x = jnp.where(first, xp_ref[...], xs_ref[...]) if two_x else xp_ref[...]
    z = _dot(g.astype(BF16), w_ref[...])
    o_ref[...] = x + z[:, :D] * jax.nn.sigmoid(z[:, D:])


def _glu_residual(g_p, g_s, x_parts, w_glu, tm=512):
    t = g_p.shape[0] + g_s.shape[0]
    n0 = g_p.shape[0] // tm
    two_x = len(x_parts) == 2
    tile = lambda m: pl.BlockSpec((tm, D), m)
    x_specs = [tile(_lo_map(n0)), tile(_hi_map(n0))] if two_x else [tile(lambda i: (i, 0))]
    return pl.pallas_call(
        functools.partial(_glu_kernel, n0=n0, two_x=two_x), grid=(t // tm,),
        in_specs=[tile(_lo_map(n0)), tile(_hi_map(n0))] + x_specs + [pl.BlockSpec((D, 2 * D), lambda i: (0, 0))],
        out_specs=tile(lambda i: (i, 0)),
        out_shape=jax.ShapeDtypeStruct((t, D), F32),
        compiler_params=_cparams(("parallel",)), name="glu_residual",
    )(g_p, g_s, *x_parts, w_glu)


def _s5_layer(x_parts, norm_g, state_re, state_im, params, dims, h_parts=None):
    n_bp, seq_p, n_bs, seq_s = dims
    tp, ts = n_bp * seq_p, n_bs * seq_s
    lam_re, lam_im, log_dt, b_re, b_im, c_re, c_im, d, w_glu = params
    tables = _s5_weights(lam_re, lam_im, log_dt, b_re, b_im, c_re, c_im, d)
    if h_parts is not None:
        h_p, h_s = h_parts
    elif len(x_parts) == 2:
        h_p = _rmsnorm(x_parts[0], norm_g, 0, tp)
        h_s = _rmsnorm(x_parts[1], norm_g, 0, ts)
    else:
        h_p = _rmsnorm(x_parts[0], norm_g, 0, tp)
        h_s = _rmsnorm(x_parts[0], norm_g, tp, ts)
    zeros = jnp.zeros((N_BUNDLES, n_bp, 2 * S5_BUNDLE * S5_STATE), F32)
    cb = math.gcd(seq_p // S5_T, max(1, 512 // n_bp))
    g_p, s_p = _s5_scan(h_p.reshape(n_bp, seq_p, D), zeros, tables, cb)
    g_s, s_s = _s5_scan(h_s.reshape(n_bs, seq_s, D), _state_to_bundles(state_re, state_im), tables, seq_s // S5_T)
    x = _glu_residual(g_p.reshape(tp, D), g_s.reshape(ts, D), x_parts, w_glu.astype(BF16))
    p_re, p_im = _state_from_bundles(s_p)
    s_re, s_im = _state_from_bundles(s_s)
    return x, (p_re, s_re, p_im, s_im)


def _ffn_kernel(x_ref, gn_ref, win_ref, wo_ref, *rest, nf, n0):
    d_ff = wo_ref.shape[0]
    tf = d_ff // nf
    x = x_ref[...]
    h = _rms(x, gn_ref[...]).astype(BF16)
    y = x
    for f in range(nf):
        cols = slice(f * tf, (f + 1) * tf)
        up = slice(d_ff + f * tf, d_ff + (f + 1) * tf)
        t = (jax.nn.silu(_dot(h, win_ref[:, cols])) * _dot(h, win_ref[:, up])).astype(BF16)
        y = y + _dot(t, wo_ref[cols, :])
    if n0 is None:
        rest[0][...] = y
        return
    gnext_ref, o_ref, hp_ref, hs_ref = rest
    o_ref[...] = y
    hn = _rms(y, gnext_ref[...])
    first = pl.program_id(0) < n0

    @pl.when(first)
    def _():
        hp_ref[...] = hn

    @pl.when(jnp.logical_not(first))
    def _():
        hs_ref[...] = hn


def _ffn_dense(x, norm_g, w_in, w_out, next_norm=None, split_rows=None, tm=512, nf=2):
    t = x.shape[0]
    tile = lambda m: pl.BlockSpec((tm, D), m)
    vec = pl.BlockSpec((1, D), lambda i: (0, 0))
    resident = lambda a: pl.BlockSpec(a.shape, lambda i: (0, 0), pipeline_mode=pl.Buffered(1))
    w_in, w_out = w_in.astype(BF16), w_out.astype(BF16)
    args = [x, norm_g.reshape(1, D), w_in, w_out]
    in_specs = [tile(lambda i: (i, 0)), vec, resident(w_in), resident(w_out)]
    out_specs, out_shape, n0 = tile(lambda i: (i, 0)), jax.ShapeDtypeStruct((t, D), F32), None
    if next_norm is not None:
        n0 = split_rows // tm
        args.append(next_norm.reshape(1, D))
        in_specs.append(vec)
        out_specs = [out_specs, tile(_lo_map(n0)), tile(_hi_map(n0))]
        out_shape = [out_shape, jax.ShapeDtypeStruct((split_rows, D), F32),
                     jax.ShapeDtypeStruct((t - split_rows, D), F32)]
    return pl.pallas_call(
        functools.partial(_ffn_kernel, nf=nf, n0=n0), grid=(t // tm,),
        in_specs=in_specs, out_specs=out_specs, out_shape=out_shape,
        compiler_params=_cparams(("arbitrary",)), name="ffn_dense",
    )(*args)


def _swap_halves(w):
    half = w.shape[-1] // 2
    return jnp.concatenate([w[..., half:], w[..., :half]], axis=-1)


def _pad_lanes(w, n=128):
    return jnp.concatenate([w, jnp.zeros(w.shape[:-1] + (n - w.shape[-1],), w.dtype)], axis=-1)


def _rope_tables(seq_p, seq_s, past_len, rows_s):
    half = QK_ROPE // 2
    inv_freq = ROPE_THETA ** (-jnp.arange(half, dtype=F32) / half)
    pos = jnp.concatenate([jnp.arange(seq_p), past_len + (jnp.arange(rows_s) % seq_s)])
    ang = pos.astype(F32)[:, None] * inv_freq[None, :]
    cos, sin = jnp.cos(ang), jnp.sin(ang)
    return _pad_lanes(jnp.concatenate([cos, cos], axis=1)), _pad_lanes(jnp.concatenate([-sin, sin], axis=1))


def _rope_norm(x, x_sw, g, g_sw, cos, sin):
    r = lax.rsqrt(jnp.sum(x * x, axis=-1, keepdims=True) * (1.0 / QK_ROPE) + NORM_EPS)
    return r * (x * g * cos + x_sw * g_sw * sin)


def _mla_proj_kernel(x_ref, gmix, win, nq, nkv, cos, sin, gkr, gkrs, wq, gqn, gqr, gqrs, wuk, gkn,
                     cp_out, cs_out, krp_out, krs_out, cbt_out, q_out, k_out, *, n0):
    is_prompt = pl.program_id(0) < n0
    hn = _rms(x_ref[...], gmix[...]).astype(BF16)
    a = _dot(hn, win[...])
    hq = _rms(a[:, :Q_RANK], nq[...]).astype(BF16)
    c = _rms(a[:, Q_RANK:Q_RANK + KV_RANK], nkv[...])
    cb = c.astype(BF16)
    kr = _rope_norm(a[:, 768:896], a[:, 896:1024], gkr[...], gkrs[...], cos[...], sin[...])
    krb = kr.astype(BF16)

    @pl.when(is_prompt)
    def _():
        cp_out[...] = c
        krp_out[...] = kr
        cbt_out[...] = c.T.astype(BF16)

    @pl.when(jnp.logical_not(is_prompt))
    def _():
        cs_out[...] = c
        krs_out[...] = kr

    q_scale = jnp.where(is_prompt, ATTN_SCALE * LOG2E, 1.0)
    for h in range(N_HEADS):
        q3 = _dot(hq, wq[:, h * 384:(h + 1) * 384])
        qn = _rms(q3[:, :128], gqn[...])
        qr = _rope_norm(q3[:, 128:256], q3[:, 256:384], gqr[...], gqrs[...], cos[...], sin[...])
        q_out[h] = (jnp.concatenate([qn, qr], axis=1) * q_scale).astype(BF16)

    @pl.when(is_prompt)
    def _():
        for h in range(N_HEADS):
            kn = _rms(_dot(cb, wuk[:, h * QK_NOPE:(h + 1) * QK_NOPE]), gkn[...])
            k_out[h] = jnp.concatenate([kn.astype(BF16), krb], axis=1)


def _mla_proj(x, norm_g, params, dims, past_len, tm=512):
    n_bp, seq_p, n_bs, seq_s = dims
    t = x.shape[0]
    tp = n_bp * seq_p
    ts = t - tp
    w_in, norm_q, norm_kv, w_q, w_uk, w_uv, qn_nope, qn_rope, kn_nope, kn_rope, w_o = params
    w_kr = w_in[:, Q_RANK + KV_RANK:]
    win = jnp.concatenate([w_in[:, :Q_RANK + KV_RANK], _pad_lanes(w_kr), _pad_lanes(_swap_halves(w_kr))],
                          axis=1).astype(BF16)
    wq3 = w_q.reshape(Q_RANK, N_HEADS, QK_NOPE + QK_ROPE)
    wq_r = wq3[:, :, QK_NOPE:]
    wq = jnp.concatenate([wq3[:, :, :QK_NOPE], _pad_lanes(wq_r), _pad_lanes(_swap_halves(wq_r))], axis=2)
    wq = wq.reshape(Q_RANK, N_HEADS * 384).astype(BF16)
    wuk = w_uk.reshape(KV_RANK, N_HEADS * QK_NOPE).astype(BF16)
    cos, sin = _rope_tables(seq_p, seq_s, past_len, min(tm, ts))
    n_pt = seq_p // tm
    n0 = tp // tm
    tab = lambda i: (jnp.where(i < n0, i % n_pt, n_pt), 0)
    row = lambda v: v.reshape(1, -1).astype(F32)
    vec = lambda n: pl.BlockSpec((1, n), lambda i: (0, 0))
    full = lambda a: pl.BlockSpec(a.shape, lambda i: (0, 0))
    lo = lambda n: pl.BlockSpec((tm, n), _lo_map(n0))
    hi = lambda n: pl.BlockSpec((tm, n), _hi_map(n0))
    sds = jax.ShapeDtypeStruct
    return pl.pallas_call(
        functools.partial(_mla_proj_kernel, n0=n0), grid=(t // tm,),
        in_specs=[pl.BlockSpec((tm, D), lambda i: (i, 0)), vec(D), full(win), vec(Q_RANK), vec(KV_RANK),
                  pl.BlockSpec((tm, 128), tab), pl.BlockSpec((tm, 128), tab), vec(128), vec(128),
                  full(wq), vec(128), vec(128), vec(128), full(wuk), vec(128)],
        out_specs=[lo(KV_RANK), hi(KV_RANK), lo(128), hi(128),
                   pl.BlockSpec((KV_RANK, tm), lambda i: (0, jnp.minimum(i, n0 - 1))),
                   pl.BlockSpec((N_HEADS, tm, 256), lambda i: (0, i, 0)),
                   pl.BlockSpec((N_HEADS, tm, 256), lambda i: (0, jnp.minimum(i, n0 - 1), 0))],
        out_shape=[sds((tp, KV_RANK), F32), sds((ts, KV_RANK), F32), sds((tp, 128), F32), sds((ts, 128), F32),
                   sds((KV_RANK, tp), BF16), sds((N_HEADS, t, 256), BF16), sds((N_HEADS, tp, 256), BF16)],
        compiler_params=_cparams(("arbitrary",)), name="mla_proj",
    )(x, row(norm_g), win, row(norm_q), row(norm_kv), cos, sin,
      row(_pad_lanes(kn_rope)), row(_pad_lanes(_swap_halves(kn_rope))), wq, row(qn_nope),
      row(_pad_lanes(qn_rope)), row(_pad_lanes(_swap_halves(qn_rope))), wuk, row(kn_nope))


def _flash_kernel(qi_tab, ki_tab, q_ref, k_ref, vt_ref, o_ref, m, l, acc):
    step = pl.program_id(1)
    qi, ki = qi_tab[step], ki_tab[step]

    @pl.when(ki == 0)
    def _():
        m[...] = jnp.full_like(m, NEG_INF)
        l[...] = jnp.zeros_like(l)
        acc[...] = jnp.zeros_like(acc)

    def update(h, masked):
        s = _dot_nt(k_ref[h], q_ref[h])
        if masked:
            key = lax.broadcasted_iota(I32, s.shape, 0)
            qry = lax.broadcasted_iota(I32, s.shape, 1)
            s = jnp.where(key <= qry, s, NEG_INF)
        m_new = jnp.maximum(m[h], jnp.max(s, axis=0, keepdims=True))
        corr = jnp.exp2(m[h] - m_new)
        p = jnp.exp2(s - m_new)
        l[h] = l[h] * corr + jnp.sum(p, axis=0, keepdims=True)
        acc[h] = acc[h] * corr + _dot(vt_ref[...], p.astype(BF16))
        m[h] = m_new

    def heads(masked):
        def pair(hp, carry):
            update(2 * hp, masked)
            update(2 * hp + 1, masked)
            return carry
        lax.fori_loop(0, N_HEADS // 2, pair, 0)

    @pl.when(ki < qi)
    def _():
        heads(False)

    @pl.when(ki == qi)
    def _():
        heads(True)
        for h in range(N_HEADS):
            o_ref[h] = (acc[h] / l[h]).T.astype(o_ref.dtype)


def _mla_prompt_attn(qcat, kcat, cbt, n_b, seq, tq=512):
    nq = seq // tq
    pairs = [(qi, ki) for qi in range(nq) for ki in range(qi + 1)]
    qi_tab = jnp.array([p[0] for p in pairs], I32)
    ki_tab = jnp.array([p[1] for p in pairs], I32)
    grid_spec = pltpu.PrefetchScalarGridSpec(
        num_scalar_prefetch=2, grid=(n_b, len(pairs)),
        in_specs=[pl.BlockSpec((N_HEADS, tq, 256), lambda b, s, qt, kt: (0, b * nq + qt[s], 0)),
                  pl.BlockSpec((N_HEADS, tq, 256), lambda b, s, qt, kt: (0, b * nq + kt[s], 0)),
                  pl.BlockSpec((KV_RANK, tq), lambda b, s, qt, kt: (0, b * nq + kt[s]))],
        out_specs=pl.BlockSpec((N_HEADS, tq, KV_RANK), lambda b, s, qt, kt: (0, b * nq + qt[s], 0)),
        scratch_shapes=[pltpu.VMEM((N_HEADS, 1, tq), F32), pltpu.VMEM((N_HEADS, 1, tq), F32),
                        pltpu.VMEM((N_HEADS, KV_RANK, tq), F32)])
    return pl.pallas_call(
        _flash_kernel, grid_spec=grid_spec,
        out_shape=jax.ShapeDtypeStruct((N_HEADS, n_b * seq, KV_RANK), BF16),
        compiler_params=_cparams(("parallel", "arbitrary")), name="mla_prompt_attn",
    )(qi_tab, ki_tab, qcat, kcat, cbt)


def _sample_attn_kernel(pt_ref, lat_hbm, krt_hbm, cnew, krtnew, q_ref, wuk_ref, wukt_ref, gkn_ref, o_ref,
                        lat_buf, krt_buf, sem, lhs, m, l, acc, *, n_b, npg, n_chunks, dq):
    b, kc = pl.program_id(0), pl.program_id(1)
    nhq = N_HEADS * dq
    nkd = N_HEADS * QK_NOPE
    slot = (b * n_chunks + kc) % 2

    def page_copies(bb, cc, s):
        copies = []
        for j in range(npg):
            pid = pt_ref[bb, cc * npg + j]
            copies.append(pltpu.make_async_copy(lat_hbm.at[pid], lat_buf.at[s, j], sem.at[0, s]))
            copies.append(pltpu.make_async_copy(krt_hbm.at[pid], krt_buf.at[s, j], sem.at[1, s]))
        return copies

    @pl.when(kc == 0)
    def _():
        m[...] = jnp.full_like(m, NEG_INF)
        l[...] = jnp.zeros_like(l)
        acc[...] = jnp.zeros_like(acc)
        lhs[0:nkd, :] = wukt_ref[...]
        qn = (q_ref[:, :QK_NOPE] * gkn_ref[...]).astype(BF16)
        qabs = [_dot_nt(qn[h * dq:(h + 1) * dq], wuk_ref[h]) for h in range(N_HEADS)]
        lhs[nkd:nkd + nhq, :] = jnp.concatenate(qabs, axis=0).astype(BF16)

    def attend(c_pages, krt_pages, mask):
        ct = jnp.concatenate([c.T for c in c_pages], axis=1).astype(BF16)
        cb = jnp.concatenate(c_pages, axis=0).astype(BF16)
        kt = jnp.concatenate(krt_pages, axis=1).astype(BF16)
        nk = ct.shape[1]
        r = _dot(lhs[...], ct)
        rinv = []
        for h in range(N_HEADS):
            kn = r[h * QK_NOPE:(h + 1) * QK_NOPE]
            ssq = jnp.sum(kn * kn, axis=0, keepdims=True)
            rinv.append(jnp.broadcast_to(lax.rsqrt(ssq * (1.0 / QK_NOPE) + NORM_EPS), (dq, nk)))
        qr = q_ref[:, QK_NOPE:QK_NOPE + QK_ROPE].astype(BF16)
        s = (r[nkd:] * jnp.concatenate(rinv, axis=0) + _dot(qr, kt)) * ATTN_SCALE
        if mask is not None:
            s = jnp.where(mask, s, NEG_INF)
        m_new = jnp.maximum(m[...], jnp.max(s, axis=-1, keepdims=True))
        corr = jnp.exp(m[...] - m_new)
        p = jnp.exp(s - m_new)
        l[...] = l[...] * corr + jnp.sum(p, axis=-1, keepdims=True)
        acc[...] = acc[...] * corr + _dot(p.astype(BF16), cb)
        m[...] = m_new

    @pl.when(kc < n_chunks)
    def _():
        @pl.when((b == 0) & (kc == 0))
        def _():
            for c in page_copies(b, kc, slot):
                c.start()

        last_chunk = kc + 1 == n_chunks
        nxt_b = jnp.where(last_chunk, b + 1, b)
        nxt_c = jnp.where(last_chunk, 0, kc + 1)

        @pl.when(nxt_b < n_b)
        def _():
            for c in page_copies(nxt_b, nxt_c, 1 - slot):
                c.start()

        for c in page_copies(b, kc, slot):
            c.wait()
        group = min(npg, ATTEND_PAGES)
        for j0 in range(0, npg, group):
            pages = range(j0, j0 + group)
            attend([lat_buf[slot, j] for j in pages], [krt_buf[slot, j] for j in pages], None)

    @pl.when(kc == n_chunks)
    def _():
        key = lax.broadcasted_iota(I32, (nhq, PAGE), 1)
        qpos = lax.broadcasted_iota(I32, (nhq, PAGE), 0) % dq
        attend([cnew[...]], [krtnew[...]], key <= qpos)
        o_ref[...] = (acc[...] / l[...]).astype(o_ref.dtype)


def _mla_sample_attn(q_s, c_new, kr_new, cache_latent, cache_krope, page_table, wuk, gkn, n_b, dq):
    n_pages = page_table.shape[1]
    npg = math.gcd(n_pages, PAGES_PER_STEP)
    n_chunks = n_pages // npg
    nhq = N_HEADS * dq
    nkd = N_HEADS * QK_NOPE
    wukt = wuk.transpose(0, 2, 1).reshape(nkd, KV_RANK)
    pad = lambda a: jnp.concatenate([a, jnp.zeros((n_b, PAGE - dq, a.shape[-1]), a.dtype)], axis=1)
    krt_cache = cache_krope.swapaxes(1, 2)
    krt_new = pad(kr_new).swapaxes(1, 2)
    per_b = lambda r, w: pl.BlockSpec((None, r, w), lambda b, kc, pt: (b, 0, 0))
    grid_spec = pltpu.PrefetchScalarGridSpec(
        num_scalar_prefetch=1, grid=(n_b, n_chunks + 1),
        in_specs=[pl.BlockSpec(memory_space=pl.ANY), pl.BlockSpec(memory_space=pl.ANY),
                  per_b(PAGE, KV_RANK), per_b(QK_ROPE, PAGE), per_b(nhq, 256),
                  pl.BlockSpec((N_HEADS, KV_RANK, QK_NOPE), lambda b, kc, pt: (0, 0, 0)),
                  pl.BlockSpec((nkd, KV_RANK), lambda b, kc, pt: (0, 0)),
                  pl.BlockSpec((1, QK_NOPE), lambda b, kc, pt: (0, 0))],
        out_specs=per_b(nhq, KV_RANK),
        scratch_shapes=[pltpu.VMEM((2, npg, PAGE, KV_RANK), F32), pltpu.VMEM((2, npg, QK_ROPE, PAGE), F32),
                        pltpu.SemaphoreType.DMA((2, 2)), pltpu.VMEM((nkd + nhq, KV_RANK), BF16),
                        pltpu.VMEM((nhq, 1), F32), pltpu.VMEM((nhq, 1), F32), pltpu.VMEM((nhq, KV_RANK), F32)])
    return pl.pallas_call(
        functools.partial(_sample_attn_kernel, n_b=n_b, npg=npg, n_chunks=n_chunks, dq=dq), grid_spec=grid_spec,
        out_shape=jax.ShapeDtypeStruct((n_b, nhq, KV_RANK), BF16),
        compiler_params=_cparams(("arbitrary", "arbitrary")), name="mla_sample_attn",
    )(page_table, cache_latent, krt_cache, pad(c_new), krt_new, q_s,
      wuk, wukt, gkn.reshape(1, QK_NOPE).astype(F32))


def _mla_out_kernel(op_ref, os_ref, x_ref, wuv_ref, wo_ref, out_ref, *, n0):
    first = pl.program_id(0) < n0
    v = jnp.concatenate([_dot(jnp.where(first, op_ref[h], os_ref[h]), wuv_ref[h]) for h in range(N_HEADS)], axis=1)
    out_ref[...] = x_ref[...] + _dot(v.astype(BF16), wo_ref[...])


def _mla_out(o_p, o_s, x, w_uv, w_o, tm=512):
    t = x.shape[0]
    n0 = o_p.shape[1] // tm
    return pl.pallas_call(
        functools.partial(_mla_out_kernel, n0=n0), grid=(t // tm,),
        in_specs=[pl.BlockSpec((N_HEADS, tm, KV_RANK), lambda i: (0, jnp.minimum(i, n0 - 1), 0)),
                  pl.BlockSpec((N_HEADS, tm, KV_RANK), lambda i: (0, jnp.maximum(i - n0, 0), 0)),
                  pl.BlockSpec((tm, D), lambda i: (i, 0)),
                  pl.BlockSpec((N_HEADS, KV_RANK, 128), lambda i: (0, 0, 0)), pl.BlockSpec((D, D), lambda i: (0, 0))],
        out_specs=pl.BlockSpec((tm, D), lambda i: (i, 0)),
        out_shape=jax.ShapeDtypeStruct((t, D), F32),
        compiler_params=_cparams(("parallel",)), name="mla_out",
    )(o_p, o_s, x, w_uv.transpose(1, 0, 2).astype(BF16), w_o.astype(BF16))


def _mla_layer(x, norm_g, cache_latent, cache_krope, page_table, params, dims):
    n_bp, seq_p, n_bs, seq_s = dims
    tp = n_bp * seq_p
    past_len = page_table.shape[1] * cache_latent.shape[1]
    w_uk, w_uv, kn_nope, w_o = params[4], params[5], params[8], params[10]
    c_p, c_s, kr_p, kr_s, cbt, qcat, kcat = _mla_proj(x, norm_g, params, dims, past_len)
    c_s = c_s.reshape(n_bs, seq_s, KV_RANK)
    kr_s = kr_s[:, :QK_ROPE].reshape(n_bs, seq_s, QK_ROPE)
    o_p = _mla_prompt_attn(qcat, kcat, cbt, n_bp, seq_p)
    q_s = qcat[:, tp:].astype(F32).reshape(N_HEADS, n_bs, seq_s, 256).transpose(1, 0, 2, 3)
    q_s = q_s.reshape(n_bs, N_HEADS * seq_s, 256)
    o_s = _mla_sample_attn(q_s, c_s, kr_s, cache_latent, cache_krope, page_table,
                           w_uk.transpose(1, 0, 2).astype(BF16), kn_nope, n_bs, seq_s)
    o_s = o_s.reshape(n_bs, N_HEADS, seq_s, KV_RANK).transpose(1, 0, 2, 3).reshape(N_HEADS, n_bs * seq_s, KV_RANK)
    x = _mla_out(o_p, o_s, x, w_uv, w_o)
    state = (c_p.reshape(n_bp, seq_p, KV_RANK), c_s, kr_p[:, :QK_ROPE].reshape(n_bp, seq_p, QK_ROPE), kr_s)
    return x, state


def _pool_kernel(x_ref, prev_ref, gn_ref, w_ref, sc_ref, o_ref, st_ref, ext, *, tl, ns, nt, start):
    ti = pl.program_id(1)
    for s in range(ns):
        rows = slice(s * tl, (s + 1) * tl)
        if nt == 1:
            ext[s, 0:POOL_HALO, :] = prev_ref[s]
        else:
            @pl.when(ti == 0)
            def _():
                ext[s, 0:POOL_HALO, :] = prev_ref[s]

            @pl.when(ti > 0)
            def _():
                ext[s, 0:POOL_HALO, :] = ext[s, tl:tl + POOL_HALO, :]

        x = x_ref[rows, :]
        h = _rms(x, gn_ref[...])
        ext[s, POOL_HALO:POOL_HALO + tl, :] = h
        pos = start + ti * tl + lax.broadcasted_iota(I32, (tl, 1), 0)
        run = ext[s]
        z = []
        for g, w in enumerate(POOL_WINDOWS):
            run = run[:, (POOL_GROUP if g else 0):]
            run = run + pltpu.roll(run, w // 2, 0)
            total = run[POOL_HALO:POOL_HALO + tl, :POOL_GROUP]
            count = jnp.minimum(w, pos + 1).astype(F32)
            p = total / count - h[:, g * POOL_GROUP:(g + 1) * POOL_GROUP]
            z.append(_dot(p.astype(BF16), w_ref[g]))
        o_ref[rows, :] = x + jnp.concatenate(z, axis=1) * sc_ref[...]

        if nt == 1:
            st_ref[s] = ext[s, tl:tl + POOL_HALO, :]
        else:
            @pl.when(ti == nt - 1)
            def _():
                st_ref[s] = ext[s, tl:tl + POOL_HALO, :]


def _pool_stream(x, prev, norm_g, pool_w, pool_scale, n_b, seq, row0, start, tl, ns):
    nt = seq // tl
    blk0 = row0 // (ns * tl)
    prev = jnp.concatenate([jnp.zeros((n_b, 1, D), F32), prev.astype(F32)], axis=1)
    rows = lambda b, t: (blk0 + b * nt + t, 0)
    out, st = pl.pallas_call(
        functools.partial(_pool_kernel, tl=tl, ns=ns, nt=nt, start=start), grid=(n_b // ns, nt),
        in_specs=[pl.BlockSpec((ns * tl, D), rows),
                  pl.BlockSpec((ns, POOL_HALO, D), lambda b, t: (b, 0, 0)),
                  pl.BlockSpec((1, D), lambda b, t: (0, 0)),
                  pl.BlockSpec((4, POOL_GROUP, POOL_GROUP), lambda b, t: (0, 0, 0)),
                  pl.BlockSpec((1, D), lambda b, t: (0, 0))],
        out_specs=[pl.BlockSpec((ns * tl, D), rows),
                   pl.BlockSpec((ns, POOL_HALO, D), lambda b, t: (b, 0, 0))],
        out_shape=[jax.ShapeDtypeStruct(x.shape, F32), jax.ShapeDtypeStruct((n_b, POOL_HALO, D), F32)],
        scratch_shapes=[pltpu.VMEM((ns, POOL_HALO + tl, D), F32)],
        input_output_aliases={0: 0},
        compiler_params=_cparams(("parallel", "arbitrary")), name="pool_mixer",
    )(x, prev, norm_g.reshape(1, D), pool_w.astype(BF16), pool_scale.reshape(1, D))
    return out, st[:, 1:]


def _pool_layer(x, norm_g, state_pool, params, dims, past_len):
    n_bp, seq_p, n_bs, seq_s = dims
    tp = n_bp * seq_p
    pool_w, pool_scale = params
    zero = jnp.zeros((n_bp, POOL_HALO - 1, D), F32)
    x, st_p = _pool_stream(x, zero, norm_g, pool_w, pool_scale, n_bp, seq_p, 0, 0, min(512, seq_p), 1)
    x, st_s = _pool_stream(x, state_pool, norm_g, pool_w, pool_scale, n_bs, seq_s, tp, past_len, seq_s,
                           math.gcd(n_bs, 8))
    return x, (st_p, st_s)


def _router_kernel(x_ref, gn_ref, wr_ref, tri_ref, hb_ref, sel_ref, gate_ref, cnt_ref):
    h = _rms(x_ref[...], gn_ref[...])
    hb_ref[...] = h.astype(BF16)
    logits = lax.dot_general(wr_ref[...], h, (((1,), (1,)), ((), ())), precision=lax.Precision.HIGHEST,
                             preferred_element_type=F32)
    idx = lax.broadcasted_iota(I32, logits.shape, 0)
    m1 = jnp.max(logits, axis=0, keepdims=True)
    i1 = jnp.min(jnp.where(logits == m1, idx, N_EXPERTS), axis=0, keepdims=True)
    first = idx == i1
    rest = jnp.where(first, -jnp.inf, logits)
    m2 = jnp.max(rest, axis=0, keepdims=True)
    i2 = jnp.min(jnp.where(rest == m2, idx, N_EXPERTS), axis=0, keepdims=True)
    second = idx == i2
    e2 = jnp.exp(m2 - m1)
    g1 = 1.0 / (1.0 + e2)
    g2 = e2 / (1.0 + e2)
    chosen = jnp.where(first | second, 1.0, 0.0)
    rank = _dot(chosen.astype(BF16), tri_ref[...])
    rank1 = jnp.sum(jnp.where(first, rank, 0.0), axis=0, keepdims=True)
    rank2 = jnp.sum(jnp.where(second, rank, 0.0), axis=0, keepdims=True)
    zero = jnp.zeros((N_EXPERTS - 4,) + i1.shape[1:], I32)
    sel_ref[...] = jnp.concatenate([i1, i2, rank1.astype(I32), rank2.astype(I32), zero], axis=0)
    gate_ref[...] = jnp.concatenate([g1, g2, jnp.zeros((N_EXPERTS - 2,) + g1.shape[1:], F32)], axis=0)
    cnt_ref[...] = jnp.broadcast_to(jnp.sum(chosen, axis=1, keepdims=True), cnt_ref.shape).astype(I32)


def _moe_route(x, norm_g, w_router):
    t = x.shape[0]
    tb = MOE_TB
    nblk = t // tb
    tri = (jnp.arange(tb)[:, None] < jnp.arange(tb)[None, :]).astype(BF16)
    blk = lambda dt: (pl.BlockSpec((None, N_EXPERTS, tb), lambda i: (i, 0, 0)),
                      jax.ShapeDtypeStruct((nblk, N_EXPERTS, tb), dt))
    (sel_spec, sel_shape), (gate_spec, gate_shape) = blk(I32), blk(F32)
    return pl.pallas_call(
        _router_kernel, grid=(nblk,),
        in_specs=[pl.BlockSpec((tb, D), lambda i: (i, 0)), pl.BlockSpec((1, D), lambda i: (0, 0)),
                  pl.BlockSpec((N_EXPERTS, D), lambda i: (0, 0)), pl.BlockSpec((tb, tb), lambda i: (0, 0))],
        out_specs=[pl.BlockSpec((tb, D), lambda i: (i, 0)), sel_spec, gate_spec,
                   pl.BlockSpec((None, N_EXPERTS, 128), lambda i: (i, 0, 0))],
        out_shape=[jax.ShapeDtypeStruct((t, D), BF16), sel_shape, gate_shape,
                   jax.ShapeDtypeStruct((nblk, N_EXPERTS, 128), I32)],
        compiler_params=_cparams(("parallel",)), name="moe_router",
    )(x, norm_g.reshape(1, D), w_router.T.astype(F32), tri)


def _moe_rows(t):
    nblk = t // MOE_TB
    worst = 2 * t + nblk * N_EXPERTS * (MOE_ALIGN - 1) + N_EXPERTS * (MOE_TM - 1)
    return (worst + MOE_TM - 1) // MOE_TM * MOE_TM


def _moe_plan(sel, counts, t):
    cnt = counts[:, :, 0]
    cpad = (cnt + MOE_ALIGN - 1) // MOE_ALIGN * MOE_ALIGN
    per_e = jnp.sum(cpad, axis=0)
    per_e_pad = (per_e + MOE_TM - 1) // MOE_TM * MOE_TM
    e_start = jnp.cumsum(per_e_pad) - per_e_pad
    off = e_start[None, :] + jnp.cumsum(cpad, axis=0) - cpad
    lend = jnp.cumsum(cpad, axis=1)
    lstart = lend - cpad
    e1, e2, r1, r2 = sel[:, 0], sel[:, 1], sel[:, 2], sel[:, 3]
    experts = jnp.arange(N_EXPERTS)
    pick = lambda tab, e: jnp.sum(jnp.where(e[:, :, None] == experts, tab[:, None, :], 0), axis=2)
    dstl = jnp.stack([pick(lstart, e1) + r1, pick(lstart, e2) + r2], axis=1).astype(I32)
    prow = jnp.arange(MOE_PIECES)[None, :] * MOE_ALIGN
    pe = jnp.minimum(jnp.sum(prow[:, :, None] >= lend[:, None, :], axis=2), N_EXPERTS - 1)
    grow = pick(off, pe) + prow - pick(lstart, pe)
    n_pieces = lend[:, -1] // MOE_ALIGN
    grow = jnp.where(jnp.arange(MOE_PIECES)[None, :] < n_pieces[:, None], grow, 0)
    n_tiles = _moe_rows(t) // MOE_TM
    e_end = e_start + per_e_pad
    tile_e = jnp.sum((jnp.arange(n_tiles) * MOE_TM)[:, None] >= e_end[None, :], axis=1)
    active = (tile_e < N_EXPERTS).astype(I32)
    tile_e = jnp.where(active == 1, tile_e, N_EXPERTS - 1)
    return dict(dstl=dstl, grow=grow.astype(I32).reshape(-1), n_pieces=n_pieces.astype(I32),
                tile_e=tile_e.astype(I32), active=active)


def _dispatch_kernel(np_ref, grow_ref, hb_ref, dst_ref, gate_ref, xs_in, xs_ref, buf, sem):
    del xs_in
    b = pl.program_id(0)
    nblk = pl.num_programs(0)
    slot = b % 2

    def copy(blk, i, s):
        r = pl.multiple_of(grow_ref[blk * MOE_PIECES + i], MOE_ALIGN)
        src = buf.at[s, pl.ds(pl.multiple_of(i * MOE_ALIGN, MOE_ALIGN), MOE_ALIGN), :]
        return pltpu.make_async_copy(src, xs_ref.at[pl.ds(r, MOE_ALIGN), :], sem.at[s])

    def drain(blk, s):
        lax.fori_loop(0, np_ref[blk], lambda i, c: (copy(blk, i, s).wait(), c)[1], 0)

    @pl.when(b >= 2)
    def _():
        drain(b - 2, slot)

    want = lax.broadcasted_iota(I32, (MOE_KL, 1), 0)
    hit1 = dst_ref[0:1, :] == want
    hit2 = dst_ref[1:2, :] == want
    onehot = jnp.where(hit1 | hit2, 1.0, 0.0).astype(BF16)
    buf[slot, :, 0:D] = _dot(onehot, hb_ref[...]).astype(BF16)
    gate = jnp.sum(jnp.where(hit1, gate_ref[0:1, :], 0.0) + jnp.where(hit2, gate_ref[1:2, :], 0.0),
                   axis=1, keepdims=True)
    g_hi = gate.astype(BF16).astype(F32)
    g_mid = (gate - g_hi).astype(BF16).astype(F32)
    g_lo = gate - g_hi - g_mid
    lane = lax.broadcasted_iota(I32, (MOE_KL, 128), 1)
    slab = jnp.where(lane == 0, g_hi, jnp.where(lane == 1, g_mid, jnp.where(lane == 2, g_lo, 0.0)))
    buf[slot, :, D:MOE_XW] = slab.astype(BF16)
    lax.fori_loop(0, np_ref[b], lambda i, c: (copy(b, i, slot).start(), c)[1], 0)

    @pl.when(b == nblk - 1)
    def _():
        @pl.when(b >= 1)
        def _():
            drain(b - 1, 1 - slot)
        drain(b, slot)


def _moe_dispatch(hb, gate, plan):
    t = hb.shape[0]
    nblk = t // MOE_TB
    rows = _moe_rows(t)
    grid_spec = pltpu.PrefetchScalarGridSpec(
        num_scalar_prefetch=2, grid=(nblk,),
        in_specs=[pl.BlockSpec((MOE_TB, D), lambda b, *_: (b, 0)),
                  pl.BlockSpec((None, 2, MOE_TB), lambda b, *_: (b, 0, 0)),
                  pl.BlockSpec((None, N_EXPERTS, MOE_TB), lambda b, *_: (b, 0, 0)),
                  pl.BlockSpec(memory_space=pl.ANY)],
        out_specs=pl.BlockSpec(memory_space=pl.ANY),
        scratch_shapes=[pltpu.VMEM((2, MOE_KL, MOE_XW), BF16), pltpu.SemaphoreType.DMA((2,))])
    return pl.pallas_call(
        _dispatch_kernel, grid_spec=grid_spec,
        out_shape=jax.ShapeDtypeStruct((rows, MOE_XW), BF16),
        input_output_aliases={5: 0},
        compiler_params=_cparams(("arbitrary",)), name="moe_dispatch",
    )(plan["n_pieces"], plan["grow"], hb, plan["dstl"], gate, jnp.zeros((rows, MOE_XW), BF16))


def _expert_kernel(te_ref, act_ref, x_ref, wg_ref, wu_ref, wo_ref, o_ref, acc, *, nf):
    i, f = pl.program_id(0), pl.program_id(1)

    @pl.when(f == 0)
    def _():
        acc[...] = jnp.zeros_like(acc)

    @pl.when(act_ref[i] == 1)
    def _():
        x = x_ref[:, 0:D]
        t = (jax.nn.silu(_dot(x, wg_ref[...])) * _dot(x, wu_ref[...])).astype(BF16)
        acc[...] += _dot(t, wo_ref[...])

    @pl.when(f == nf - 1)
    def _():
        gate = jnp.sum(x_ref[:, D:MOE_XW].astype(F32), axis=1, keepdims=True)
        o_ref[...] = (acc[...] * gate).astype(o_ref.dtype)


def _moe_experts(xs, w_in, w_out, plan, tf=512):
    rows = xs.shape[0]
    d_ff = w_out.shape[1]
    nf = d_ff // tf
    w_in, w_out = w_in.astype(BF16), w_out.astype(BF16)
    fe = lambda i, f, act: jnp.where(act[i] == 1, f, nf - 1)
    slab = lambda half: pl.BlockSpec((None, D, tf), lambda i, f, te, act: (te[i], 0, half * nf + fe(i, f, act)))
    grid_spec = pltpu.PrefetchScalarGridSpec(
        num_scalar_prefetch=2, grid=(rows // MOE_TM, nf),
        in_specs=[pl.BlockSpec((MOE_TM, MOE_XW), lambda i, f, te, act: (i, 0)),
                  slab(0), slab(1),
                  pl.BlockSpec((None, tf, D), lambda i, f, te, act: (te[i], fe(i, f, act), 0))],
        out_specs=pl.BlockSpec((MOE_TM, D), lambda i, f, te, act: (i, 0)),
        scratch_shapes=[pltpu.VMEM((MOE_TM, D), F32)])
    return pl.pallas_call(
        functools.partial(_expert_kernel, nf=nf), grid_spec=grid_spec,
        out_shape=jax.ShapeDtypeStruct((rows, D), BF16),
        compiler_params=_cparams(("parallel", "arbitrary")), name="moe_experts",
    )(plan["tile_e"], plan["active"], xs, w_in, w_in, w_out)


def _combine_kernel(np_ref, grow_ref, x_ref, dst_ref, ys_ref, *rest, n0):
    outs, (buf, sem) = rest[:-2], rest[-2:]
    b = pl.program_id(0)
    nblk = pl.num_programs(0)
    slot = b % 2

    def copy(blk, i, s):
        r = pl.multiple_of(grow_ref[blk * MOE_PIECES + i], MOE_ALIGN)
        dst = buf.at[s, pl.ds(pl.multiple_of(i * MOE_ALIGN, MOE_ALIGN), MOE_ALIGN), :]
        return pltpu.make_async_copy(ys_ref.at[pl.ds(r, MOE_ALIGN), :], dst, sem.at[s])

    def fetch(blk, s):
        lax.fori_loop(0, np_ref[blk], lambda i, c: (copy(blk, i, s).start(), c)[1], 0)

    @pl.when(b == 0)
    def _():
        buf[...] = jnp.zeros_like(buf)
        fetch(0, 0)

    @pl.when(b + 1 < nblk)
    def _():
        fetch(b + 1, 1 - slot)

    want = lax.broadcasted_iota(I32, (1, MOE_KL), 1)
    onehot = jnp.where((dst_ref[:, 0:1] == want) | (dst_ref[:, 1:2] == want), 1.0, 0.0).astype(BF16)
    lax.fori_loop(0, np_ref[b], lambda i, c: (copy(b, i, slot).wait(), c)[1], 0)
    y = x_ref[...] + _dot(onehot, buf[slot])
    if len(outs) == 1:
        outs[0][...] = y
    else:
        @pl.when(b < n0)
        def _():
            outs[0][...] = y

        @pl.when(b >= n0)
        def _():
            outs[1][...] = y


def _moe_combine(x, ys, dst_tok, plan, split_rows):
    t = x.shape[0]
    nblk = t // MOE_TB
    tile = lambda m: pl.BlockSpec((MOE_TB, D), m)
    if split_rows is None:
        n0 = nblk
        out_specs = tile(lambda b, *_: (b, 0))
        out_shape = jax.ShapeDtypeStruct((t, D), F32)
    else:
        n0 = split_rows // MOE_TB
        out_specs = [tile(lambda b, *_: (jnp.minimum(b, n0 - 1), 0)), tile(lambda b, *_: (jnp.maximum(b - n0, 0), 0))]
        out_shape = [jax.ShapeDtypeStruct((split_rows, D), F32), jax.ShapeDtypeStruct((t - split_rows, D), F32)]
    grid_spec = pltpu.PrefetchScalarGridSpec(
        num_scalar_prefetch=2, grid=(nblk,),
        in_specs=[tile(lambda b, *_: (b, 0)),
                  pl.BlockSpec((MOE_TB, 2), lambda b, *_: (b, 0)),
                  pl.BlockSpec(memory_space=pl.ANY)],
        out_specs=out_specs,
        scratch_shapes=[pltpu.VMEM((2, MOE_KL, D), BF16), pltpu.SemaphoreType.DMA((2,))])
    return pl.pallas_call(
        functools.partial(_combine_kernel, n0=n0), grid_spec=grid_spec, out_shape=out_shape,
        compiler_params=_cparams(("arbitrary",)), name="moe_combine",
    )(plan["n_pieces"], plan["grow"], x, dst_tok, ys)


def _moe_layer(x, norm_g, w_router, w_in, w_out, split_rows=None):
    t = x.shape[0]
    hb, sel, gate, counts = _moe_route(x, norm_g, w_router)
    plan = _moe_plan(sel, counts, t)
    xs = _moe_dispatch(hb, gate, plan)
    ys = _moe_experts(xs, w_in, w_out, plan)
    dst_tok = plan["dstl"].transpose(0, 2, 1).reshape(t, 2)
    return _moe_combine(x, ys, dst_tok, plan, split_rows)


def kernel(x_prompt, x_sample, state_l0_s5_re, state_l0_s5_im, cache_l1_latent, cache_l1_krope, state_l2_pool, state_l3_s5_re, state_l3_s5_im, page_table, l0_norm_mix, l0_s5_lambda_re, l0_s5_lambda_im, l0_s5_log_dt, l0_s5_b_re, l0_s5_b_im, l0_s5_c_re, l0_s5_c_im, l0_s5_d, l0_s5_w_glu, l0_norm_ffn, l0_ffn_w_in, l0_ffn_w_out, l1_norm_mix, l1_mla_w_in, l1_mla_norm_q, l1_mla_norm_kv, l1_mla_w_q, l1_mla_w_uk, l1_mla_w_uv, l1_mla_qn_nope, l1_mla_qn_rope, l1_mla_kn_nope, l1_mla_kn_rope, l1_mla_w_o, l1_norm_ffn, l1_moe_w_router, l1_moe_w_in, l1_moe_w_out, l2_norm_mix, l2_pool_w, l2_pool_scale, l2_norm_ffn, l2_ffn_w_in, l2_ffn_w_out, l3_norm_mix, l3_s5_lambda_re, l3_s5_lambda_im, l3_s5_log_dt, l3_s5_b_re, l3_s5_b_im, l3_s5_c_re, l3_s5_c_im, l3_s5_d, l3_s5_w_glu, l3_norm_ffn, l3_moe_w_router, l3_moe_w_in, l3_moe_w_out):
    n_bp, seq_p, _ = x_prompt.shape
    n_bs, seq_s, _ = x_sample.shape
    dims = (n_bp, seq_p, n_bs, seq_s)
    tp, ts = n_bp * seq_p, n_bs * seq_s
    past_len = page_table.shape[1] * cache_l1_latent.shape[1]

    x, st0 = _s5_layer((x_prompt.reshape(tp, D), x_sample.reshape(ts, D)), l0_norm_mix, state_l0_s5_re, state_l0_s5_im,
                       (l0_s5_lambda_re, l0_s5_lambda_im, l0_s5_log_dt, l0_s5_b_re, l0_s5_b_im,
                        l0_s5_c_re, l0_s5_c_im, l0_s5_d, l0_s5_w_glu), dims)
    x = _ffn_dense(x, l0_norm_ffn, l0_ffn_w_in, l0_ffn_w_out)

    x, st1 = _mla_layer(x, l1_norm_mix, cache_l1_latent, cache_l1_krope, page_table,
                        (l1_mla_w_in, l1_mla_norm_q, l1_mla_norm_kv, l1_mla_w_q, l1_mla_w_uk, l1_mla_w_uv,
                         l1_mla_qn_nope, l1_mla_qn_rope, l1_mla_kn_nope, l1_mla_kn_rope, l1_mla_w_o), dims)
    x = _moe_layer(x, l1_norm_ffn, l1_moe_w_router, l1_moe_w_in, l1_moe_w_out)

    x, st2 = _pool_layer(x, l2_norm_mix, state_l2_pool, (l2_pool_w, l2_pool_scale), dims, past_len)
    x, h_p, h_s = _ffn_dense(x, l2_norm_ffn, l2_ffn_w_in, l2_ffn_w_out, next_norm=l3_norm_mix, split_rows=tp)

    x, st3 = _s5_layer((x,), l3_norm_mix, state_l3_s5_re, state_l3_s5_im,
                       (l3_s5_lambda_re, l3_s5_lambda_im, l3_s5_log_dt, l3_s5_b_re, l3_s5_b_im,
                        l3_s5_c_re, l3_s5_c_im, l3_s5_d, l3_s5_w_glu), dims, h_parts=(h_p, h_s))
    y_p, y_s = _moe_layer(x, l3_norm_ffn, l3_moe_w_router, l3_moe_w_in, l3_moe_w_out, split_rows=tp)

    return (y_p.reshape(n_bp, seq_p, D), y_s.reshape(n_bs, seq_s, D)) + st0 + st1 + st2 + st3
```

```python
import functools
import math

import jax
import jax.numpy as jnp
from jax import lax
from jax.experimental import pallas as pl
from jax.experimental.pallas import tpu as pltpu

F32 = jnp.float32
BF16 = jnp.bfloat16
I32 = jnp.int32

D = 1024
NORM_EPS = 1e-6
NEG_INF = -1e30
S5_GROUP = 16
S5_STATE = 64
N_GROUPS = D // S5_GROUP
S5_T = 8
S5_BUNDLE = 8
N_BUNDLES = N_GROUPS // S5_BUNDLE
S5_W = S5_T * 128
N_HEADS = 8
QK_NOPE = 128
QK_ROPE = 64
Q_RANK = 512
KV_RANK = 256
ROPE_THETA = 10000.0
ATTN_SCALE = (QK_NOPE + QK_ROPE) ** -0.5
LOG2E = math.log2(math.e)
PAGE = 128
PAGES_PER_STEP = 32
ATTEND_PAGES = 16
POOL_WINDOWS = (2, 4, 8, 16)
POOL_HALO = 16
POOL_GROUP = D // 4
N_EXPERTS = 8
MOE_TB = 512
MOE_TM = 1024
MOE_ALIGN = 16
MOE_KL = (2 * MOE_TB + N_EXPERTS * (MOE_ALIGN - 1) + 127) // 128 * 128
MOE_PIECES = MOE_KL // MOE_ALIGN
MOE_XW = D + 128


def _cparams(sem, vmem_mb=48):
    return pltpu.CompilerParams(dimension_semantics=sem, vmem_limit_bytes=vmem_mb * 1024 * 1024)


def _rms(x, g):
    r = lax.rsqrt(jnp.mean(x * x, axis=-1, keepdims=True) + NORM_EPS)
    return x * r * g


def _dot(a, b):
    return jnp.dot(a, b, preferred_element_type=F32)


def _dot_nt(a, b):
    return lax.dot_general(a, b, (((1,), (1,)), ((), ())), preferred_element_type=F32)


def _lo_map(n):
    return lambda i: (jnp.minimum(i, n - 1), 0)


def _hi_map(n):
    return lambda i: (jnp.maximum(i - n, 0), 0)


def _rmsnorm_kernel(x_ref, g_ref, o_ref):
    o_ref[...] = _rms(x_ref[...], g_ref[...]).astype(o_ref.dtype)


def _rmsnorm(x, g, row0, rows, tm=1024):
    tm = min(tm, rows)
    blk0 = row0 // tm
    return pl.pallas_call(
        _rmsnorm_kernel, grid=(rows // tm,),
        in_specs=[pl.BlockSpec((tm, D), lambda i: (blk0 + i, 0)), pl.BlockSpec((1, D), lambda i: (0, 0))],
        out_specs=pl.BlockSpec((tm, D), lambda i: (i, 0)),
        out_shape=jax.ShapeDtypeStruct((rows, D), F32),
        compiler_params=_cparams(("parallel",)), name="rmsnorm",
    )(x, g.reshape(1, D))


def _s5_weights(lam_re, lam_im, log_dt, b_re, b_im, c_re, c_im, d):
    lr, li = lam_re.astype(F32), lam_im.astype(F32)
    dt = jnp.exp(log_dt.astype(F32))[:, None]
    mag = jnp.exp(lr * dt)
    ar, ai = mag * jnp.cos(li * dt), mag * jnp.sin(li * dt)
    nr, ni = ar - 1.0, ai
    den = lr * lr + li * li
    f_re = ((nr * lr + ni * li) / den)[..., None]
    f_im = ((ni * lr - nr * li) / den)[..., None]
    br, bi = b_re.astype(F32), b_im.astype(F32)
    bb_re, bb_im = f_re * br - f_im * bi, f_re * bi + f_im * br
    cr, ci = c_re.astype(F32), c_im.astype(F32)

    pr, pi = [jnp.ones_like(ar)], [jnp.zeros_like(ar)]
    for _ in range(S5_T):
        pr.append(pr[-1] * ar - pi[-1] * ai)
        pi.append(pr[-2] * ai + pi[-1] * ar)
    pr, pi = jnp.stack(pr), jnp.stack(pi)

    ab_re = pr[:S5_T, :, :, None] * bb_re - pi[:S5_T, :, :, None] * bb_im
    ab_im = pr[:S5_T, :, :, None] * bb_im + pi[:S5_T, :, :, None] * bb_re
    hi = lax.Precision.HIGHEST
    kern = (jnp.einsum("gcp,jgpd->jgcd", cr, ab_re, precision=hi)
            - jnp.einsum("gcp,jgpd->jgcd", ci, ab_im, precision=hi))

    nb, bg, t = N_BUNDLES, S5_BUNDLE, S5_T
    tt = jnp.arange(t)
    lag = tt[None, :] - tt[:, None]
    ksel = jnp.where((lag >= 0)[:, :, None, None, None], kern[jnp.clip(lag, 0, t - 1)], 0.0)
    ksel = ksel.reshape(t, t, nb, bg, S5_GROUP, S5_GROUP).transpose(2, 0, 3, 5, 1, 4)
    m = ksel.astype(BF16).reshape(nb, t * 128, 128)

    inj = jnp.stack([ab_re[::-1], ab_im[::-1]])
    inj = inj.reshape(2, t, nb, bg, S5_STATE, S5_GROUP).transpose(2, 1, 3, 5, 0, 4)
    we = inj.astype(BF16).reshape(nb, t * 128, 128)

    p1r, p1i = pr[1:, :, None, :], pi[1:, :, None, :]
    co_re = cr[None] * p1r - ci[None] * p1i
    co_im = -(cr[None] * p1i + ci[None] * p1r)
    co = jnp.stack([co_re, co_im]).reshape(2, t, nb, bg, S5_GROUP, S5_STATE).transpose(2, 0, 3, 5, 1, 4)
    ws = co.astype(BF16).reshape(nb, 2 * bg * S5_STATE, 128)

    a_t = jnp.stack([pr[t], pi[t]]).reshape(2, nb, bg * S5_STATE)
    a_t = a_t.transpose(1, 0, 2).reshape(nb, 1, 2 * bg * S5_STATE)
    dv = jnp.broadcast_to(d.astype(F32).reshape(1, nb, 128), (t, nb, 128)).transpose(1, 0, 2).reshape(nb, 1, t * 128)
    return m, we, ws, a_t, dv


def _group_spread(n_outer, inner):
    r = jnp.arange(n_outer * inner)[:, None]
    c = jnp.arange(n_outer * S5_BUNDLE * inner)[None, :]
    return ((r // inner == c // (S5_BUNDLE * inner)) & (r % inner == c % inner)).astype(BF16)


def _widen(compact, spread, row_shift, col_shift):
    wide = _dot(compact, spread)
    row_g = (lax.broadcasted_iota(I32, wide.shape, 0) >> row_shift) & (S5_BUNDLE - 1)
    col_h = (lax.broadcasted_iota(I32, wide.shape, 1) >> col_shift) & (S5_BUNDLE - 1)
    return jnp.where(row_g == col_h, wide, 0.0).astype(BF16)


def _s5_kernel(u_ref, s0_ref, mc_ref, wec_ref, wsc_ref, sp_tc, sp_rp, a_ref, d_ref, g_ref, sfin_ref,
               m_ref, we_ref, ws_ref, state, ubuf, sprev, ebuf, *, nb, cb, last):
    i = pl.program_id(1)
    half = S5_BUNDLE * S5_STATE

    @pl.when(i == 0)
    def _():
        state[...] = s0_ref[...]
        m_ref[...] = _widen(mc_ref[...], sp_tc[...], 4, 4)
        we_ref[...] = _widen(wec_ref[...], sp_rp[...], 4, 6)
        ws_ref[...] = _widen(wsc_ref[...], sp_tc[...], 6, 4)

    for c in range(cb):
        for t in range(S5_T):
            ubuf[c * nb:(c + 1) * nb, t * 128:(t + 1) * 128] = u_ref[:, c * S5_T + t, :]
    u = ubuf[...]
    ub = u.astype(BF16)
    ebuf[...] = _dot(ub, we_ref[...])
    ar, ai = a_ref[:, :half], a_ref[:, half:]
    s = state[...]
    for c in range(cb):
        rows = slice(c * nb, (c + 1) * nb)
        sprev[rows, :] = s
        sr, si = s[:, :half], s[:, half:]
        s = jnp.concatenate([ar * sr - ai * si + ebuf[rows, :half],
                             ar * si + ai * sr + ebuf[rows, half:]], axis=1)
    state[...] = s
    y = jnp.concatenate([_dot(ub[:, :256 * (j + 1)], m_ref[0:256 * (j + 1), 256 * j:256 * (j + 1)])
                         for j in range(S5_W // 256)], axis=1)
    y = y + _dot(sprev[...].astype(BF16), ws_ref[...])
    ebuf[...] = jax.nn.gelu(y + d_ref[...] * u)
    for c in range(cb):
        for t in range(S5_T):
            g_ref[:, c * S5_T + t, :] = ebuf[c * nb:(c + 1) * nb, t * 128:(t + 1) * 128]

    @pl.when(i == last)
    def _():
        sfin_ref[...] = s


def _s5_scan(h, s0, tables, cb):
    m, we, ws, a_t, dv = tables
    nb, seq, _ = h.shape
    tok = S5_T * cb
    ni = seq // tok
    r = nb * cb
    w = S5_W
    kern = functools.partial(_s5_kernel, nb=nb, cb=cb, last=ni - 1)
    compact = lambda: pl.BlockSpec((None, w, 128), lambda g, i: (g, 0, 0))
    spread = lambda: pl.BlockSpec((128, w), lambda g, i: (0, 0))
    vec = lambda: pl.BlockSpec((None, 1, w), lambda g, i: (g, 0, 0))
    return pl.pallas_call(
        kern, grid=(N_BUNDLES, ni),
        in_specs=[pl.BlockSpec((nb, tok, 128), lambda g, i: (0, i, g)),
                  pl.BlockSpec((None, nb, w), lambda g, i: (g, 0, 0)),
                  compact(), compact(), compact(), spread(), spread(), vec(), vec()],
        out_specs=[pl.BlockSpec((nb, tok, 128), lambda g, i: (0, i, g)),
                   pl.BlockSpec((None, nb, w), lambda g, i: (g, 0, 0))],
        out_shape=[jax.ShapeDtypeStruct((nb, seq, D), F32),
                   jax.ShapeDtypeStruct((N_BUNDLES, nb, w), F32)],
        scratch_shapes=[pltpu.VMEM((w, w), BF16), pltpu.VMEM((w, w), BF16), pltpu.VMEM((w, w), BF16),
                        pltpu.VMEM((nb, w), F32), pltpu.VMEM((r, w), F32), pltpu.VMEM((r, w), F32),
                        pltpu.VMEM((r, w), F32)],
        compiler_params=_cparams(("parallel", "arbitrary")), name="s5_scan",
    )(h, s0, m, we, ws, _group_spread(S5_T, S5_GROUP), _group_spread(2, S5_STATE), a_t, dv)


def _state_to_bundles(s_re, s_im):
    n_b = s_re.shape[0]
    s = jnp.stack([s_re, s_im], axis=1).reshape(n_b, 2, N_BUNDLES, S5_BUNDLE * S5_STATE)
    return s.transpose(2, 0, 1, 3).reshape(N_BUNDLES, n_b, 2 * S5_BUNDLE * S5_STATE)


def _state_from_bundles(s):
    n_b = s.shape[1]
    s = s.reshape(N_BUNDLES, n_b, 2, S5_BUNDLE, S5_STATE).transpose(2, 1, 0, 3, 4)
    s = s.reshape(2, n_b, N_GROUPS, S5_STATE)
    return s[0], s[1]


def _glu_kernel(*refs, n0, two_x):
    if two_x:
        gp_ref, gs_ref, xp_ref, xs_ref, w_ref, o_ref = refs
    else:
        gp_ref, gs_ref, xp_ref, w_ref, o_ref = refs
    first = pl.program_id(0) < n0
    g = jnp.where(first, gp_ref[...], gs_ref[...])
    x = jnp.where(first, xp_ref[...], xs_ref[...]) if two_x else xp_ref[...]
    z = _dot(g.astype(BF16), w_ref[...])
    o_ref[...] = x + z[:, :D] * jax.nn.sigmoid(z[:, D:])


def _glu_residual(g_p, g_s, x_parts, w_glu, tm=512):
    t = g_p.shape[0] + g_s.shape[0]
    n0 = g_p.shape[0] // tm
    two_x = len(x_parts) == 2
    tile = lambda m: pl.BlockSpec((tm, D), m)
    x_specs = [tile(_lo_map(n0)), tile(_hi_map(n0))] if two_x else [tile(lambda i: (i, 0))]
    return pl.pallas_call(
        functools.partial(_glu_kernel, n0=n0, two_x=two_x), grid=(t // tm,),
        in_specs=[tile(_lo_map(n0)), tile(_hi_map(n0))] + x_specs + [pl.BlockSpec((D, 2 * D), lambda i: (0, 0))],
        out_specs=tile(lambda i: (i, 0)),
        out_shape=jax.ShapeDtypeStruct((t, D), F32),
        compiler_params=_cparams(("parallel",)), name="glu_residual",
    )(g_p, g_s, *x_parts, w_glu)


def _s5_layer(x_parts, norm_g, state_re, state_im, params, dims, h_parts=None):
    n_bp, seq_p, n_bs, seq_s = dims
    tp, ts = n_bp * seq_p, n_bs * seq_s
    lam_re, lam_im, log_dt, b_re, b_im, c_re, c_im, d, w_glu = params
    tables = _s5_weights(lam_re, lam_im, log_dt, b_re, b_im, c_re, c_im, d)
    if h_parts is not None:
        h_p, h_s = h_parts
    elif len(x_parts) == 2:
        h_p = _rmsnorm(x_parts[0], norm_g, 0, tp)
        h_s = _rmsnorm(x_parts[1], norm_g, 0, ts)
    else:
        h_p = _rmsnorm(x_parts[0], norm_g, 0, tp)
        h_s = _rmsnorm(x_parts[0], norm_g, tp, ts)
    zeros = jnp.zeros((N_BUNDLES, n_bp, 2 * S5_BUNDLE * S5_STATE), F32)
    cb = math.gcd(seq_p // S5_T, max(1, 512 // n_bp))
    g_p, s_p = _s5_scan(h_p.reshape(n_bp, seq_p, D), zeros, tables, cb)
    g_s, s_s = _s5_scan(h_s.reshape(n_bs, seq_s, D), _state_to_bundles(state_re, state_im), tables, seq_s // S5_T)
    x = _glu_residual(g_p.reshape(tp, D), g_s.reshape(ts, D), x_parts, w_glu.astype(BF16))
    p_re, p_im = _state_from_bundles(s_p)
    s_re, s_im = _state_from_bundles(s_s)
    return x, (p_re, s_re, p_im, s_im)


def _ffn_kernel(x_ref, gn_ref, win_ref, wo_ref, *rest, nf, n0):
    d_ff = wo_ref.shape[0]
    tf = d_ff // nf
    x = x_ref[...]
    h = _rms(x, gn_ref[...]).astype(BF16)
    y = x
    for f in range(nf):
        cols = slice(f * tf, (f + 1) * tf)
        up = slice(d_ff + f * tf, d_ff + (f + 1) * tf)
        t = (jax.nn.silu(_dot(h, win_ref[:, cols])) * _dot(h, win_ref[:, up])).astype(BF16)
        y = y + _dot(t, wo_ref[cols, :])
    if n0 is None:
        rest[0][...] = y
        return
    gnext_ref, o_ref, hp_ref, hs_ref = rest
    o_ref[...] = y
    hn = _rms(y, gnext_ref[...])
    first = pl.program_id(0) < n0

    @pl.when(first)
    def _():
        hp_ref[...] = hn

    @pl.when(jnp.logical_not(first))
    def _():
        hs_ref[...] = hn


def _ffn_dense(x, norm_g, w_in, w_out, next_norm=None, split_rows=None, tm=512, nf=2):
    t = x.shape[0]
    tile = lambda m: pl.BlockSpec((tm, D), m)
    vec = pl.BlockSpec((1, D), lambda i: (0, 0))
    resident = lambda a: pl.BlockSpec(a.shape, lambda i: (0, 0), pipeline_mode=pl.Buffered(1))
    w_in, w_out = w_in.astype(BF16), w_out.astype(BF16)
    args = [x, norm_g.reshape(1, D), w_in, w_out]
    in_specs = [tile(lambda i: (i, 0)), vec, resident(w_in), resident(w_out)]
    out_specs, out_shape, n0 = tile(lambda i: (i, 0)), jax.ShapeDtypeStruct((t, D), F32), None
    if next_norm is not None:
        n0 = split_rows // tm
        args.append(next_norm.reshape(1, D))
        in_specs.append(vec)
        out_specs = [out_specs, tile(_lo_map(n0)), tile(_hi_map(n0))]
        out_shape = [out_shape, jax.ShapeDtypeStruct((split_rows, D), F32),
                     jax.ShapeDtypeStruct((t - split_rows, D), F32)]
    return pl.pallas_call(
        functools.partial(_ffn_kernel, nf=nf, n0=n0), grid=(t // tm,),
        in_specs=in_specs, out_specs=out_specs, out_shape=out_shape,
        compiler_params=_cparams(("arbitrary",)), name="ffn_dense",
    )(*args)


def _swap_halves(w):
    half = w.shape[-1] // 2
    return jnp.concatenate([w[..., half:], w[..., :half]], axis=-1)


def _pad_lanes(w, n=128):
    return jnp.concatenate([w, jnp.zeros(w.shape[:-1] + (n - w.shape[-1],), w.dtype)], axis=-1)


def _rope_tables(seq_p, seq_s, past_len, rows_s):
    half = QK_ROPE // 2
    inv_freq = ROPE_THETA ** (-jnp.arange(half, dtype=F32) / half)
    pos = jnp.concatenate([jnp.arange(seq_p), past_len + (jnp.arange(rows_s) % seq_s)])
    ang = pos.astype(F32)[:, None] * inv_freq[None, :]
    cos, sin = jnp.cos(ang), jnp.sin(ang)
    return _pad_lanes(jnp.concatenate([cos, cos], axis=1)), _pad_lanes(jnp.concatenate([-sin, sin], axis=1))


def _rope_norm(x, x_sw, g, g_sw, cos, sin):
    r = lax.rsqrt(jnp.sum(x * x, axis=-1, keepdims=True) * (1.0 / QK_ROPE) + NORM_EPS)
    return r * (x * g * cos + x_sw * g_sw * sin)


def _mla_proj_kernel(x_ref, gmix, win, nq, nkv, cos, sin, gkr, gkrs, wq, gqn, gqr, gqrs, wuk, gkn,
                     cp_out, cs_out, krp_out, krs_out, cbt_out, q_out, k_out, *, n0):
    is_prompt = pl.program_id(0) < n0
    hn = _rms(x_ref[...], gmix[...]).astype(BF16)
    a = _dot(hn, win[...])
    hq = _rms(a[:, :Q_RANK], nq[...]).astype(BF16)
    c = _rms(a[:, Q_RANK:Q_RANK + KV_RANK], nkv[...])
    cb = c.astype(BF16)
    kr = _rope_norm(a[:, 768:896], a[:, 896:1024], gkr[...], gkrs[...], cos[...], sin[...])
    krb = kr.astype(BF16)

    @pl.when(is_prompt)
    def _():
        cp_out[...] = c
        krp_out[...] = kr
        cbt_out[...] = c.T.astype(BF16)

    @pl.when(jnp.logical_not(is_prompt))
    def _():
        cs_out[...] = c
        krs_out[...] = kr

    q_scale = jnp.where(is_prompt, ATTN_SCALE * LOG2E, 1.0)
    for h in range(N_HEADS):
        q3 = _dot(hq, wq[:, h * 384:(h + 1) * 384])
        qn = _rms(q3[:, :128], gqn[...])
        qr = _rope_norm(q3[:, 128:256], q3[:, 256:384], gqr[...], gqrs[...], cos[...], sin[...])
        q_out[h] = (jnp.concatenate([qn, qr], axis=1) * q_scale).astype(BF16)

    @pl.when(is_prompt)
    def _():
        for h in range(N_HEADS):
            kn = _rms(_dot(cb, wuk[:, h * QK_NOPE:(h + 1) * QK_NOPE]), gkn[...])
            k_out[h] = jnp.concatenate([kn.astype(BF16), krb], axis=1)


def _mla_proj(x, norm_g, params, dims, past_len, tm=512):
    n_bp, seq_p, n_bs, seq_s = dims
    t = x.shape[0]
    tp = n_bp * seq_p
    ts = t - tp
    w_in, norm_q, norm_kv, w_q, w_uk, w_uv, qn_nope, qn_rope, kn_nope, kn_rope, w_o = params
    w_kr = w_in[:, Q_RANK + KV_RANK:]
    win = jnp.concatenate([w_in[:, :Q_RANK + KV_RANK], _pad_lanes(w_kr), _pad_lanes(_swap_halves(w_kr))],
                          axis=1).astype(BF16)
    wq3 = w_q.reshape(Q_RANK, N_HEADS, QK_NOPE + QK_ROPE)
    wq_r = wq3[:, :, QK_NOPE:]
    wq = jnp.concatenate([wq3[:, :, :QK_NOPE], _pad_lanes(wq_r), _pad_lanes(_swap_halves(wq_r))], axis=2)
    wq = wq.reshape(Q_RANK, N_HEADS * 384).astype(BF16)
    wuk = w_uk.reshape(KV_RANK, N_HEADS * QK_NOPE).astype(BF16)
    cos, sin = _rope_tables(seq_p, seq_s, past_len, min(tm, ts))
    n_pt = seq_p // tm
    n0 = tp // tm
    tab = lambda i: (jnp.where(i < n0, i % n_pt, n_pt), 0)
    row = lambda v: v.reshape(1, -1).astype(F32)
    vec = lambda n: pl.BlockSpec((1, n), lambda i: (0, 0))
    full = lambda a: pl.BlockSpec(a.shape, lambda i: (0, 0))
    lo = lambda n: pl.BlockSpec((tm, n), _lo_map(n0))
    hi = lambda n: pl.BlockSpec((tm, n), _hi_map(n0))
    sds = jax.ShapeDtypeStruct
    return pl.pallas_call(
        functools.partial(_mla_proj_kernel, n0=n0), grid=(t // tm,),
        in_specs=[pl.BlockSpec((tm, D), lambda i: (i, 0)), vec(D), full(win), vec(Q_RANK), vec(KV_RANK),
                  pl.BlockSpec((tm, 128), tab), pl.BlockSpec((tm, 128), tab), vec(128), vec(128),
                  full(wq), vec(128), vec(128), vec(128), full(wuk), vec(128)],
        out_specs=[lo(KV_RANK), hi(KV_RANK), lo(128), hi(128),
                   pl.BlockSpec((KV_RANK, tm), lambda i: (0, jnp.minimum(i, n0 - 1))),
                   pl.BlockSpec((N_HEADS, tm, 256), lambda i: (0, i, 0)),
                   pl.BlockSpec((N_HEADS, tm, 256), lambda i: (0, jnp.minimum(i, n0 - 1), 0))],
        out_shape=[sds((tp, KV_RANK), F32), sds((ts, KV_RANK), F32), sds((tp, 128), F32), sds((ts, 128), F32),
                   sds((KV_RANK, tp), BF16), sds((N_HEADS, t, 256), BF16), sds((N_HEADS, tp, 256), BF16)],
        compiler_params=_cparams(("arbitrary",)), name="mla_proj",
    )(x, row(norm_g), win, row(norm_q), row(norm_kv), cos, sin,
      row(_pad_lanes(kn_rope)), row(_pad_lanes(_swap_halves(kn_rope))), wq, row(qn_nope),
      row(_pad_lanes(qn_rope)), row(_pad_lanes(_swap_halves(qn_rope))), wuk, row(kn_nope))


def _flash_kernel(qi_tab, ki_tab, q_ref, k_ref, vt_ref, o_ref, m, l, acc):
    step = pl.program_id(1)
    qi, ki = qi_tab[step], ki_tab[step]

    @pl.when(ki == 0)
    def _():
        m[...] = jnp.full_like(m, NEG_INF)
        l[...] = jnp.zeros_like(l)
        acc[...] = jnp.zeros_like(acc)

    def update(h, masked):
        s = _dot_nt(k_ref[h], q_ref[h])
        if masked:
            key = lax.broadcasted_iota(I32, s.shape, 0)
            qry = lax.broadcasted_iota(I32, s.shape, 1)
            s = jnp.where(key <= qry, s, NEG_INF)
        m_new = jnp.maximum(m[h], jnp.max(s, axis=0, keepdims=True))
        corr = jnp.exp2(m[h] - m_new)
        p = jnp.exp2(s - m_new)
        l[h] = l[h] * corr + jnp.sum(p, axis=0, keepdims=True)
        acc[h] = acc[h] * corr + _dot(vt_ref[...], p.astype(BF16))
        m[h] = m_new

    def heads(masked):
        def pair(hp, carry):
            update(2 * hp, masked)
            update(2 * hp + 1, masked)
            return carry
        lax.fori_loop(0, N_HEADS // 2, pair, 0)

    @pl.when(ki < qi)
    def _():
        heads(False)

    @pl.when(ki == qi)
    def _():
        heads(True)
        for h in range(N_HEADS):
            o_ref[h] = (acc[h] / l[h]).T.astype(o_ref.dtype)


def _mla_prompt_attn(qcat, kcat, cbt, n_b, seq, tq=512):
    nq = seq // tq
    pairs = [(qi, ki) for qi in range(nq) for ki in range(qi + 1)]
    qi_tab = jnp.array([p[0] for p in pairs], I32)
    ki_tab = jnp.array([p[1] for p in pairs], I32)
    grid_spec = pltpu.PrefetchScalarGridSpec(
        num_scalar_prefetch=2, grid=(n_b, len(pairs)),
        in_specs=[pl.BlockSpec((N_HEADS, tq, 256), lambda b, s, qt, kt: (0, b * nq + qt[s], 0)),
                  pl.BlockSpec((N_HEADS, tq, 256), lambda b, s, qt, kt: (0, b * nq + kt[s], 0)),
                  pl.BlockSpec((KV_RANK, tq), lambda b, s, qt, kt: (0, b * nq + kt[s]))],
        out_specs=pl.BlockSpec((N_HEADS, tq, KV_RANK), lambda b, s, qt, kt: (0, b * nq + qt[s], 0)),
        scratch_shapes=[pltpu.VMEM((N_HEADS, 1, tq), F32), pltpu.VMEM((N_HEADS, 1, tq), F32),
                        pltpu.VMEM((N_HEADS, KV_RANK, tq), F32)])
    return pl.pallas_call(
        _flash_kernel, grid_spec=grid_spec,
        out_shape=jax.ShapeDtypeStruct((N_HEADS, n_b * seq, KV_RANK), BF16),
        compiler_params=_cparams(("parallel", "arbitrary")), name="mla_prompt_attn",
    )(qi_tab, ki_tab, qcat, kcat, cbt)


def _sample_attn_kernel(pt_ref, lat_hbm, krt_hbm, cnew, krtnew, q_ref, wuk_ref, wukt_ref, gkn_ref, o_ref,
                        lat_buf, krt_buf, sem, lhs, m, l, acc, *, n_b, npg, n_chunks, dq):
    b, kc = pl.program_id(0), pl.program_id(1)
    nhq = N_HEADS * dq
    nkd = N_HEADS * QK_NOPE
    slot = (b * n_chunks + kc) % 2

    def page_copies(bb, cc, s):
        copies = []
        for j in range(npg):
            pid = pt_ref[bb, cc * npg + j]
            copies.append(pltpu.make_async_copy(lat_hbm.at[pid], lat_buf.at[s, j], sem.at[0, s]))
            copies.append(pltpu.make_async_copy(krt_hbm.at[pid], krt_buf.at[s, j], sem.at[1, s]))
        return copies

    @pl.when(kc == 0)
    def _():
        m[...] = jnp.full_like(m, NEG_INF)
        l[...] = jnp.zeros_like(l)
        acc[...] = jnp.zeros_like(acc)
        lhs[0:nkd, :] = wukt_ref[...]
        qn = (q_ref[:, :QK_NOPE] * gkn_ref[...]).astype(BF16)
        qabs = [_dot_nt(qn[h * dq:(h + 1) * dq], wuk_ref[h]) for h in range(N_HEADS)]
        lhs[nkd:nkd + nhq, :] = jnp.concatenate(qabs, axis=0).astype(BF16)

    def attend(c_pages, krt_pages, mask):
        ct = jnp.concatenate([c.T for c in c_pages], axis=1).astype(BF16)
        cb = jnp.concatenate(c_pages, axis=0).astype(BF16)
        kt = jnp.concatenate(krt_pages, axis=1).astype(BF16)
        nk = ct.shape[1]
        r = _dot(lhs[...], ct)
        rinv = []
        for h in range(N_HEADS):
            kn = r[h * QK_NOPE:(h + 1) * QK_NOPE]
            ssq = jnp.sum(kn * kn, axis=0, keepdims=True)
            rinv.append(jnp.broadcast_to(lax.rsqrt(ssq * (1.0 / QK_NOPE) + NORM_EPS), (dq, nk)))
        qr = q_ref[:, QK_NOPE:QK_NOPE + QK_ROPE].astype(BF16)
        s = (r[nkd:] * jnp.concatenate(rinv, axis=0) + _dot(qr, kt)) * ATTN_SCALE
        if mask is not None:
            s = jnp.where(mask, s, NEG_INF)
        m_new = jnp.maximum(m[...], jnp.max(s, axis=-1, keepdims=True))
        corr = jnp.exp(m[...] - m_new)
        p = jnp.exp(s - m_new)
        l[...] = l[...] * corr + jnp.sum(p, axis=-1, keepdims=True)
        acc[...] = acc[...] * corr + _dot(p.astype(BF16), cb)
        m[...] = m_new

    @pl.when(kc < n_chunks)
    def _():
        @pl.when((b == 0) & (kc == 0))
        def _():
            for c in page_copies(b, kc, slot):
                c.start()

        last_chunk = kc + 1 == n_chunks
        nxt_b = jnp.where(last_chunk, b + 1, b)
        nxt_c = jnp.where(last_chunk, 0, kc + 1)

        @pl.when(nxt_b < n_b)
        def _():
            for c in page_copies(nxt_b, nxt_c, 1 - slot):
                c.start()

        for c in page_copies(b, kc, slot):
            c.wait()
        group = min(npg, ATTEND_PAGES)
        for j0 in range(0, npg, group):
            pages = range(j0, j0 + group)
            attend([lat_buf[slot, j] for j in pages], [krt_buf[slot, j] for j in pages], None)

    @pl.when(kc == n_chunks)
    def _():
        key = lax.broadcasted_iota(I32, (nhq, PAGE), 1)
        qpos = lax.broadcasted_iota(I32, (nhq, PAGE), 0) % dq
        attend([cnew[...]], [krtnew[...]], key <= qpos)
        o_ref[...] = (acc[...] / l[...]).astype(o_ref.dtype)


def _mla_sample_attn(q_s, c_new, kr_new, cache_latent, cache_krope, page_table, wuk, gkn, n_b, dq):
    n_pages = page_table.shape[1]
    npg = math.gcd(n_pages, PAGES_PER_STEP)
    n_chunks = n_pages // npg
    nhq = N_HEADS * dq
    nkd = N_HEADS * QK_NOPE
    wukt = wuk.transpose(0, 2, 1).reshape(nkd, KV_RANK)
    pad = lambda a: jnp.concatenate([a, jnp.zeros((n_b, PAGE - dq, a.shape[-1]), a.dtype)], axis=1)
    krt_cache = cache_krope.swapaxes(1, 2)
    krt_new = pad(kr_new).swapaxes(1, 2)
    per_b = lambda r, w: pl.BlockSpec((None, r, w), lambda b, kc, pt: (b, 0, 0))
    grid_spec = pltpu.PrefetchScalarGridSpec(
        num_scalar_prefetch=1, grid=(n_b, n_chunks + 1),
        in_specs=[pl.BlockSpec(memory_space=pl.ANY), pl.BlockSpec(memory_space=pl.ANY),
                  per_b(PAGE, KV_RANK), per_b(QK_ROPE, PAGE), per_b(nhq, 256),
                  pl.BlockSpec((N_HEADS, KV_RANK, QK_NOPE), lambda b, kc, pt: (0, 0, 0)),
                  pl.BlockSpec((nkd, KV_RANK), lambda b, kc, pt: (0, 0)),
                  pl.BlockSpec((1, QK_NOPE), lambda b, kc, pt: (0, 0))],
        out_specs=per_b(nhq, KV_RANK),
        scratch_shapes=[pltpu.VMEM((2, npg, PAGE, KV_RANK), F32), pltpu.VMEM((2, npg, QK_ROPE, PAGE), F32),
                        pltpu.SemaphoreType.DMA((2, 2)), pltpu.VMEM((nkd + nhq, KV_RANK), BF16),
                        pltpu.VMEM((nhq, 1), F32), pltpu.VMEM((nhq, 1), F32), pltpu.VMEM((nhq, KV_RANK), F32)])
    return pl.pallas_call(
        functools.partial(_sample_attn_kernel, n_b=n_b, npg=npg, n_chunks=n_chunks, dq=dq), grid_spec=grid_spec,
        out_shape=jax.ShapeDtypeStruct((n_b, nhq, KV_RANK), BF16),
        compiler_params=_cparams(("arbitrary", "arbitrary")), name="mla_sample_attn",
    )(page_table, cache_latent, krt_cache, pad(c_new), krt_new, q_s,
      wuk, wukt, gkn.reshape(1, QK_NOPE).astype(F32))


def _mla_out_kernel(op_ref, os_ref, x_ref, wuv_ref, wo_ref, out_ref, *, n0):
    first = pl.program_id(0) < n0
    v = jnp.concatenate([_dot(jnp.where(first, op_ref[h], os_ref[h]), wuv_ref[h]) for h in range(N_HEADS)], axis=1)
    out_ref[...] = x_ref[...] + _dot(v.astype(BF16), wo_ref[...])


def _mla_out(o_p, o_s, x, w_uv, w_o, tm=512):
    t = x.shape[0]
    n0 = o_p.shape[1] // tm
    return pl.pallas_call(
        functools.partial(_mla_out_kernel, n0=n0), grid=(t // tm,),
        in_specs=[pl.BlockSpec((N_HEADS, tm, KV_RANK), lambda i: (0, jnp.minimum(i, n0 - 1), 0)),
                  pl.BlockSpec((N_HEADS, tm, KV_RANK), lambda i: (0, jnp.maximum(i - n0, 0), 0)),
                  pl.BlockSpec((tm, D), lambda i: (i, 0)),
                  pl.BlockSpec((N_HEADS, KV_RANK, 128), lambda i: (0, 0, 0)), pl.BlockSpec((D, D), lambda i: (0, 0))],
        out_specs=pl.BlockSpec((tm, D), lambda i: (i, 0)),
        out_shape=jax.ShapeDtypeStruct((t, D), F32),
        compiler_params=_cparams(("parallel",)), name="mla_out",
    )(o_p, o_s, x, w_uv.transpose(1, 0, 2).astype(BF16), w_o.astype(BF16))


def _mla_layer(x, norm_g, cache_latent, cache_krope, page_table, params, dims):
    n_bp, seq_p, n_bs, seq_s = dims
    tp = n_bp * seq_p
    past_len = page_table.shape[1] * cache_latent.shape[1]
    w_uk, w_uv, kn_nope, w_o = params[4], params[5], params[8], params[10]
    c_p, c_s, kr_p, kr_s, cbt, qcat, kcat = _mla_proj(x, norm_g, params, dims, past_len)
    c_s = c_s.reshape(n_bs, seq_s, KV_RANK)
    kr_s = kr_s[:, :QK_ROPE].reshape(n_bs, seq_s, QK_ROPE)
    o_p = _mla_prompt_attn(qcat, kcat, cbt, n_bp, seq_p)
    q_s = qcat[:, tp:].astype(F32).reshape(N_HEADS, n_bs, seq_s, 256).transpose(1, 0, 2, 3)
    q_s = q_s.reshape(n_bs, N_HEADS * seq_s, 256)
    o_s = _mla_sample_attn(q_s, c_s, kr_s, cache_latent, cache_krope, page_table,
                           w_uk.transpose(1, 0, 2).astype(BF16), kn_nope, n_bs, seq_s)
    o_s = o_s.reshape(n_bs, N_HEADS, seq_s, KV_RANK).transpose(1, 0, 2, 3).reshape(N_HEADS, n_bs * seq_s, KV_RANK)
    x = _mla_out(o_p, o_s, x, w_uv, w_o)
    state = (c_p.reshape(n_bp, seq_p, KV_RANK), c_s, kr_p[:, :QK_ROPE].reshape(n_bp, seq_p, QK_ROPE), kr_s)
    return x, state


def _pool_kernel(x_ref, prev_ref, gn_ref, w_ref, sc_ref, o_ref, st_ref, ext, *, tl, ns, nt, start):
    ti = pl.program_id(1)
    for s in range(ns):
        rows = slice(s * tl, (s + 1) * tl)
        if nt == 1:
            ext[s, 0:POOL_HALO, :] = prev_ref[s]
        else:
            @pl.when(ti == 0)
            def _():
                ext[s, 0:POOL_HALO, :] = prev_ref[s]

            @pl.when(ti > 0)
            def _():
                ext[s, 0:POOL_HALO, :] = ext[s, tl:tl + POOL_HALO, :]

        x = x_ref[rows, :]
        h = _rms(x, gn_ref[...])
        ext[s, POOL_HALO:POOL_HALO + tl, :] = h
        pos = start + ti * tl + lax.broadcasted_iota(I32, (tl, 1), 0)
        run = ext[s]
        z = []
        for g, w in enumerate(POOL_WINDOWS):
            run = run[:, (POOL_GROUP if g else 0):]
            run = run + pltpu.roll(run, w // 2, 0)
            total = run[POOL_HALO:POOL_HALO + tl, :POOL_GROUP]
            count = jnp.minimum(w, pos + 1).astype(F32)
            p = total / count - h[:, g * POOL_GROUP:(g + 1) * POOL_GROUP]
            z.append(_dot(p.astype(BF16), w_ref[g]))
        o_ref[rows, :] = x + jnp.concatenate(z, axis=1) * sc_ref[...]

        if nt == 1:
            st_ref[s] = ext[s, tl:tl + POOL_HALO, :]
        else:
            @pl.when(ti == nt - 1)
            def _():
                st_ref[s] = ext[s, tl:tl + POOL_HALO, :]


def _pool_stream(x, prev, norm_g, pool_w, pool_scale, n_b, seq, row0, start, tl, ns):
    nt = seq // tl
    blk0 = row0 // (ns * tl)
    prev = jnp.concatenate([jnp.zeros((n_b, 1, D), F32), prev.astype(F32)], axis=1)
    rows = lambda b, t: (blk0 + b * nt + t, 0)
    out, st = pl.pallas_call(
        functools.partial(_pool_kernel, tl=tl, ns=ns, nt=nt, start=start), grid=(n_b // ns, nt),
        in_specs=[pl.BlockSpec((ns * tl, D), rows),
                  pl.BlockSpec((ns, POOL_HALO, D), lambda b, t: (b, 0, 0)),
                  pl.BlockSpec((1, D), lambda b, t: (0, 0)),
                  pl.BlockSpec((4, POOL_GROUP, POOL_GROUP), lambda b, t: (0, 0, 0)),
                  pl.BlockSpec((1, D), lambda b, t: (0, 0))],
        out_specs=[pl.BlockSpec((ns * tl, D), rows),
                   pl.BlockSpec((ns, POOL_HALO, D), lambda b, t: (b, 0, 0))],
        out_shape=[jax.ShapeDtypeStruct(x.shape, F32), jax.ShapeDtypeStruct((n_b, POOL_HALO, D), F32)],
        scratch_shapes=[pltpu.VMEM((ns, POOL_HALO + tl, D), F32)],
        input_output_aliases={0: 0},
        compiler_params=_cparams(("parallel", "arbitrary")), name="pool_mixer",
    )(x, prev, norm_g.reshape(1, D), pool_w.astype(BF16), pool_scale.reshape(1, D))
    return out, st[:, 1:]


def _pool_layer(x, norm_g, state_pool, params, dims, past_len):
    n_bp, seq_p, n_bs, seq_s = dims
    tp = n_bp * seq_p
    pool_w, pool_scale = params
    zero = jnp.zeros((n_bp, POOL_HALO - 1, D), F32)
    x, st_p = _pool_stream(x, zero, norm_g, pool_w, pool_scale, n_bp, seq_p, 0, 0, min(512, seq_p), 1)
    x, st_s = _pool_stream(x, state_pool, norm_g, pool_w, pool_scale, n_bs, seq_s, tp, past_len, seq_s,
                           math.gcd(n_bs, 8))
    return x, (st_p, st_s)


def _router_kernel(x_ref, gn_ref, wr_ref, tri_ref, hb_ref, sel_ref, gate_ref, cnt_ref):
    h = _rms(x_ref[...], gn_ref[...])
    hb_ref[...] = h.astype(BF16)
    logits = lax.dot_general(wr_ref[...], h, (((1,), (1,)), ((), ())), precision=lax.Precision.HIGHEST,
                             preferred_element_type=F32)
    idx = lax.broadcasted_iota(I32, logits.shape, 0)
    m1 = jnp.max(logits, axis=0, keepdims=True)
    i1 = jnp.min(jnp.where(logits == m1, idx, N_EXPERTS), axis=0, keepdims=True)
    first = idx == i1
    rest = jnp.where(first, -jnp.inf, logits)
    m2 = jnp.max(rest, axis=0, keepdims=True)
    i2 = jnp.min(jnp.where(rest == m2, idx, N_EXPERTS), axis=0, keepdims=True)
    second = idx == i2
    e2 = jnp.exp(m2 - m1)
    g1 = 1.0 / (1.0 + e2)
    g2 = e2 / (1.0 + e2)
    chosen = jnp.where(first | second, 1.0, 0.0)
    rank = _dot(chosen.astype(BF16), tri_ref[...])
    rank1 = jnp.sum(jnp.where(first, rank, 0.0), axis=0, keepdims=True)
    rank2 = jnp.sum(jnp.where(second, rank, 0.0), axis=0, keepdims=True)
    zero = jnp.zeros((N_EXPERTS - 4,) + i1.shape[1:], I32)
    sel_ref[...] = jnp.concatenate([i1, i2, rank1.astype(I32), rank2.astype(I32), zero], axis=0)
    gate_ref[...] = jnp.concatenate([g1, g2, jnp.zeros((N_EXPERTS - 2,) + g1.shape[1:], F32)], axis=0)
    cnt_ref[...] = jnp.broadcast_to(jnp.sum(chosen, axis=1, keepdims=True), cnt_ref.shape).astype(I32)


def _moe_route(x, norm_g, w_router):
    t = x.shape[0]
    tb = MOE_TB
    nblk = t // tb
    tri = (jnp.arange(tb)[:, None] < jnp.arange(tb)[None, :]).astype(BF16)
    blk = lambda dt: (pl.BlockSpec((None, N_EXPERTS, tb), lambda i: (i, 0, 0)),
                      jax.ShapeDtypeStruct((nblk, N_EXPERTS, tb), dt))
    (sel_spec, sel_shape), (gate_spec, gate_shape) = blk(I32), blk(F32)
    return pl.pallas_call(
        _router_kernel, grid=(nblk,),
        in_specs=[pl.BlockSpec((tb, D), lambda i: (i, 0)), pl.BlockSpec((1, D), lambda i: (0, 0)),
                  pl.BlockSpec((N_EXPERTS, D), lambda i: (0, 0)), pl.BlockSpec((tb, tb), lambda i: (0, 0))],
        out_specs=[pl.BlockSpec((tb, D), lambda i: (i, 0)), sel_spec, gate_spec,
                   pl.BlockSpec((None, N_EXPERTS, 128), lambda i: (i, 0, 0))],
        out_shape=[jax.ShapeDtypeStruct((t, D), BF16), sel_shape, gate_shape,
                   jax.ShapeDtypeStruct((nblk, N_EXPERTS, 128), I32)],
        compiler_params=_cparams(("parallel",)), name="moe_router",
    )(x, norm_g.reshape(1, D), w_router.T.astype(F32), tri)


def _moe_rows(t):
    nblk = t // MOE_TB
    worst = 2 * t + nblk * N_EXPERTS * (MOE_ALIGN - 1) + N_EXPERTS * (MOE_TM - 1)
    return (worst + MOE_TM - 1) // MOE_TM * MOE_TM


def _moe_plan(sel, counts, t):
    cnt = counts[:, :, 0]
    cpad = (cnt + MOE_ALIGN - 1) // MOE_ALIGN * MOE_ALIGN
    per_e = jnp.sum(cpad, axis=0)
    per_e_pad = (per_e + MOE_TM - 1) // MOE_TM * MOE_TM
    e_start = jnp.cumsum(per_e_pad) - per_e_pad
    off = e_start[None, :] + jnp.cumsum(cpad, axis=0) - cpad
    lend = jnp.cumsum(cpad, axis=1)
    lstart = lend - cpad
    e1, e2, r1, r2 = sel[:, 0], sel[:, 1], sel[:, 2], sel[:, 3]
    experts = jnp.arange(N_EXPERTS)
    pick = lambda tab, e: jnp.sum(jnp.where(e[:, :, None] == experts, tab[:, None, :], 0), axis=2)
    dstl = jnp.stack([pick(lstart, e1) + r1, pick(lstart, e2) + r2], axis=1).astype(I32)
    prow = jnp.arange(MOE_PIECES)[None, :] * MOE_ALIGN
    pe = jnp.minimum(jnp.sum(prow[:, :, None] >= lend[:, None, :], axis=2), N_EXPERTS - 1)
    grow = pick(off, pe) + prow - pick(lstart, pe)
    n_pieces = lend[:, -1] // MOE_ALIGN
    grow = jnp.where(jnp.arange(MOE_PIECES)[None, :] < n_pieces[:, None], grow, 0)
    n_tiles = _moe_rows(t) // MOE_TM
    e_end = e_start + per_e_pad
    tile_e = jnp.sum((jnp.arange(n_tiles) * MOE_TM)[:, None] >= e_end[None, :], axis=1)
    active = (tile_e < N_EXPERTS).astype(I32)
    tile_e = jnp.where(active == 1, tile_e, N_EXPERTS - 1)
    return dict(dstl=dstl, grow=grow.astype(I32).reshape(-1), n_pieces=n_pieces.astype(I32),
                tile_e=tile_e.astype(I32), active=active)


def _dispatch_kernel(np_ref, grow_ref, hb_ref, dst_ref, gate_ref, xs_in, xs_ref, buf, sem):
    del xs_in
    b = pl.program_id(0)
    nblk = pl.num_programs(0)
    slot = b % 2

    def copy(blk, i, s):
        r = pl.multiple_of(grow_ref[blk * MOE_PIECES + i], MOE_ALIGN)
        src = buf.at[s, pl.ds(pl.multiple_of(i * MOE_ALIGN, MOE_ALIGN), MOE_ALIGN), :]
        return pltpu.make_async_copy(src, xs_ref.at[pl.ds(r, MOE_ALIGN), :], sem.at[s])

    def drain(blk, s):
        lax.fori_loop(0, np_ref[blk], lambda i, c: (copy(blk, i, s).wait(), c)[1], 0)

    @pl.when(b >= 2)
    def _():
        drain(b - 2, slot)

    want = lax.broadcasted_iota(I32, (MOE_KL, 1), 0)
    hit1 = dst_ref[0:1, :] == want
    hit2 = dst_ref[1:2, :] == want
    onehot = jnp.where(hit1 | hit2, 1.0, 0.0).astype(BF16)
    buf[slot, :, 0:D] = _dot(onehot, hb_ref[...]).astype(BF16)
    gate = jnp.sum(jnp.where(hit1, gate_ref[0:1, :], 0.0) + jnp.where(hit2, gate_ref[1:2, :], 0.0),
                   axis=1, keepdims=True)
    g_hi = gate.astype(BF16).astype(F32)
    g_mid = (gate - g_hi).astype(BF16).astype(F32)
    g_lo = gate - g_hi - g_mid
    lane = lax.broadcasted_iota(I32, (MOE_KL, 128), 1)
    slab = jnp.where(lane == 0, g_hi, jnp.where(lane == 1, g_mid, jnp.where(lane == 2, g_lo, 0.0)))
    buf[slot, :, D:MOE_XW] = slab.astype(BF16)
    lax.fori_loop(0, np_ref[b], lambda i, c: (copy(b, i, slot).start(), c)[1], 0)

    @pl.when(b == nblk - 1)
    def _():
        @pl.when(b >= 1)
        def _():
            drain(b - 1, 1 - slot)
        drain(b, slot)


def _moe_dispatch(hb, gate, plan):
    t = hb.shape[0]
    nblk = t // MOE_TB
    rows = _moe_rows(t)
    grid_spec = pltpu.PrefetchScalarGridSpec(
        num_scalar_prefetch=2, grid=(nblk,),
        in_specs=[pl.BlockSpec((MOE_TB, D), lambda b, *_: (b, 0)),
                  pl.BlockSpec((None, 2, MOE_TB), lambda b, *_: (b, 0, 0)),
                  pl.BlockSpec((None, N_EXPERTS, MOE_TB), lambda b, *_: (b, 0, 0)),
                  pl.BlockSpec(memory_space=pl.ANY)],
        out_specs=pl.BlockSpec(memory_space=pl.ANY),
        scratch_shapes=[pltpu.VMEM((2, MOE_KL, MOE_XW), BF16), pltpu.SemaphoreType.DMA((2,))])
    return pl.pallas_call(
        _dispatch_kernel, grid_spec=grid_spec,
        out_shape=jax.ShapeDtypeStruct((rows, MOE_XW), BF16),
        input_output_aliases={5: 0},
        compiler_params=_cparams(("arbitrary",)), name="moe_dispatch",
    )(plan["n_pieces"], plan["grow"], hb, plan["dstl"], gate, jnp.zeros((rows, MOE_XW), BF16))


def _expert_kernel(te_ref, act_ref, x_ref, wg_ref, wu_ref, wo_ref, o_ref, acc, *, nf):
    i, f = pl.program_id(0), pl.program_id(1)

    @pl.when(f == 0)
    def _():
        acc[...] = jnp.zeros_like(acc)

    @pl.when(act_ref[i] == 1)
    def _():
        x = x_ref[:, 0:D]
        t = (jax.nn.silu(_dot(x, wg_ref[...])) * _dot(x, wu_ref[...])).astype(BF16)
        acc[...] += _dot(t, wo_ref[...])

    @pl.when(f == nf - 1)
    def _():
        gate = jnp.sum(x_ref[:, D:MOE_XW].astype(F32), axis=1, keepdims=True)
        o_ref[...] = (acc[...] * gate).astype(o_ref.dtype)


def _moe_experts(xs, w_in, w_out, plan, tf=896):
    rows = xs.shape[0]
    d_ff = w_out.shape[1]
    nf = d_ff // tf
    w_in, w_out = w_in.astype(BF16), w_out.astype(BF16)
    fe = lambda i, f, act: jnp.where(act[i] == 1, f, nf - 1)
    slab = lambda half: pl.BlockSpec((None, D, tf), lambda i, f, te, act: (te[i], 0, half * nf + fe(i, f, act)))
    grid_spec = pltpu.PrefetchScalarGridSpec(
        num_scalar_prefetch=2, grid=(rows // MOE_TM, nf),
        in_specs=[pl.BlockSpec((MOE_TM, MOE_XW), lambda i, f, te, act: (i, 0)),
                  slab(0), slab(1),
                  pl.BlockSpec((None, tf, D), lambda i, f, te, act: (te[i], fe(i, f, act), 0))],
        out_specs=pl.BlockSpec((MOE_TM, D), lambda i, f, te, act: (i, 0)),
        scratch_shapes=[pltpu.VMEM((MOE_TM, D), F32)])
    return pl.pallas_call(
        functools.partial(_expert_kernel, nf=nf), grid_spec=grid_spec,
        out_shape=jax.ShapeDtypeStruct((rows, D), BF16),
        compiler_params=_cparams(("parallel", "arbitrary")), name="moe_experts",
    )(plan["tile_e"], plan["active"], xs, w_in, w_in, w_out)


def _combine_kernel(np_ref, grow_ref, x_ref, dst_ref, ys_ref, *rest, n0):
    outs, (buf, sem) = rest[:-2], rest[-2:]
    b = pl.program_id(0)
    nblk = pl.num_programs(0)
    slot = b % 2

    def copy(blk, i, s):
        r = pl.multiple_of(grow_ref[blk * MOE_PIECES + i], MOE_ALIGN)
        dst = buf.at[s, pl.ds(pl.multiple_of(i * MOE_ALIGN, MOE_ALIGN), MOE_ALIGN), :]
        return pltpu.make_async_copy(ys_ref.at[pl.ds(r, MOE_ALIGN), :], dst, sem.at[s])

    def fetch(blk, s):
        lax.fori_loop(0, np_ref[blk], lambda i, c: (copy(blk, i, s).start(), c)[1], 0)

    @pl.when(b == 0)
    def _():
        buf[...] = jnp.zeros_like(buf)
        fetch(0, 0)

    @pl.when(b + 1 < nblk)
    def _():
        fetch(b + 1, 1 - slot)

    want = lax.broadcasted_iota(I32, (1, MOE_KL), 1)
    onehot = jnp.where((dst_ref[:, 0:1] == want) | (dst_ref[:, 1:2] == want), 1.0, 0.0).astype(BF16)
    lax.fori_loop(0, np_ref[b], lambda i, c: (copy(b, i, slot).wait(), c)[1], 0)
    y = x_ref[...] + _dot(onehot, buf[slot])
    if len(outs) == 1:
        outs[0][...] = y
    else:
        @pl.when(b < n0)
        def _():
            outs[0][...] = y

        @pl.when(b >= n0)
        def _():
            outs[1][...] = y


def _moe_combine(x, ys, dst_tok, plan, split_rows):
    t = x.shape[0]
    nblk = t // MOE_TB
    tile = lambda m: pl.BlockSpec((MOE_TB, D), m)
    if split_rows is None:
        n0 = nblk
        out_specs = tile(lambda b, *_: (b, 0))
        out_shape = jax.ShapeDtypeStruct((t, D), F32)
    else:
        n0 = split_rows // MOE_TB
        out_specs = [tile(lambda b, *_: (jnp.minimum(b, n0 - 1), 0)), tile(lambda b, *_: (jnp.maximum(b - n0, 0), 0))]
        out_shape = [jax.ShapeDtypeStruct((split_rows, D), F32), jax.ShapeDtypeStruct((t - split_rows, D), F32)]
    grid_spec = pltpu.PrefetchScalarGridSpec(
        num_scalar_prefetch=2, grid=(nblk,),
        in_specs=[tile(lambda b, *_: (b, 0)),
                  pl.BlockSpec((MOE_TB, 2), lambda b, *_: (b, 0)),
                  pl.BlockSpec(memory_space=pl.ANY)],
        out_specs=out_specs,
        scratch_shapes=[pltpu.VMEM((2, MOE_KL, D), BF16), pltpu.SemaphoreType.DMA((2,))])
    return pl.pallas_call(
        functools.partial(_combine_kernel, n0=n0), grid_spec=grid_spec, out_shape=out_shape,
        compiler_params=_cparams(("arbitrary",)), name="moe_combine",
    )(plan["n_pieces"], plan["grow"], x, dst_tok, ys)


def _moe_layer(x, norm_g, w_router, w_in, w_out, split_rows=None):
    t = x.shape[0]
    hb, sel, gate, counts = _moe_route(x, norm_g, w_router)
    plan = _moe_plan(sel, counts, t)
    xs = _moe_dispatch(hb, gate, plan)
    ys = _moe_experts(xs, w_in, w_out, plan)
    dst_tok = plan["dstl"].transpose(0, 2, 1).reshape(t, 2)
    return _moe_combine(x, ys, dst_tok, plan, split_rows)


def kernel(x_prompt, x_sample, state_l0_s5_re, state_l0_s5_im, cache_l1_latent, cache_l1_krope, state_l2_pool, state_l3_s5_re, state_l3_s5_im, page_table, l0_norm_mix, l0_s5_lambda_re, l0_s5_lambda_im, l0_s5_log_dt, l0_s5_b_re, l0_s5_b_im, l0_s5_c_re, l0_s5_c_im, l0_s5_d, l0_s5_w_glu, l0_norm_ffn, l0_ffn_w_in, l0_ffn_w_out, l1_norm_mix, l1_mla_w_in, l1_mla_norm_q, l1_mla_norm_kv, l1_mla_w_q, l1_mla_w_uk, l1_mla_w_uv, l1_mla_qn_nope, l1_mla_qn_rope, l1_mla_kn_nope, l1_mla_kn_rope, l1_mla_w_o, l1_norm_ffn, l1_moe_w_router, l1_moe_w_in, l1_moe_w_out, l2_norm_mix, l2_pool_w, l2_pool_scale, l2_norm_ffn, l2_ffn_w_in, l2_ffn_w_out, l3_norm_mix, l3_s5_lambda_re, l3_s5_lambda_im, l3_s5_log_dt, l3_s5_b_re, l3_s5_b_im, l3_s5_c_re, l3_s5_c_im, l3_s5_d, l3_s5_w_glu, l3_norm_ffn, l3_moe_w_router, l3_moe_w_in, l3_moe_w_out):
    n_bp, seq_p, _ = x_prompt.shape
    n_bs, seq_s, _ = x_sample.shape
    dims = (n_bp, seq_p, n_bs, seq_s)
    tp, ts = n_bp * seq_p, n_bs * seq_s
    past_len = page_table.shape[1] * cache_l1_latent.shape[1]

    x, st0 = _s5_layer((x_prompt.reshape(tp, D), x_sample.reshape(ts, D)), l0_norm_mix, state_l0_s5_re, state_l0_s5_im,
                       (l0_s5_lambda_re, l0_s5_lambda_im, l0_s5_log_dt, l0_s5_b_re, l0_s5_b_im,
                        l0_s5_c_re, l0_s5_c_im, l0_s5_d, l0_s5_w_glu), dims)
    x = _ffn_dense(x, l0_norm_ffn, l0_ffn_w_in, l0_ffn_w_out)

    x, st1 = _mla_layer(x, l1_norm_mix, cache_l1_latent, cache_l1_krope, page_table,
                        (l1_mla_w_in, l1_mla_norm_q, l1_mla_norm_kv, l1_mla_w_q, l1_mla_w_uk, l1_mla_w_uv,
                         l1_mla_qn_nope, l1_mla_qn_rope, l1_mla_kn_nope, l1_mla_kn_rope, l1_mla_w_o), dims)
    x = _moe_layer(x, l1_norm_ffn, l1_moe_w_router, l1_moe_w_in, l1_moe_w_out)

    x, st2 = _pool_layer(x, l2_norm_mix, state_l2_pool, (l2_pool_w, l2_pool_scale), dims, past_len)
    x, h_p, h_s = _ffn_dense(x, l2_norm_ffn, l2_ffn_w_in, l2_ffn_w_out, next_norm=l3_norm_mix, split_rows=tp)

    x, st3 = _s5_layer((x,), l3_norm_mix, state_l3_s5_re, state_l3_s5_im,
                       (l3_s5_lambda_re, l3_s5_lambda_im, l3_s5_log_dt, l3_s5_b_re, l3_s5_b_im,
                        l3_s5_c_re, l3_s5_c_im, l3_s5_d, l3_s5_w_glu), dims, h_parts=(h_p, h_s))
    y_p, y_s = _moe_layer(x, l3_norm_ffn, l3_moe_w_router, l3_moe_w_in, l3_moe_w_out, split_rows=tp)

    return (y_p.reshape(n_bp, seq_p, D), y_s.reshape(n_bs, seq_s, D)) + st0 + st1 + st2 + st3
```
